```python
import math
import jax, jax.numpy as jnp
from jax import lax
import numpy as np

D_MODEL = 2048
BATCH = 4
SEQ = 2048
DEPTH = 2
DEC_BATCH = 128
DEC_SEQ = 1
PAST_LEN = 2048
PAGE_SIZE = 128

N_MIXERS = 4
G_W = D_MODEL // N_MIXERS
POOL_WINDOWS = (2, 4, 8, 16)
POOL_GROUP = G_W // len(POOL_WINDOWS)
POOL_BUF = max(POOL_WINDOWS) - 1
SGU_HEADS = 4
SGU_HEAD_DIM = G_W // SGU_HEADS
SGU_CHUNK = 128
SB_HEADS = 4
SB_HEAD_DIM = G_W // SB_HEADS
SB_BLOCK = 128
SB_BIAS_INIT = -6.0
DN_HEADS = 4
DN_HEAD_DIM = G_W // DN_HEADS
DN_CHUNK = 64
CONV_W = 4
D_FF = 256 * ((8 * D_MODEL // 3 + 255) // 256)
D_IN = 10 * G_W + 2 * DN_HEADS
SPLIT_POINTS = (G_W, 2 * G_W, 3 * G_W, 4 * G_W, 5 * G_W, 6 * G_W, 9 * G_W, 10 * G_W, 10 * G_W + DN_HEADS)
DEEPNORM_ALPHA = (2.0 * DEPTH) ** 0.25
DEEPNORM_BETA = (8.0 * DEPTH) ** -0.25
LN_EPS = 1e-5
NORM_EPS = 1e-6

kernel_name = 'pool_gmlp_stickbreak_deltanet_hybrid_step'


def layer_norm(x, g, b):
    xf = x.astype(jnp.float32)
    mu = jnp.mean(xf, -1, keepdims=True)
    var = jnp.mean(jnp.square(xf - mu), -1, keepdims=True)
    return ((xf - mu) * lax.rsqrt(var + LN_EPS) * g.astype(jnp.float32) + b.astype(jnp.float32)).astype(x.dtype)


def swiglu(x, w_in, w_out):
    gate, up = jnp.split(x @ w_in, 2, axis=-1)
    return (jax.nn.silu(gate) * up) @ w_out


def l2_normalize(x):
    return x * lax.rsqrt(jnp.sum(x * x, -1, keepdims=True) + NORM_EPS)


def pool_mixer(a, buf, pos0, pool_w, pool_scale):
    B, T, _ = a.shape
    af = a.astype(jnp.float32)
    ext = jnp.concatenate([buf.astype(jnp.float32), af], axis=1)
    csum = jnp.concatenate([jnp.zeros((B, 1, G_W), jnp.float32), jnp.cumsum(ext, axis=1)], axis=1)
    pos = pos0 + jnp.arange(T)
    end = csum[:, POOL_BUF + 1:]
    diffs = []
    for gi, w in enumerate(POOL_WINDOWS):
        sl = slice(gi * POOL_GROUP, (gi + 1) * POOL_GROUP)
        start = csum[:, POOL_BUF + 1 - w: POOL_BUF + 1 - w + T, sl]
        cnt = jnp.minimum(pos + 1, w).astype(jnp.float32)[None, :, None]
        diffs.append((end[..., sl] - start) / cnt - af[..., sl])
    d = jnp.stack(diffs, axis=2)
    y = jnp.einsum('btgc,gcd->btgd', d, pool_w.astype(jnp.float32)).reshape(B, T, G_W)
    y = y * pool_scale.astype(jnp.float32)
    return y, ext[:, T:].astype(buf.dtype)


def spatial_gating(u, v, sgu_w, sgu_b):
    B, T, _ = u.shape
    n = -(-T // SGU_CHUNK)
    pad = n * SGU_CHUNK - T
    vc = jnp.pad(v, ((0, 0), (0, pad), (0, 0))).reshape(B, n, SGU_CHUNK, SGU_HEADS, SGU_HEAD_DIM)
    causal = jnp.tril(jnp.ones((SGU_CHUNK, SGU_CHUNK), dtype=bool))
    w = jnp.where(causal, sgu_w, 0)
    mixed = jnp.einsum('hts,bnshd->bnthd', w, vc) + sgu_b.T[None, None, :, :, None]
    mixed = mixed.reshape(B, n * SGU_CHUNK, G_W)[:, :T]
    return u * mixed


def stick_breaking(q, k, v, sb_bias, pos0):
    B, T, H, Dh = q.shape
    Tk = k.shape[1]
    blk = min(SB_BLOCK, T)
    n_blk = -(-T // blk)
    qp = jnp.pad(q, ((0, 0), (0, n_blk * blk - T), (0, 0), (0, 0)))
    q_blocks = jnp.swapaxes(qp.reshape(B, n_blk, blk, H, Dh), 0, 1)
    q_pos = (pos0 + jnp.arange(n_blk * blk)).reshape(n_blk, blk)
    k_pos = jnp.arange(Tk)
    scale = SB_HEAD_DIM ** -0.5
    bias = sb_bias.astype(jnp.float32)[None, :, None, None]

    def one_block(args):
        qb, qpos = args
        z = jnp.einsum('bqhd,bkhd->bhqk', qb, k, preferred_element_type=jnp.float32) * scale + bias
        mask = k_pos[None, :] < qpos[:, None]
        log_fail = jnp.where(mask, jax.nn.log_sigmoid(-z), 0.0)
        later = lax.cumsum(log_fail, axis=3, reverse=True) - log_fail
        att = jnp.where(mask, jnp.exp(jax.nn.log_sigmoid(z) + later), 0.0)
        return jnp.einsum('bhqk,bkhd->bqhd', att.astype(v.dtype), v, preferred_element_type=jnp.float32)

    o = lax.map(one_block, (q_blocks, q_pos))
    return jnp.swapaxes(o, 0, 1).reshape(B, n_blk * blk, H, Dh)[:, :T]


def chunked_gated_delta(q, k, v, g, beta, s0):
    B, T, H, _ = q.shape
    DV = v.shape[-1]
    n = -(-T // DN_CHUNK)
    pad = n * DN_CHUNK - T

    def to_chunks(a):
        a = jnp.pad(a, ((0, 0), (0, pad)) + ((0, 0),) * (a.ndim - 2))
        a = a.reshape((B, n, DN_CHUNK) + a.shape[2:])
        return jnp.moveaxis(a, 3, 1)

    qc, kc, vc, gc, bc = (to_chunks(a) for a in (q, k, v, g, beta))
    G = jnp.cumsum(gc, axis=-1)
    idx = jnp.arange(DN_CHUNK)
    incl = idx[:, None] >= idx[None, :]
    strict = idx[:, None] > idx[None, :]
    diff = G[..., :, None] - G[..., None, :]
    decay = jnp.where(incl, jnp.exp(jnp.where(incl, diff, 0.0)), 0.0)
    kk = jnp.einsum('bhnid,bhnjd->bhnij', kc, kc)
    a_mat = jnp.where(strict, bc[..., :, None] * kk * decay, 0.0)
    lower = a_mat + jnp.eye(DN_CHUNK, dtype=a_mat.dtype)
    rhs = jnp.concatenate([bc[..., None] * vc, (bc * jnp.exp(G))[..., None] * kc], axis=-1)
    sol = lax.linalg.triangular_solve(lower, rhs, left_side=True, lower=True, unit_diagonal=True)
    u_base, w_mat = sol[..., :DV], sol[..., DV:]
    qk = jnp.einsum('bhnid,bhnjd->bhnij', qc, kc) * decay
    q_dec = qc * jnp.exp(G)[..., None]
    k_dec = kc * jnp.exp(G[..., -1:] - G)[..., None]
    g_end = jnp.exp(G[..., -1])
    xs = tuple(jnp.moveaxis(a, 2, 0) for a in (u_base, w_mat, qk, q_dec, k_dec, g_end))

    def step(S, inp):
        u_b, w_c, qk_c, qd_c, kd_c, ge_c = inp
        u = u_b - jnp.einsum('bhck,bhkv->bhcv', w_c, S)
        o = jnp.einsum('bhck,bhkv->bhcv', qd_c, S) + jnp.einsum('bhcs,bhsv->bhcv', qk_c, u)
        S = S * ge_c[..., None, None] + jnp.einsum('bhck,bhcv->bhkv', kd_c, u)
        return S, o

    s_end, o = lax.scan(step, s0, xs)
    o = jnp.moveaxis(jnp.moveaxis(o, 0, 2), 1, 3).reshape(B, n * DN_CHUNK, H, DV)[:, :T]
    return o, s_end


def gated_deltanet(qkv_raw, z, b_raw, a_raw, conv_buf, s0, conv_w, a_log, dt_bias, norm_g):
    B, T, _ = qkv_raw.shape
    xc = jnp.concatenate([conv_buf.astype(qkv_raw.dtype), qkv_raw], axis=1)
    conv = sum(xc[:, j:j + T].astype(jnp.float32) * conv_w[j].astype(jnp.float32) for j in range(CONV_W))
    act = jax.nn.silu(conv)
    q, k, v = jnp.split(act, 3, axis=-1)
    q = l2_normalize(q.reshape(B, T, DN_HEADS, DN_HEAD_DIM)) * (DN_HEAD_DIM ** -0.5)
    k = l2_normalize(k.reshape(B, T, DN_HEADS, DN_HEAD_DIM))
    v = v.reshape(B, T, DN_HEADS, DN_HEAD_DIM)
    beta = jax.nn.sigmoid(b_raw.astype(jnp.float32))
    g = -jnp.exp(a_log.astype(jnp.float32)) * jax.nn.softplus(a_raw.astype(jnp.float32) + dt_bias.astype(jnp.float32))
    o, s_end = chunked_gated_delta(q, k, v, g, beta, s0.astype(jnp.float32))
    o = o * lax.rsqrt(jnp.mean(o * o, -1, keepdims=True) + NORM_EPS) * norm_g.astype(jnp.float32)
    o = o * jax.nn.silu(z.astype(jnp.float32).reshape(B, T, DN_HEADS, DN_HEAD_DIM))
    return o.reshape(B, T, G_W), xc[:, T:], s_end


def trunk_layer(x, pool_buf, conv_buf, s0, k_past, v_past,
                ln_g, ln_b, w_ffn1_in, w_ffn1_out, w_ffn2_in, w_ffn2_out, w_in, w_out,
                pool_w, pool_scale, sgu_w, sgu_b, sb_bias, dn_conv_w, dn_a_log, dn_dt_bias, dn_norm_g):
    B, T, _ = x.shape
    pos0 = k_past.shape[1]
    h = layer_norm(DEEPNORM_ALPHA * x + 0.5 * swiglu(x, w_ffn1_in, w_ffn1_out), ln_g[0], ln_b[0])
    proj = h @ w_in
    p_pool, s_u, s_v, sb_q, sb_k, sb_v, dn_qkv, dn_z, dn_b, dn_a = jnp.split(proj, SPLIT_POINTS, axis=-1)
    y_pool, new_pool = pool_mixer(p_pool, pool_buf, pos0, pool_w, pool_scale)
    y_sgu = spatial_gating(s_u, s_v, sgu_w, sgu_b)
    k_new = sb_k.reshape(B, T, SB_HEADS, SB_HEAD_DIM)
    v_new = sb_v.reshape(B, T, SB_HEADS, SB_HEAD_DIM)
    k_all = jnp.concatenate([k_past.astype(k_new.dtype), k_new], axis=1)
    v_all = jnp.concatenate([v_past.astype(v_new.dtype), v_new], axis=1)
    y_sb = stick_breaking(sb_q.reshape(B, T, SB_HEADS, SB_HEAD_DIM), k_all, v_all, sb_bias, pos0).reshape(B, T, G_W)
    y_dn, new_conv, new_s = gated_deltanet(dn_qkv, dn_z, dn_b, dn_a, conv_buf, s0,
                                           dn_conv_w, dn_a_log, dn_dt_bias, dn_norm_g)
    mixed = jnp.concatenate([y_pool.astype(h.dtype), y_sgu.astype(h.dtype),
                             y_sb.astype(h.dtype), y_dn.astype(h.dtype)], axis=-1) @ w_out
    h = layer_norm(DEEPNORM_ALPHA * h + mixed, ln_g[1], ln_b[1])
    h = layer_norm(DEEPNORM_ALPHA * h + 0.5 * swiglu(h, w_ffn2_in, w_ffn2_out), ln_g[2], ln_b[2])
    return h, k_new, v_new, new_pool, new_conv, new_s, s_v


def setup_inputs(seed: int = 0) -> dict:
    key = jax.random.key(seed)
    ks = jax.random.split(key, 25)
    f32 = jnp.float32
    n_pages = PAST_LEN // PAGE_SIZE
    n_used = DEC_BATCH * n_pages
    n_phys = n_used + n_used // 4

    def nrm(k, shape, scale):
        return scale * jax.random.normal(k, shape, f32)

    page_table = jax.random.permutation(ks[4], n_phys)[:n_used].reshape(DEC_BATCH, n_pages).astype(jnp.int32)
    dt = jnp.exp(jax.random.uniform(ks[21], (DEPTH, DN_HEADS), f32, math.log(1e-3), math.log(1e-1)))
    return {
        'x_prompt': nrm(ks[0], (BATCH, SEQ, D_MODEL), 1.0),
        'x_sample': nrm(ks[1], (DEC_BATCH, DEC_SEQ, D_MODEL), 1.0),
        'cache_k': nrm(ks[2], (DEPTH, n_phys, PAGE_SIZE, SB_HEADS, SB_HEAD_DIM), 1.0),
        'cache_v': nrm(ks[3], (DEPTH, n_phys, PAGE_SIZE, SB_HEADS, SB_HEAD_DIM), 1.0),
        'page_table': page_table,
        'state_pool': nrm(ks[5], (DEPTH, DEC_BATCH, POOL_BUF, G_W), 1.0),
        'state_conv': nrm(ks[6], (DEPTH, DEC_BATCH, CONV_W - 1, 3 * G_W), 1.0),
        'state_delta': nrm(ks[7], (DEPTH, DEC_BATCH, DN_HEADS, DN_HEAD_DIM, DN_HEAD_DIM), 0.1),
        'ln_g': 1.0 + nrm(ks[8], (DEPTH, 3, D_MODEL), 0.02),
        'ln_b': nrm(ks[9], (DEPTH, 3, D_MODEL), 0.02),
        'w_ffn1_in': nrm(ks[10], (DEPTH, D_MODEL, 2 * D_FF), D_MODEL ** -0.5),
        'w_ffn1_out': nrm(ks[11], (DEPTH, D_FF, D_MODEL), DEEPNORM_BETA * D_FF ** -0.5),
        'w_ffn2_in': nrm(ks[12], (DEPTH, D_MODEL, 2 * D_FF), D_MODEL ** -0.5),
        'w_ffn2_out': nrm(ks[13], (DEPTH, D_FF, D_MODEL), DEEPNORM_BETA * D_FF ** -0.5),
        'w_in': nrm(ks[14], (DEPTH, D_MODEL, D_IN), D_MODEL ** -0.5),
        'w_out': nrm(ks[15], (DEPTH, N_MIXERS * G_W, D_MODEL), DEEPNORM_BETA * (N_MIXERS * G_W) ** -0.5),
        'pool_w': nrm(ks[16], (DEPTH, len(POOL_WINDOWS), POOL_GROUP, POOL_GROUP), POOL_GROUP ** -0.5),
        'pool_scale': 1.0 + nrm(ks[17], (DEPTH, G_W), 0.02),
        'sgu_w': nrm(ks[18], (DEPTH, SGU_HEADS, SGU_CHUNK, SGU_CHUNK), SGU_CHUNK ** -0.5),
        'sgu_b': 1.0 + nrm(ks[19], (DEPTH, SGU_HEADS, SGU_CHUNK), 0.02),
        'sb_bias': SB_BIAS_INIT + nrm(ks[24], (DEPTH, SB_HEADS), 0.1),
        'dn_conv_w': nrm(ks[20], (DEPTH, CONV_W, 3 * G_W), CONV_W ** -0.5),
        'dn_a_log': jnp.log(jax.random.uniform(ks[22], (DEPTH, DN_HEADS), f32, 1.0, 16.0)),
        'dn_dt_bias': dt + jnp.log(-jnp.expm1(-dt)),
        'dn_norm_g': 1.0 + nrm(ks[23], (DEPTH, DN_HEAD_DIM), 0.02),
    }


def reference(x_prompt, x_sample, cache_k, cache_v, page_table, state_pool, state_conv, state_delta,
              ln_g, ln_b, w_ffn1_in, w_ffn1_out, w_ffn2_in, w_ffn2_out, w_in, w_out,
              pool_w, pool_scale, sgu_w, sgu_b, sb_bias, dn_conv_w, dn_a_log, dn_dt_bias, dn_norm_g):
    b_p = x_prompt.shape[0]
    b_s = x_sample.shape[0]
    n_pages = page_table.shape[1]
    past = n_pages * cache_k.shape[2]
    dt = x_prompt.dtype
    kp_l, vp_l, poolp_l, convp_l, sp_l = [], [], [], [], []
    ks_l, vs_l, pools_l, convs_l, ss_l, sgus_l = [], [], [], [], [], []
    hp, hs = x_prompt, x_sample
    for l in range(DEPTH):
        weights = (ln_g[l], ln_b[l], w_ffn1_in[l], w_ffn1_out[l], w_ffn2_in[l], w_ffn2_out[l],
                   w_in[l], w_out[l], pool_w[l], pool_scale[l], sgu_w[l], sgu_b[l], sb_bias[l],
                   dn_conv_w[l], dn_a_log[l], dn_dt_bias[l], dn_norm_g[l])
        hp, kp, vp, poolp, convp, sp, _ = trunk_layer(
            hp,
            jnp.zeros((b_p, POOL_BUF, G_W), dt),
            jnp.zeros((b_p, CONV_W - 1, 3 * G_W), dt),
            jnp.zeros((b_p, DN_HEADS, DN_HEAD_DIM, DN_HEAD_DIM), jnp.float32),
            jnp.zeros((b_p, 0, SB_HEADS, SB_HEAD_DIM), dt),
            jnp.zeros((b_p, 0, SB_HEADS, SB_HEAD_DIM), dt),
            *weights)
        k_past = cache_k[l][page_table].reshape(b_s, past, SB_HEADS, SB_HEAD_DIM)
        v_past = cache_v[l][page_table].reshape(b_s, past, SB_HEADS, SB_HEAD_DIM)
        hs, kS, vS, poolS, convS, sS, sguS = trunk_layer(
            hs, state_pool[l], state_conv[l], state_delta[l], k_past, v_past, *weights)
        kp_l.append(kp); vp_l.append(vp); poolp_l.append(poolp); convp_l.append(convp); sp_l.append(sp)
        ks_l.append(kS); vs_l.append(vS); pools_l.append(poolS); convs_l.append(convS); ss_l.append(sS)
        sgus_l.append(sguS)
    return (hp, hs,
            jnp.stack(kp_l), jnp.stack(vp_l), jnp.stack(poolp_l), jnp.stack(convp_l), jnp.stack(sp_l),
            jnp.stack(ks_l), jnp.stack(vs_l), jnp.stack(pools_l), jnp.stack(convs_l), jnp.stack(ss_l),
            jnp.stack(sgus_l))
```

```python
import functools

import jax
import jax.numpy as jnp
from jax import lax
from jax.experimental import pallas as pl
from jax.experimental.pallas import tpu as pltpu

F32 = jnp.float32
BF16 = jnp.bfloat16
HIGHEST = lax.Precision.HIGHEST

HEADS = 4
HEAD_DIM = 128
G_W = HEADS * HEAD_DIM
POOL_WINDOWS = (2, 4, 8, 16)
POOL_BUF = max(POOL_WINDOWS) - 1
POOL_HALO = 16
SGU_CHUNK = 128
DN_CHUNK = 64
CONV_W = 4
CONV_HALO = 8
LN_EPS = 1e-5
NORM_EPS = 1e-6
VMEM_LIMIT_BYTES = 48 * 1024 * 1024

COL_POOL, COL_SGU_U, COL_SGU_V, COL_SB_Q, COL_SB_K, COL_SB_V, COL_DN_QKV, COL_DN_Z = 0, 1, 2, 3, 4, 5, 6, 9
N_MAIN_COLS = 10 * G_W
BG_LANES = 128


def _params(*sem):
    return pltpu.CompilerParams(dimension_semantics=sem, vmem_limit_bytes=VMEM_LIMIT_BYTES)


def _hdot(a, b):
    return jnp.dot(a, b, precision=HIGHEST, preferred_element_type=F32)


def _hdot_nt(a, b):
    return lax.dot_general(a, b, (((1,), (1,)), ((), ())), precision=HIGHEST, preferred_element_type=F32)


def _bdot(a, b):
    return jnp.dot(a.astype(BF16), b.astype(BF16), preferred_element_type=F32)


def _bdot_nt(a, b):
    return lax.dot_general(a.astype(BF16), b.astype(BF16), (((1,), (1,)), ((), ())),
                           preferred_element_type=F32)


def _silu(x):
    return x * jax.nn.sigmoid(x)


def _softplus(x):
    return jnp.maximum(x, 0.0) + jnp.log1p(jnp.exp(-jnp.abs(x)))


def _layer_norm(y, g, b):
    mu = jnp.mean(y, -1, keepdims=True)
    d = y - mu
    var = jnp.mean(d * d, -1, keepdims=True)
    return d * lax.rsqrt(var + LN_EPS) * g + b


def _split_bf16(x):
    hi = x.astype(BF16)
    lo = (x - hi.astype(F32)).astype(BF16)
    return hi, lo


def _ffn_ln_kernel(x_ref, wg_ref, wu_ref, wo_ref, g_ref, b_ref, o_ref, xb_ref, acc_ref, *, alpha):
    f = pl.program_id(1)

    @pl.when(f == 0)
    def _():
        xb_ref[...] = x_ref[...].astype(BF16)
        acc_ref[...] = jnp.zeros_like(acc_ref)

    xb = xb_ref[...]
    gate = jnp.dot(xb, wg_ref[...], preferred_element_type=F32)
    up = jnp.dot(xb, wu_ref[...], preferred_element_type=F32)
    act = (_silu(gate) * up).astype(BF16)
    acc_ref[...] += jnp.dot(act, wo_ref[...], preferred_element_type=F32)

    @pl.when(f == pl.num_programs(1) - 1)
    def _():
        o_ref[...] = _layer_norm(alpha * x_ref[...] + 0.5 * acc_ref[...], g_ref[...], b_ref[...])


def _ffn_ln(x, w_in, w_out, g, b, *, alpha, tm, tf):
    n, d = x.shape
    dff = w_out.shape[0]
    nf = dff // tf
    return pl.pallas_call(
        functools.partial(_ffn_ln_kernel, alpha=alpha),
        out_shape=jax.ShapeDtypeStruct((n, d), F32),
        grid=(n // tm, nf),
        in_specs=[
            pl.BlockSpec((tm, d), lambda i, f: (i, 0)),
            pl.BlockSpec((d, tf), lambda i, f: (0, f)),
            pl.BlockSpec((d, tf), lambda i, f: (0, f + nf)),
            pl.BlockSpec((tf, d), lambda i, f: (f, 0)),
            pl.BlockSpec((1, d), lambda i, f: (0, 0)),
            pl.BlockSpec((1, d), lambda i, f: (0, 0)),
        ],
        out_specs=pl.BlockSpec((tm, d), lambda i, f: (i, 0)),
        scratch_shapes=[pltpu.VMEM((tm, d), BF16), pltpu.VMEM((tm, d), F32)],
        compiler_params=_params("parallel", "arbitrary"),
        name="ffn_ln",
    )(x, w_in, w_in, w_out, g.reshape(1, d), b.reshape(1, d))


def _in_proj_kernel(h_ref, w_ref, wt_ref, o_ref, bg_ref, hb_ref):
    @pl.when(pl.program_id(1) == 0)
    def _():
        hb = h_ref[...].astype(BF16)
        hb_ref[...] = hb
        bg_ref[...] = jnp.dot(hb, wt_ref[...], preferred_element_type=F32)

    o_ref[...] = jnp.dot(hb_ref[...], w_ref[...], preferred_element_type=F32)


def _in_proj(h, w_main, w_tail, *, tm, tn):
    n, d = h.shape
    return pl.pallas_call(
        _in_proj_kernel,
        out_shape=(jax.ShapeDtypeStruct((n, N_MAIN_COLS), F32), jax.ShapeDtypeStruct((n, BG_LANES), F32)),
        grid=(n // tm, N_MAIN_COLS // tn),
        in_specs=[
            pl.BlockSpec((tm, d), lambda i, j: (i, 0)),
            pl.BlockSpec((d, tn), lambda i, j: (0, j)),
            pl.BlockSpec((d, BG_LANES), lambda i, j: (0, 0)),
        ],
        out_specs=(pl.BlockSpec((tm, tn), lambda i, j: (i, j)),
                   pl.BlockSpec((tm, BG_LANES), lambda i, j: (i, 0))),
        scratch_shapes=[pltpu.VMEM((tm, d), BF16)],
        compiler_params=_params("parallel", "arbitrary"),
        name="in_proj",
    )(h, w_main, w_tail)


def _out_proj_ln_kernel(h_ref, y0_ref, y1_ref, y2_ref, y3_ref, w_ref, g_ref, b_ref, o_ref, *, alpha):
    acc = None
    for m, y_ref in enumerate((y0_ref, y1_ref, y2_ref, y3_ref)):
        part = jnp.dot(y_ref[...].astype(BF16), w_ref[m * G_W:(m + 1) * G_W, :], preferred_element_type=F32)
        acc = part if acc is None else acc + part
    o_ref[...] = _layer_norm(alpha * h_ref[...] + acc, g_ref[...], b_ref[...])


def _out_proj_ln(h, ys, w_out, g, b, *, alpha, tm):
    n, d = h.shape
    y_spec = pl.BlockSpec((tm, G_W), lambda i: (i, 0))
    return pl.pallas_call(
        functools.partial(_out_proj_ln_kernel, alpha=alpha),
        out_shape=jax.ShapeDtypeStruct((n, d), F32),
        grid=(n // tm,),
        in_specs=[pl.BlockSpec((tm, d), lambda i: (i, 0)), y_spec, y_spec, y_spec, y_spec,
                  pl.BlockSpec((4 * G_W, d), lambda i: (0, 0)),
                  pl.BlockSpec((1, d), lambda i: (0, 0)),
                  pl.BlockSpec((1, d), lambda i: (0, 0))],
        out_specs=pl.BlockSpec((tm, d), lambda i: (i, 0)),
        compiler_params=_params("parallel"),
        name="out_proj_ln",
    )(h, *ys, w_out, g.reshape(1, d), b.reshape(1, d))


def _pool_sgu_kernel(cur_ref, halo_ref, u_ref, v_ref, pw_ref, ps_ref, sw_ref, sbc_ref,
                     yp_ref, ys_ref, ext_ref, *, chunks_per_seq):
    c = pl.program_id(0) % chunks_per_seq
    cur = cur_ref[...]
    ext_ref[0:POOL_HALO, :] = jnp.where(c == 0, 0.0, halo_ref[...])
    ext_ref[POOL_HALO:, :] = cur
    rows = SGU_CHUNK
    pos = c * rows + lax.broadcasted_iota(jnp.int32, (rows, 1), 0)
    for gi, w in enumerate(POOL_WINDOWS):
        cols = slice(gi * HEAD_DIM, (gi + 1) * HEAD_DIM)
        s = cur[:, cols]
        for j in range(1, w):
            s = s + ext_ref[POOL_HALO - j:POOL_HALO - j + rows, cols]
        cnt = jnp.minimum(pos + 1, w).astype(F32)
        d = s / cnt - cur[:, cols]
        yp_ref[:, cols] = _hdot(d, pw_ref[gi]) * ps_ref[:, cols]

    r = lax.broadcasted_iota(jnp.int32, (rows, rows), 0)
    s_ = lax.broadcasted_iota(jnp.int32, (rows, rows), 1)
    causal = r >= s_
    for h in range(HEADS):
        cols = slice(h * HEAD_DIM, (h + 1) * HEAD_DIM)
        wm = jnp.where(causal, sw_ref[h], 0.0)
        mixed = _hdot(wm, v_ref[:, cols]) + sbc_ref[:, h:h + 1]
        ys_ref[:, cols] = u_ref[:, cols] * mixed


def _pool_sgu(proj, pool_w, pool_scale, sgu_w, sgu_b, *, seq):
    n = proj.shape[0]
    rows = SGU_CHUNK
    halo_blocks = rows // POOL_HALO
    blk = lambda col: pl.BlockSpec((rows, G_W), lambda g: (g, col))
    whole = lambda a: pl.BlockSpec(a.shape, lambda g: (0,) * a.ndim)
    ps = pool_scale.reshape(1, G_W)
    sbc = sgu_b.T
    return pl.pallas_call(
        functools.partial(_pool_sgu_kernel, chunks_per_seq=seq // rows),
        out_shape=(jax.ShapeDtypeStruct((n, G_W), F32), jax.ShapeDtypeStruct((n, G_W), F32)),
        grid=(n // rows,),
        in_specs=[blk(COL_POOL),
                  pl.BlockSpec((POOL_HALO, G_W), lambda g: (jnp.maximum(g * halo_blocks - 1, 0), COL_POOL)),
                  blk(COL_SGU_U), blk(COL_SGU_V), whole(pool_w), whole(ps), whole(sgu_w), whole(sbc)],
        out_specs=(pl.BlockSpec((rows, G_W), lambda g: (g, 0)), pl.BlockSpec((rows, G_W), lambda g: (g, 0))),
        scratch_shapes=[pltpu.VMEM((POOL_HALO + rows, G_W), F32)],
        compiler_params=_params("parallel"),
        name="pool_sgu",
    )(proj, proj, proj, proj, pool_w, ps, sgu_w, sbc)


def _sb_attn_kernel(bias_ref, q_ref, k_ref, v_ref, o_ref, *, blk, scale):
    h = pl.program_id(1)
    qi = pl.program_id(2)
    bias = bias_ref[h]
    q = q_ref[...].astype(BF16)
    row = lax.broadcasted_iota(jnp.int32, (blk, blk), 0)
    col = lax.broadcasted_iota(jnp.int32, (blk, blk), 1)
    after = (row > col).astype(BF16)

    def body(step, carry):
        o, run = carry
        j = qi - step
        start = pl.multiple_of(j * blk, blk)
        kb = k_ref[pl.ds(start, blk), :]
        vb = v_ref[pl.ds(start, blk), :]
        z = _bdot_nt(q, kb) * scale + bias
        sp = _softplus(z)
        valid = (j * blk + col) < (qi * blk + row)
        log_fail = jnp.where(valid, -sp, 0.0)
        hi, lo = _split_bf16(log_fail)
        later = (jnp.dot(hi, after, preferred_element_type=F32)
                 + jnp.dot(lo, after, preferred_element_type=F32) + run)
        att = jnp.where(valid, jnp.exp(z - sp + later), 0.0)
        o = o + _bdot(att, vb)
        run = run + jnp.sum(log_fail, axis=-1, keepdims=True)
        return o, run

    o, _ = lax.fori_loop(0, qi + 1, body,
                         (jnp.zeros((blk, HEAD_DIM), F32), jnp.zeros((blk, 1), F32)))
    o_ref[...] = o


def _sb_attn(proj, sb_bias, *, batch, seq, blk):
    n = proj.shape[0]
    nq = seq // blk
    cpb = G_W // HEAD_DIM
    return pl.pallas_call(
        functools.partial(_sb_attn_kernel, blk=blk, scale=HEAD_DIM ** -0.5),
        out_shape=jax.ShapeDtypeStruct((n, G_W), F32),
        grid=(batch, HEADS, nq),
        in_specs=[pl.BlockSpec(memory_space=pltpu.SMEM),
                  pl.BlockSpec((blk, HEAD_DIM), lambda b, h, i: (b * nq + i, COL_SB_Q * cpb + h)),
                  pl.BlockSpec((seq, HEAD_DIM), lambda b, h, i: (b, COL_SB_K * cpb + h)),
                  pl.BlockSpec((seq, HEAD_DIM), lambda b, h, i: (b, COL_SB_V * cpb + h))],
        out_specs=pl.BlockSpec((blk, HEAD_DIM), lambda b, h, i: (b * nq + i, h)),
        compiler_params=_params("parallel", "parallel", "arbitrary"),
        name="sb_attn",
    )(sb_bias, proj, proj, proj)


def _sb_decode_kernel(pt_ref, q_ref, bias_ref, *refs, n_pages, page, scale):
    del pt_ref
    k_refs = refs[:n_pages]
    v_refs = refs[n_pages:2 * n_pages]
    o_ref = refs[2 * n_pages]
    rows = 8
    q = q_ref[0]
    rid = lax.broadcasted_iota(jnp.int32, (rows, G_W), 0)
    cid = lax.broadcasted_iota(jnp.int32, (rows, G_W), 1)
    own = lax.shift_right_logical(cid, HEAD_DIM.bit_length() - 1) == rid
    qbd = jnp.where(own, jnp.broadcast_to(q, (rows, G_W)), 0.0).astype(BF16)
    bias = bias_ref[...]
    r = lax.broadcasted_iota(jnp.int32, (page, page), 0)
    c = lax.broadcasted_iota(jnp.int32, (page, page), 1)
    after = (r > c).astype(BF16)

    zs, sps, inblk, tot = [], [], [], []
    for p in range(n_pages):
        z = _bdot_nt(qbd, k_refs[p][0]) * scale + bias
        sp = _softplus(z)
        hi, lo = _split_bf16(-sp)
        inblk.append(jnp.dot(hi, after, preferred_element_type=F32)
                     + jnp.dot(lo, after, preferred_element_type=F32))
        tot.append(jnp.sum(-sp, axis=-1, keepdims=True))
        zs.append(z)
        sps.append(sp)
    run = jnp.zeros((rows, 1), F32)
    acc = jnp.zeros((rows, G_W), F32)
    for p in range(n_pages - 1, -1, -1):
        att = jnp.exp(zs[p] - sps[p] + inblk[p] + run)
        acc = acc + _bdot(att, v_refs[p][0])
        run = run + tot[p]
    o_ref[0] = jnp.sum(jnp.where(own, acc, 0.0), axis=0, keepdims=True)


def _sb_decode(proj3, bias8, cache_k, cache_v, phys_pages, *, n_pages, page):
    bs = proj3.shape[0]

    def page_spec(p):
        return pl.BlockSpec((1, page, G_W), lambda b, pt: (pt[b * n_pages + p], 0, 0))

    grid_spec = pltpu.PrefetchScalarGridSpec(
        num_scalar_prefetch=1,
        grid=(bs,),
        in_specs=[pl.BlockSpec((1, 1, G_W), lambda b, pt: (b, 0, COL_SB_Q)),
                  pl.BlockSpec((8, page), lambda b, pt: (0, 0))]
                 + [page_spec(p) for p in range(n_pages)] + [page_spec(p) for p in range(n_pages)],
        out_specs=pl.BlockSpec((1, 1, G_W), lambda b, pt: (b, 0, 0)),
    )
    out = pl.pallas_call(
        functools.partial(_sb_decode_kernel, n_pages=n_pages, page=page, scale=HEAD_DIM ** -0.5),
        out_shape=jax.ShapeDtypeStruct((bs, 1, G_W), F32),
        grid_spec=grid_spec,
        compiler_params=_params("arbitrary"),
        name="sb_decode",
    )(phys_pages, proj3, bias8, *([cache_k] * n_pages), *([cache_v] * n_pages))
    return out.reshape(bs, G_W)


def _beta_and_log_decay(raw, neg_a_row, dt_row):
    lane = lax.broadcasted_iota(jnp.int32, raw.shape, 1)
    beta = jax.nn.sigmoid(raw)
    g = neg_a_row * _softplus(raw + dt_row)
    return jnp.where(lane < HEADS, beta, g)


def _l2_normalize(x):
    return x * lax.rsqrt(jnp.sum(x * x, -1, keepdims=True) + NORM_EPS)


def _dn_prep_kernel(raw_ref, halo_ref, bgr_ref, cw_ref, na_ref, dt_ref,
                    q_ref, k_ref, v_ref, bg_ref, ext_ref, *, tiles_per_seq, rows):
    first = (pl.program_id(0) % tiles_per_seq) == 0
    ext_ref[0:CONV_HALO, :] = jnp.where(first, 0.0, halo_ref[...])
    ext_ref[CONV_HALO:, :] = raw_ref[...]
    conv = None
    for j in range(CONV_W):
        off = CONV_HALO - (CONV_W - 1) + j
        term = ext_ref[off:off + rows, :] * cw_ref[j:j + 1, :]
        conv = term if conv is None else conv + term
    act = _silu(conv)
    for h in range(HEADS):
        cols = slice(h * HEAD_DIM, (h + 1) * HEAD_DIM)
        q_ref[:, cols] = _l2_normalize(act[:, h * HEAD_DIM:(h + 1) * HEAD_DIM]) * (HEAD_DIM ** -0.5)
        k_ref[:, cols] = _l2_normalize(act[:, G_W + h * HEAD_DIM:G_W + (h + 1) * HEAD_DIM])
    v_ref[...] = act[:, 2 * G_W:]
    bg_ref[...] = _beta_and_log_decay(bgr_ref[...], na_ref[...], dt_ref[...])


def _dn_prep(proj, bg_raw, conv_w, neg_a_row, dt_row, *, seq, rows):
    n = proj.shape[0]
    halo_blocks = rows // CONV_HALO
    qkv_w = 3 * G_W
    col = COL_DN_QKV * G_W // qkv_w
    whole = lambda a: pl.BlockSpec(a.shape, lambda g: (0,) * a.ndim)
    out = jax.ShapeDtypeStruct((n, G_W), F32)
    o_spec = pl.BlockSpec((rows, G_W), lambda g: (g, 0))
    return pl.pallas_call(
        functools.partial(_dn_prep_kernel, tiles_per_seq=seq // rows, rows=rows),
        out_shape=(out, out, out, jax.ShapeDtypeStruct((n, BG_LANES), F32)),
        grid=(n // rows,),
        in_specs=[pl.BlockSpec((rows, qkv_w), lambda g: (g, col)),
                  pl.BlockSpec((CONV_HALO, qkv_w), lambda g: (jnp.maximum(g * halo_blocks - 1, 0), col)),
                  pl.BlockSpec((rows, BG_LANES), lambda g: (g, 0)),
                  whole(conv_w), whole(neg_a_row), whole(dt_row)],
        out_specs=(o_spec, o_spec, o_spec, pl.BlockSpec((rows, BG_LANES), lambda g: (g, 0))),
        scratch_shapes=[pltpu.VMEM((CONV_HALO + rows, qkv_w), F32)],
        compiler_params=_params("parallel"),
        name="dn_prep",
    )(proj, proj, bg_raw, conv_w, neg_a_row, dt_row)


def _unit_lower_inverse(a_mat, eye):
    n = a_mat.shape[0]
    inv = eye - a_mat
    power = _hdot(a_mat, a_mat)
    span = 2
    while span < n:
        inv = inv + _hdot(inv, power)
        span *= 2
        if span < n:
            power = _hdot(power, power)
    return inv


def _dn_chunk(q, k, v, bg, s, h, consts):
    incl, strict, eye, lower_ones, upper_ones, all_ones = consts
    c = q.shape[0]
    beta = bg[:, h:h + 1]
    g = bg[:, HEADS + h:HEADS + h + 1]
    g_cum = _hdot(lower_ones, jnp.broadcast_to(g, (c, HEAD_DIM)))
    g_cum_row = _hdot(all_ones, g * upper_ones)
    g_col = g_cum[:, 0:1]
    diff = g_cum[:, 0:c] - g_cum_row
    decay = jnp.where(incl, jnp.exp(jnp.where(incl, diff, 0.0)), 0.0)
    kk = _hdot_nt(k, k)
    a_mat = jnp.where(strict, beta * kk * decay, 0.0)
    inv = _unit_lower_inverse(a_mat, eye)
    e_g = jnp.exp(g_col)
    rhs = jnp.concatenate([beta * v, (beta * e_g) * k], axis=1)
    sol = _hdot(inv, rhs)
    u_base, w_mat = sol[:, :HEAD_DIM], sol[:, HEAD_DIM:]
    qk = _hdot_nt(q, k) * decay
    q_dec = q * e_g
    g_last = g_col[c - 1:c, :]
    k_dec = k * jnp.exp(g_last - g_col)
    u = u_base - _hdot(w_mat, s)
    o = _hdot(q_dec, s) + _hdot(qk, u)
    pad = jnp.zeros((HEAD_DIM - c, HEAD_DIM), F32)
    k_dec_t = jnp.concatenate([k_dec, pad], axis=0).T
    s_new = s * jnp.exp(g_last) + _hdot(k_dec_t, jnp.concatenate([u, pad], axis=0))
    return o, s_new


def _dn_consts(c):
    r = lax.broadcasted_iota(jnp.int32, (c, c), 0)
    s = lax.broadcasted_iota(jnp.int32, (c, c), 1)
    incl = r >= s
    strict = r > s
    eye = (r == s).astype(F32)
    lower_ones = incl.astype(F32)
    upper_ones = (r <= s).astype(F32)
    all_ones = jnp.ones((c, c), F32)
    return incl, strict, eye, lower_ones, upper_ones, all_ones


def _dn_out(o, z, norm_g):
    o = o * lax.rsqrt(jnp.mean(o * o, -1, keepdims=True) + NORM_EPS) * norm_g
    return o * _silu(z)


def _dn_scan_kernel(q_ref, k_ref, v_ref, z_ref, bg_ref, ng_ref, y_ref, s_out_ref, s_ref, *, chunks):
    t = pl.program_id(1)

    @pl.when(t == 0)
    def _():
        s_ref[...] = jnp.zeros_like(s_ref)

    consts = _dn_consts(DN_CHUNK)
    norm_g = ng_ref[...]
    for h in range(HEADS):
        cols = slice(h * HEAD_DIM, (h + 1) * HEAD_DIM)
        s = s_ref[h]
        for ci in range(chunks):
            rows = slice(ci * DN_CHUNK, (ci + 1) * DN_CHUNK)
            o, s = _dn_chunk(q_ref[rows, cols], k_ref[rows, cols], v_ref[rows, cols], bg_ref[rows, :],
                             s, h, consts)
            y_ref[rows, cols] = _dn_out(o, z_ref[rows, cols], norm_g)
        s_ref[h] = s

    @pl.when(t == pl.num_programs(1) - 1)
    def _():
        s_out_ref[0] = s_ref[...]


def _dn_scan(q, k, v, proj, bg, norm_g, *, batch, seq, chunks):
    n = q.shape[0]
    rows = chunks * DN_CHUNK
    steps = seq // rows
    spec = pl.BlockSpec((rows, G_W), lambda b, t: (b * steps + t, 0))
    ng = norm_g.reshape(1, HEAD_DIM)
    return pl.pallas_call(
        functools.partial(_dn_scan_kernel, chunks=chunks),
        out_shape=(jax.ShapeDtypeStruct((n, G_W), F32),
                   jax.ShapeDtypeStruct((batch, HEADS, HEAD_DIM, HEAD_DIM), F32)),
        grid=(batch, steps),
        in_specs=[spec, spec, spec,
                  pl.BlockSpec((rows, G_W), lambda b, t: (b * steps + t, COL_DN_Z)),
                  pl.BlockSpec((rows, BG_LANES), lambda b, t: (b * steps + t, 0)),
                  pl.BlockSpec((1, HEAD_DIM), lambda b, t: (0, 0))],
        out_specs=(spec, pl.BlockSpec((1, HEADS, HEAD_DIM, HEAD_DIM), lambda b, t: (b, 0, 0, 0))),
        scratch_shapes=[pltpu.VMEM((HEADS, HEAD_DIM, HEAD_DIM), F32)],
        compiler_params=_params("parallel", "arbitrary"),
        name="dn_scan",
    )(q, k, v, proj, bg, ng)


def _sample_mix_kernel(pool_ref, u_ref, v_ref, qkv_ref, z_ref, bgr_ref, sp_ref, sc_ref, sd_ref,
                       pw_ref, ps_ref, sw0_ref, sb0_ref, cw_ref, na_ref, dt_ref, ng_ref,
                       yp_ref, ys_ref, yd_ref, np_ref, nc_ref, nd_ref, *, bt, pos0):
    new = pool_ref[...]
    for gi, w in enumerate(POOL_WINDOWS):
        cols = slice(gi * HEAD_DIM, (gi + 1) * HEAD_DIM)
        s = new[:, cols]
        for j in range(1, w):
            s = s + sp_ref[:, POOL_BUF - j, cols]
        d = s / float(min(pos0 + 1, w)) - new[:, cols]
        yp_ref[:, cols] = _hdot(d, pw_ref[gi]) * ps_ref[:, cols]
    np_ref[:, 0:POOL_BUF - 1, :] = sp_ref[:, 1:POOL_BUF, :]
    np_ref[:, POOL_BUF - 1, :] = new

    ys_ref[...] = u_ref[...] * (sw0_ref[...] * v_ref[...] + sb0_ref[...])

    raw = qkv_ref[...]
    conv = raw * cw_ref[CONV_W - 1:CONV_W, :]
    for j in range(CONV_W - 1):
        conv = conv + sc_ref[:, j, :] * cw_ref[j:j + 1, :]
    nc_ref[:, 0:CONV_W - 2, :] = sc_ref[:, 1:CONV_W - 1, :]
    nc_ref[:, CONV_W - 2, :] = raw
    act = _silu(conv)
    bg = _beta_and_log_decay(bgr_ref[...], na_ref[...], dt_ref[...])
    norm_g = ng_ref[...]
    qs, ks = [], []
    for h in range(HEADS):
        qs.append(_l2_normalize(act[:, h * HEAD_DIM:(h + 1) * HEAD_DIM]) * (HEAD_DIM ** -0.5))
        ks.append(_l2_normalize(act[:, G_W + h * HEAD_DIM:G_W + (h + 1) * HEAD_DIM]))
    pad = jnp.zeros((HEAD_DIM - bt, HEAD_DIM), F32)
    for h in range(HEADS):
        cols = slice(h * HEAD_DIM, (h + 1) * HEAD_DIM)
        v_h = act[:, 2 * G_W + h * HEAD_DIM:2 * G_W + (h + 1) * HEAD_DIM]
        z_h = z_ref[:, cols]
        k_t = jnp.concatenate([ks[h], pad], axis=0).T
        q_t = jnp.concatenate([qs[h], pad], axis=0).T
        for b in range(bt):
            s0 = sd_ref[b, h]
            k_col = jnp.broadcast_to(k_t[:, b:b + 1], (HEAD_DIM, HEAD_DIM))
            q_col = jnp.broadcast_to(q_t[:, b:b + 1], (HEAD_DIM, HEAD_DIM))
            beta = bg[b:b + 1, h:h + 1]
            a = jnp.exp(bg[b:b + 1, HEADS + h:HEADS + h + 1])
            k_s = jnp.sum(k_col * s0, axis=0, keepdims=True)
            u = beta * (v_h[b:b + 1, :] - a * k_s)
            s1 = a * s0 + k_col * u
            nd_ref[b, h] = s1
            o = jnp.sum(q_col * s1, axis=0, keepdims=True)
            yd_ref[b:b + 1, cols] = _dn_out(o, z_h[b:b + 1, :], norm_g)


def _sample_mix(proj, bg_raw, state_pool, state_conv, state_delta, pool_w, pool_scale, sgu_w0, sgu_b0,
                conv_w, neg_a_row, dt_row, norm_g, *, bt, pos0):
    bs = proj.shape[0]
    qkv_w = 3 * G_W
    col = lambda c: pl.BlockSpec((bt, G_W), lambda i: (i, c))
    whole = lambda a: pl.BlockSpec(a.shape, lambda i: (0,) * a.ndim)
    row = pl.BlockSpec((bt, G_W), lambda i: (i, 0))
    ps = pool_scale.reshape(1, G_W)
    ng = norm_g.reshape(1, HEAD_DIM)
    consts = (pool_w, ps, sgu_w0, sgu_b0, conv_w, neg_a_row, dt_row, ng)
    out_sds = lambda a: jax.ShapeDtypeStruct(a.shape, F32)
    pool_spec = pl.BlockSpec((bt, POOL_BUF, G_W), lambda i: (i, 0, 0))
    conv_spec = pl.BlockSpec((bt, CONV_W - 1, qkv_w), lambda i: (i, 0, 0))
    delta_spec = pl.BlockSpec((bt, HEADS, HEAD_DIM, HEAD_DIM), lambda i: (i, 0, 0, 0))
    y = jax.ShapeDtypeStruct((bs, G_W), F32)
    return pl.pallas_call(
        functools.partial(_sample_mix_kernel, bt=bt, pos0=pos0),
        out_shape=(y, y, y, out_sds(state_pool), out_sds(state_conv), out_sds(state_delta)),
        grid=(bs // bt,),
        in_specs=[col(COL_POOL), col(COL_SGU_U), col(COL_SGU_V),
                  pl.BlockSpec((bt, qkv_w), lambda i: (i, COL_DN_QKV * G_W // qkv_w)),
                  col(COL_DN_Z),
                  pl.BlockSpec((bt, BG_LANES), lambda i: (i, 0)),
                  pool_spec, conv_spec, delta_spec] + [whole(a) for a in consts],
        out_specs=(row, row, row, pool_spec, conv_spec, delta_spec),
        compiler_params=_params("parallel"),
        name="sample_mix",
    )(proj, proj, proj, proj, proj, bg_raw, state_pool, state_conv, state_delta, *consts)


def _lane_row(values, offset):
    return jnp.zeros((1, BG_LANES), F32).at[0, offset:offset + values.shape[0]].set(values)


def kernel(x_prompt, x_sample, cache_k, cache_v, page_table, state_pool, state_conv, state_delta,
           ln_g, ln_b, w_ffn1_in, w_ffn1_out, w_ffn2_in, w_ffn2_out, w_in, w_out,
           pool_w, pool_scale, sgu_w, sgu_b, sb_bias, dn_conv_w, dn_a_log, dn_dt_bias, dn_norm_g):
    depth = ln_g.shape[0]
    alpha = (2.0 * depth) ** 0.25
    bp, seq, d_model = x_prompt.shape
    bs, dec_seq, _ = x_sample.shape
    assert dec_seq == 1
    n_phys, page = cache_k.shape[1], cache_k.shape[2]
    n_pages = page_table.shape[1]
    pos0 = n_pages * page
    ck = cache_k.reshape(depth * n_phys, page, G_W)
    cv = cache_v.reshape(depth * n_phys, page, G_W)
    hp = x_prompt.reshape(bp * seq, d_model)
    hs = x_sample.reshape(bs, d_model)
    tm_p = 512
    tm_s = bs

    outs = {name: [] for name in ("kp", "vp", "poolp", "convp", "sp", "ks", "vs", "pools", "convs", "ss", "sgus")}
    for l in range(depth):
        w1i, w1o = w_ffn1_in[l].astype(BF16), w_ffn1_out[l].astype(BF16)
        w2i, w2o = w_ffn2_in[l].astype(BF16), w_ffn2_out[l].astype(BF16)
        wi = w_in[l].astype(BF16)
        wi_main = wi[:, :N_MAIN_COLS]
        wi_tail = jnp.pad(wi[:, N_MAIN_COLS:], ((0, 0), (0, BG_LANES - (wi.shape[1] - N_MAIN_COLS))))
        wo = w_out[l].astype(BF16)
        neg_a_row = _lane_row(-jnp.exp(dn_a_log[l]), HEADS)
        dt_row = _lane_row(dn_dt_bias[l], HEADS)

        h1 = _ffn_ln(hp, w1i, w1o, ln_g[l, 0], ln_b[l, 0], alpha=alpha, tm=tm_p, tf=512)
        proj, bg_raw = _in_proj(h1, wi_main, wi_tail, tm=tm_p, tn=512)
        y_pool, y_sgu = _pool_sgu(proj, pool_w[l], pool_scale[l], sgu_w[l], sgu_b[l], seq=seq)
        y_sb = _sb_attn(proj, sb_bias[l], batch=bp, seq=seq, blk=256)
        dq, dk, dv, bg = _dn_prep(proj, bg_raw, dn_conv_w[l], neg_a_row, dt_row, seq=seq, rows=256)
        y_dn, s_end = _dn_scan(dq, dk, dv, proj, bg, dn_norm_g[l], batch=bp, seq=seq, chunks=4)
        h2 = _out_proj_ln(h1, (y_pool, y_sgu, y_sb, y_dn), wo, ln_g[l, 1], ln_b[l, 1], alpha=alpha, tm=256)
        hp = _ffn_ln(h2, w2i, w2o, ln_g[l, 2], ln_b[l, 2], alpha=alpha, tm=tm_p, tf=512)
        proj3 = proj.reshape(bp, seq, N_MAIN_COLS)
        outs["kp"].append(proj3[:, :, COL_SB_K * G_W:(COL_SB_K + 1) * G_W].reshape(bp, seq, HEADS, HEAD_DIM))
        outs["vp"].append(proj3[:, :, COL_SB_V * G_W:(COL_SB_V + 1) * G_W].reshape(bp, seq, HEADS, HEAD_DIM))
        outs["poolp"].append(proj3[:, seq - POOL_BUF:, :G_W])
        outs["convp"].append(proj3[:, seq - (CONV_W - 1):, COL_DN_QKV * G_W:(COL_DN_QKV + 3) * G_W])
        outs["sp"].append(s_end)

        g1 = _ffn_ln(hs, w1i, w1o, ln_g[l, 0], ln_b[l, 0], alpha=alpha, tm=tm_s, tf=512)
        sproj, sbg_raw = _in_proj(g1, wi_main, wi_tail, tm=tm_s, tn=512)
        sgu_w0 = jnp.repeat(sgu_w[l, :, 0, 0], HEAD_DIM).reshape(1, G_W)
        sgu_b0 = jnp.repeat(sgu_b[l, :, 0], HEAD_DIM).reshape(1, G_W)
        sy_pool, sy_sgu, sy_dn, new_pool, new_conv, new_delta = _sample_mix(
            sproj, sbg_raw, state_pool[l], state_conv[l], state_delta[l], pool_w[l], pool_scale[l],
            sgu_w0, sgu_b0, dn_conv_w[l], neg_a_row, dt_row, dn_norm_g[l], bt=8, pos0=pos0)
        bias8 = jnp.zeros((8, page), F32).at[:HEADS, :].set(jnp.broadcast_to(sb_bias[l][:, None], (HEADS, page)))
        phys = (page_table + l * n_phys).reshape(-1)
        sy_sb = _sb_decode(sproj.reshape(bs, 1, N_MAIN_COLS), bias8, ck, cv, phys, n_pages=n_pages, page=page)
        g2 = _out_proj_ln(g1, (sy_pool, sy_sgu, sy_sb, sy_dn), wo, ln_g[l, 1], ln_b[l, 1], alpha=alpha, tm=tm_s)
        hs = _ffn_ln(g2, w2i, w2o, ln_g[l, 2], ln_b[l, 2], alpha=alpha, tm=tm_s, tf=512)
        outs["ks"].append(sproj[:, COL_SB_K * G_W:(COL_SB_K + 1) * G_W].reshape(bs, 1, HEADS, HEAD_DIM))
        outs["vs"].append(sproj[:, COL_SB_V * G_W:(COL_SB_V + 1) * G_W].reshape(bs, 1, HEADS, HEAD_DIM))
        outs["pools"].append(new_pool)
        outs["convs"].append(new_conv)
        outs["ss"].append(new_delta)
        outs["sgus"].append(sproj[:, COL_SGU_V * G_W:(COL_SGU_V + 1) * G_W].reshape(bs, 1, G_W))

    st = lambda name: jnp.stack(outs[name])
    return (hp.reshape(bp, seq, d_model), hs.reshape(bs, 1, d_model),
            st("kp"), st("vp"), st("poolp"), st("convp"), st("sp"),
            st("ks"), st("vs"), st("pools"), st("convs"), st("ss"), st("sgus"))
```

```python
import functools

import jax
import jax.numpy as jnp
import numpy as np
from jax import lax
from jax.experimental import pallas as pl
from jax.experimental.pallas import tpu as pltpu

F32 = jnp.float32
BF16 = jnp.bfloat16
HIGHEST = lax.Precision.HIGHEST

HEADS = 4
HEAD_DIM = 128
G_W = HEADS * HEAD_DIM
POOL_WINDOWS = (2, 4, 8, 16)
POOL_BUF = max(POOL_WINDOWS) - 1
POOL_HALO = 16
SGU_CHUNK = 128
DN_CHUNK = 64
CONV_W = 4
CONV_HALO = 8
LN_EPS = 1e-5
NORM_EPS = 1e-6
VMEM_LIMIT_BYTES = 48 * 1024 * 1024

COL_POOL, COL_SGU_U, COL_SGU_V, COL_SB_Q, COL_SB_K, COL_SB_V, COL_DN_QKV, COL_DN_Z = 0, 1, 2, 3, 4, 5, 6, 9
N_MAIN_COLS = 10 * G_W
BG_LANES = 128


def _params(*sem):
    return pltpu.CompilerParams(dimension_semantics=sem, vmem_limit_bytes=VMEM_LIMIT_BYTES)


def _hdot(a, b):
    return jnp.dot(a, b, precision=HIGHEST, preferred_element_type=F32)


def _hdot_nt(a, b):
    return lax.dot_general(a, b, (((1,), (1,)), ((), ())), precision=HIGHEST, preferred_element_type=F32)


def _bdot(a, b):
    return jnp.dot(a.astype(BF16), b.astype(BF16), preferred_element_type=F32)


def _bdot_nt(a, b):
    return lax.dot_general(a.astype(BF16), b.astype(BF16), (((1,), (1,)), ((), ())),
                           preferred_element_type=F32)


def _silu(x):
    return x * jax.nn.sigmoid(x)


def _softplus(x):
    return jnp.maximum(x, 0.0) + jnp.log1p(jnp.exp(-jnp.abs(x)))


def _layer_norm(y, g, b):
    mu = jnp.mean(y, -1, keepdims=True)
    d = y - mu
    var = jnp.mean(d * d, -1, keepdims=True)
    return d * lax.rsqrt(var + LN_EPS) * g + b


def _split_bf16(x):
    hi = x.astype(BF16)
    lo = (x - hi.astype(F32)).astype(BF16)
    return hi, lo


def _dot3(a, b, dims=(((1,), (0,)), ((), ()))):
    (ah, al), (bh, bl) = a, b
    dot = functools.partial(lax.dot_general, dimension_numbers=dims, preferred_element_type=F32)
    return dot(ah, bh) + dot(al, bh) + dot(ah, bl)


_NT_DIMS = (((1,), (1,)), ((), ()))


def _ffn_ln_kernel(x_ref, wg_ref, wu_ref, wo_ref, g_ref, b_ref, o_ref, xb_ref, acc_ref, *, alpha):
    f = pl.program_id(1)

    @pl.when(f == 0)
    def _():
        xb_ref[...] = x_ref[...].astype(BF16)
        acc_ref[...] = jnp.zeros_like(acc_ref)

    xb = xb_ref[...]
    gate = jnp.dot(xb, wg_ref[...], preferred_element_type=F32)
    up = jnp.dot(xb, wu_ref[...], preferred_element_type=F32)
    act = (_silu(gate) * up).astype(BF16)
    acc_ref[...] += jnp.dot(act, wo_ref[...], preferred_element_type=F32)

    @pl.when(f == pl.num_programs(1) - 1)
    def _():
        o_ref[...] = _layer_norm(alpha * x_ref[...] + 0.5 * acc_ref[...], g_ref[...], b_ref[...])


def _ffn_ln(x, w_in, w_out, g, b, *, alpha, tm, tf):
    n, d = x.shape
    dff = w_out.shape[0]
    nf = dff // tf
    return pl.pallas_call(
        functools.partial(_ffn_ln_kernel, alpha=alpha),
        out_shape=jax.ShapeDtypeStruct((n, d), F32),
        grid=(n // tm, nf),
        in_specs=[
            pl.BlockSpec((tm, d), lambda i, f: (i, 0)),
            pl.BlockSpec((d, tf), lambda i, f: (0, f)),
            pl.BlockSpec((d, tf), lambda i, f: (0, f + nf)),
            pl.BlockSpec((tf, d), lambda i, f: (f, 0)),
            pl.BlockSpec((1, d), lambda i, f: (0, 0)),
            pl.BlockSpec((1, d), lambda i, f: (0, 0)),
        ],
        out_specs=pl.BlockSpec((tm, d), lambda i, f: (i, 0)),
        scratch_shapes=[pltpu.VMEM((tm, d), BF16), pltpu.VMEM((tm, d), F32)],
        compiler_params=_params("parallel", "arbitrary"),
        name="ffn_ln",
    )(x, w_in, w_in, w_out, g.reshape(1, d), b.reshape(1, d))


def _in_proj_kernel(h_ref, w_ref, wt_ref, o_ref, bg_ref, k_ref, v_ref, *, tm):
    hb = h_ref[...].astype(BF16)
    bg_ref[...] = jnp.dot(hb, wt_ref[...], preferred_element_type=F32)
    o = jnp.dot(hb, w_ref[...], preferred_element_type=F32)
    o_ref[...] = o
    for dst, col in ((k_ref, COL_SB_K), (v_ref, COL_SB_V)):
        for h in range(HEADS):
            lo = col * G_W + h * HEAD_DIM
            dst[pl.ds(h, tm, stride=HEADS), :] = o[:, lo:lo + HEAD_DIM]


def _in_proj(h, w_main, w_tail, *, tm):
    n, d = h.shape
    resident = pl.Buffered(1)
    rows_kv = jax.ShapeDtypeStruct((n * HEADS, HEAD_DIM), F32)
    kv_spec = pl.BlockSpec((tm * HEADS, HEAD_DIM), lambda i: (i, 0))
    return pl.pallas_call(
        functools.partial(_in_proj_kernel, tm=tm),
        out_shape=(jax.ShapeDtypeStruct((n, N_MAIN_COLS), F32), jax.ShapeDtypeStruct((n, BG_LANES), F32),
                   rows_kv, rows_kv),
        grid=(n // tm,),
        in_specs=[
            pl.BlockSpec((tm, d), lambda i: (i, 0)),
            pl.BlockSpec((d, N_MAIN_COLS), lambda i: (0, 0), pipeline_mode=resident),
            pl.BlockSpec((d, BG_LANES), lambda i: (0, 0), pipeline_mode=resident),
        ],
        out_specs=(pl.BlockSpec((tm, N_MAIN_COLS), lambda i: (i, 0)),
                   pl.BlockSpec((tm, BG_LANES), lambda i: (i, 0)), kv_spec, kv_spec),
        compiler_params=_params("parallel"),
        name="in_proj",
    )(h, w_main, w_tail)


def _out_proj_ln_kernel(h_ref, y0_ref, y1_ref, y2_ref, y3_ref, w_ref, g_ref, b_ref, o_ref, *, alpha):
    acc = None
    for m, y_ref in enumerate((y0_ref, y1_ref, y2_ref, y3_ref)):
        part = jnp.dot(y_ref[...].astype(BF16), w_ref[m * G_W:(m + 1) * G_W, :], preferred_element_type=F32)
        acc = part if acc is None else acc + part
    o_ref[...] = _layer_norm(alpha * h_ref[...] + acc, g_ref[...], b_ref[...])


def _out_proj_ln(h, ys, w_out, g, b, *, alpha, tm):
    n, d = h.shape
    y_spec = pl.BlockSpec((tm, G_W), lambda i: (i, 0))
    return pl.pallas_call(
        functools.partial(_out_proj_ln_kernel, alpha=alpha),
        out_shape=jax.ShapeDtypeStruct((n, d), F32),
        grid=(n // tm,),
        in_specs=[pl.BlockSpec((tm, d), lambda i: (i, 0)), y_spec, y_spec, y_spec, y_spec,
                  pl.BlockSpec((4 * G_W, d), lambda i: (0, 0)),
                  pl.BlockSpec((1, d), lambda i: (0, 0)),
                  pl.BlockSpec((1, d), lambda i: (0, 0))],
        out_specs=pl.BlockSpec((tm, d), lambda i: (i, 0)),
        compiler_params=_params("parallel"),
        name="out_proj_ln",
    )(h, *ys, w_out, g.reshape(1, d), b.reshape(1, d))


def _pool_sgu_kernel(cur_ref, halo_ref, u_ref, v_ref, pw_ref, ps_ref, sw_ref, sbc_ref,
                     yp_ref, ys_ref, ext_ref, *, chunks_per_seq):
    c = pl.program_id(0) % chunks_per_seq
    cur = cur_ref[...]
    ext_ref[0:POOL_HALO, :] = jnp.where(c == 0, 0.0, halo_ref[...])
    ext_ref[POOL_HALO:, :] = cur
    rows = SGU_CHUNK
    pos = c * rows + lax.broadcasted_iota(jnp.int32, (rows, 1), 0)
    for gi, w in enumerate(POOL_WINDOWS):
        cols = slice(gi * HEAD_DIM, (gi + 1) * HEAD_DIM)
        s = cur[:, cols]
        for j in range(1, w):
            s = s + ext_ref[POOL_HALO - j:POOL_HALO - j + rows, cols]
        cnt = jnp.minimum(pos + 1, w).astype(F32)
        d = s / cnt - cur[:, cols]
        yp_ref[:, cols] = _hdot(d, pw_ref[gi]) * ps_ref[:, cols]

    r = lax.broadcasted_iota(jnp.int32, (rows, rows), 0)
    s_ = lax.broadcasted_iota(jnp.int32, (rows, rows), 1)
    causal = r >= s_
    for h in range(HEADS):
        cols = slice(h * HEAD_DIM, (h + 1) * HEAD_DIM)
        wm = jnp.where(causal, sw_ref[h], 0.0)
        mixed = _hdot(wm, v_ref[:, cols]) + sbc_ref[:, h:h + 1]
        ys_ref[:, cols] = u_ref[:, cols] * mixed


def _pool_sgu(proj, pool_w, pool_scale, sgu_w, sgu_b, *, seq):
    n = proj.shape[0]
    rows = SGU_CHUNK
    halo_blocks = rows // POOL_HALO
    blk = lambda col: pl.BlockSpec((rows, G_W), lambda g: (g, col))
    whole = lambda a: pl.BlockSpec(a.shape, lambda g: (0,) * a.ndim)
    ps = pool_scale.reshape(1, G_W)
    sbc = sgu_b.T
    return pl.pallas_call(
        functools.partial(_pool_sgu_kernel, chunks_per_seq=seq // rows),
        out_shape=(jax.ShapeDtypeStruct((n, G_W), F32), jax.ShapeDtypeStruct((n, G_W), F32)),
        grid=(n // rows,),
        in_specs=[blk(COL_POOL),
                  pl.BlockSpec((POOL_HALO, G_W), lambda g: (jnp.maximum(g * halo_blocks - 1, 0), COL_POOL)),
                  blk(COL_SGU_U), blk(COL_SGU_V), whole(pool_w), whole(ps), whole(sgu_w), whole(sbc)],
        out_specs=(pl.BlockSpec((rows, G_W), lambda g: (g, 0)), pl.BlockSpec((rows, G_W), lambda g: (g, 0))),
        scratch_shapes=[pltpu.VMEM((POOL_HALO + rows, G_W), F32)],
        compiler_params=_params("parallel"),
        name="pool_sgu",
    )(proj, proj, proj, proj, pool_w, ps, sgu_w, sbc)


def _sb_attn_kernel(bias_ref, q_ref, k_ref, v_ref, o_ref, *, blk, scale):
    qi = pl.program_id(1)
    heads = lambda x: jnp.stack([x[:, h * HEAD_DIM:(h + 1) * HEAD_DIM] for h in range(HEADS)])
    q = heads(q_ref[...]).astype(BF16)
    bias = bias_ref[...]
    row = lax.broadcasted_iota(jnp.int32, (blk, blk), 0)
    col = lax.broadcasted_iota(jnp.int32, (blk, blk), 1)
    after = (row > col).astype(BF16)

    def visit(j, carry, valid):
        o, run = carry
        start = pl.multiple_of(j * blk, blk)
        kb = heads(k_ref[pl.ds(start, blk), :]).astype(BF16)
        vb = heads(v_ref[pl.ds(start, blk), :]).astype(BF16)
        z = lax.dot_general(q, kb, _B_NT, preferred_element_type=F32) * scale + bias
        sp = _softplus(z)
        log_fail = -sp if valid is None else jnp.where(valid, -sp, 0.0)
        hi, lo = _split_bf16(log_fail.reshape(HEADS * blk, blk))
        later = (jnp.dot(hi, after, preferred_element_type=F32)
                 + jnp.dot(lo, after, preferred_element_type=F32)).reshape(HEADS, blk, blk) + run
        att = jnp.exp(z - sp + later)
        if valid is not None:
            att = jnp.where(valid, att, 0.0)
        o = o + lax.dot_general(att.astype(BF16), vb, _B_NN, preferred_element_type=F32)
        run = run + jnp.sum(log_fail, axis=-1, keepdims=True)
        return o, run

    init = (jnp.zeros((HEADS, blk, HEAD_DIM), F32), jnp.zeros((HEADS, blk, 1), F32))
    carry = visit(qi, init, (col < row)[None])
    o, _ = lax.fori_loop(0, qi, lambda step, carry: visit(qi - 1 - step, carry, None), carry)
    for h in range(HEADS):
        o_ref[:, h * HEAD_DIM:(h + 1) * HEAD_DIM] = o[h]


def _sb_attn(proj, sb_bias, *, batch, seq, blk):
    n = proj.shape[0]
    nq = seq // blk
    return pl.pallas_call(
        functools.partial(_sb_attn_kernel, blk=blk, scale=HEAD_DIM ** -0.5),
        out_shape=jax.ShapeDtypeStruct((n, G_W), F32),
        grid=(batch, nq),
        in_specs=[pl.BlockSpec((HEADS, 1, 1), lambda b, i: (0, 0, 0)),
                  pl.BlockSpec((blk, G_W), lambda b, i: (b * nq + i, COL_SB_Q)),
                  pl.BlockSpec((seq, G_W), lambda b, i: (b, COL_SB_K)),
                  pl.BlockSpec((seq, G_W), lambda b, i: (b, COL_SB_V))],
        out_specs=pl.BlockSpec((blk, G_W), lambda b, i: (b * nq + i, 0)),
        compiler_params=_params("parallel", "arbitrary"),
        name="sb_attn",
    )(sb_bias.reshape(HEADS, 1, 1), proj, proj, proj)


def _sb_decode_kernel(pt_ref, q_ref, bias_ref, cum_ref, *refs, n_pages, width, scale):
    del pt_ref
    k_refs = refs[:n_pages]
    v_refs = refs[n_pages:2 * n_pages]
    o_ref = refs[2 * n_pages]
    rows = 8
    q = q_ref[0]
    q_rows = jnp.concatenate([q[:, h * HEAD_DIM:(h + 1) * HEAD_DIM] for h in range(HEADS)]
                             + [jnp.zeros((rows - HEADS, HEAD_DIM), F32)], axis=0).astype(BF16)
    rid = lax.broadcasted_iota(jnp.int32, (rows, width), 0)
    cid = lax.broadcasted_iota(jnp.int32, (rows, width), 1)
    own = (cid & (HEADS - 1)) == rid

    z_rows = []
    for p in range(n_pages):
        zf = _bdot_nt(q_rows, k_refs[p][...])
        z_rows.append(jnp.sum(jnp.where(own, zf, 0.0), axis=0, keepdims=True))
    z = jnp.concatenate(z_rows, axis=0) * scale + bias_ref[...]
    sp = _softplus(z)
    hi, lo = _split_bf16(-sp)
    cum = (jnp.dot(hi, cum_ref[...], preferred_element_type=F32)
           + jnp.dot(lo, cum_ref[...], preferred_element_type=F32))
    in_page, page_tot = cum[:, :width], cum[:, width:]
    pr = lax.broadcasted_iota(jnp.int32, (n_pages, n_pages), 0)
    pc = lax.broadcasted_iota(jnp.int32, (n_pages, n_pages), 1)
    later_pages = _hdot((pc > pr).astype(F32), page_tot)
    att = jnp.exp(z - sp + in_page + later_pages)

    acc = jnp.zeros((rows, HEAD_DIM), F32)
    for p in range(n_pages):
        a_p = jnp.where(own, jnp.broadcast_to(att[p:p + 1, :], (rows, width)), 0.0)
        acc = acc + _bdot(a_p, v_refs[p][...])
    for h in range(HEADS):
        o_ref[0, :, h * HEAD_DIM:(h + 1) * HEAD_DIM] = acc[h:h + 1, :]


def _same_head_sums(width):
    src = np.arange(width)[:, None]
    dst = np.arange(width)[None, :]
    same = (src - dst) % HEADS == 0
    return jnp.asarray(np.concatenate([same & (src > dst), same], axis=1), BF16)


def _sb_decode(proj3, bias_row, cache_k, cache_v, phys_pages, *, n_pages, width):
    bs = proj3.shape[0]

    def page_spec(p):
        return pl.BlockSpec((width, HEAD_DIM), lambda b, pt: (pt[b * n_pages + p], 0))

    grid_spec = pltpu.PrefetchScalarGridSpec(
        num_scalar_prefetch=1,
        grid=(bs,),
        in_specs=[pl.BlockSpec((1, 1, G_W), lambda b, pt: (b, 0, COL_SB_Q)),
                  pl.BlockSpec((1, width), lambda b, pt: (0, 0)),
                  pl.BlockSpec((width, 2 * width), lambda b, pt: (0, 0))]
                 + [page_spec(p) for p in range(n_pages)] + [page_spec(p) for p in range(n_pages)],
        out_specs=pl.BlockSpec((1, 1, G_W), lambda b, pt: (b, 0, 0)),
    )
    out = pl.pallas_call(
        functools.partial(_sb_decode_kernel, n_pages=n_pages, width=width, scale=HEAD_DIM ** -0.5),
        out_shape=jax.ShapeDtypeStruct((bs, 1, G_W), F32),
        grid_spec=grid_spec,
        compiler_params=_params("arbitrary"),
        name="sb_decode",
    )(phys_pages, proj3, bias_row, _same_head_sums(width), *([cache_k] * n_pages), *([cache_v] * n_pages))
    return out.reshape(bs, G_W)


def _beta_and_log_decay(raw, neg_a_row, dt_row):
    lane = lax.broadcasted_iota(jnp.int32, raw.shape, 1)
    beta = jax.nn.sigmoid(raw)
    g = neg_a_row * _softplus(raw + dt_row)
    return jnp.where(lane < HEADS, beta, g)


def _l2_normalize(x):
    return x * lax.rsqrt(jnp.sum(x * x, -1, keepdims=True) + NORM_EPS)


def _dn_prep_kernel(raw_ref, halo_ref, bgr_ref, cw_ref, na_ref, dt_ref,
                    q_ref, k_ref, v_ref, bg_ref, bgt_ref, ext_ref, *, tiles_per_seq, rows):
    first = (pl.program_id(0) % tiles_per_seq) == 0
    ext_ref[0:CONV_HALO, :] = jnp.where(first, 0.0, halo_ref[...])
    ext_ref[CONV_HALO:, :] = raw_ref[...]
    conv = None
    for j in range(CONV_W):
        off = CONV_HALO - (CONV_W - 1) + j
        term = ext_ref[off:off + rows, :] * cw_ref[j:j + 1, :]
        conv = term if conv is None else conv + term
    act = _silu(conv)
    for h in range(HEADS):
        cols = slice(h * HEAD_DIM, (h + 1) * HEAD_DIM)
        q_ref[:, cols] = _l2_normalize(act[:, h * HEAD_DIM:(h + 1) * HEAD_DIM]) * (HEAD_DIM ** -0.5)
        k_ref[:, cols] = _l2_normalize(act[:, G_W + h * HEAD_DIM:G_W + (h + 1) * HEAD_DIM])
    v_ref[...] = act[:, 2 * G_W:]
    bg = _beta_and_log_decay(bgr_ref[...], na_ref[...], dt_ref[...])
    r = lax.broadcasted_iota(jnp.int32, (rows, rows), 0)
    s = lax.broadcasted_iota(jnp.int32, (rows, rows), 1)
    shift = DN_CHUNK.bit_length() - 1
    chunk_prefix = ((r >= s) & (lax.shift_right_logical(r, shift) == lax.shift_right_logical(s, shift)))
    lane = lax.broadcasted_iota(jnp.int32, bg.shape, 1)
    bg = jnp.where(lane < HEADS, bg, _hdot(chunk_prefix.astype(F32), bg))
    bg_ref[...] = bg
    per_tile = BG_LANES // DN_CHUNK
    for t in range(rows // BG_LANES):
        tile_t = bg[t * BG_LANES:(t + 1) * BG_LANES, :].T
        for c in range(per_tile):
            bgt_ref[t * per_tile + c] = tile_t[0:2 * HEADS, c * DN_CHUNK:(c + 1) * DN_CHUNK]


def _dn_prep(proj, bg_raw, conv_w, neg_a_row, dt_row, *, seq, rows):
    n = proj.shape[0]
    halo_blocks = rows // CONV_HALO
    qkv_w = 3 * G_W
    col = COL_DN_QKV * G_W // qkv_w
    whole = lambda a: pl.BlockSpec(a.shape, lambda g: (0,) * a.ndim)
    out = jax.ShapeDtypeStruct((n, G_W), F32)
    o_spec = pl.BlockSpec((rows, G_W), lambda g: (g, 0))
    return pl.pallas_call(
        functools.partial(_dn_prep_kernel, tiles_per_seq=seq // rows, rows=rows),
        out_shape=(out, out, out, jax.ShapeDtypeStruct((n, BG_LANES), F32),
                   jax.ShapeDtypeStruct((n // DN_CHUNK, 2 * HEADS, DN_CHUNK), F32)),
        grid=(n // rows,),
        in_specs=[pl.BlockSpec((rows, qkv_w), lambda g: (g, col)),
                  pl.BlockSpec((CONV_HALO, qkv_w), lambda g: (jnp.maximum(g * halo_blocks - 1, 0), col)),
                  pl.BlockSpec((rows, BG_LANES), lambda g: (g, 0)),
                  whole(conv_w), whole(neg_a_row), whole(dt_row)],
        out_specs=(o_spec, o_spec, o_spec, pl.BlockSpec((rows, BG_LANES), lambda g: (g, 0)),
                   pl.BlockSpec((rows // DN_CHUNK, 2 * HEADS, DN_CHUNK), lambda g: (g, 0, 0))),
        scratch_shapes=[pltpu.VMEM((CONV_HALO + rows, qkv_w), F32)],
        compiler_params=_params("parallel"),
        name="dn_prep",
    )(proj, proj, bg_raw, conv_w, neg_a_row, dt_row)


_B_NN = (((2,), (1,)), ((0,), (0,)))
_B_NT = (((2,), (2,)), ((0,), (0,)))
_B_TN = (((1,), (1,)), ((0,), (0,)))


def _unit_lower_inverse(a_mat, eye):
    n = a_mat.shape[-1]
    inv = eye - a_mat
    a_split = _split_bf16(a_mat)
    power = _dot3(a_split, a_split, _B_NN)
    span = 2
    while span < n:
        p_split = _split_bf16(power)
        inv = inv + _dot3(_split_bf16(inv), p_split, _B_NN)
        span *= 2
        if span < n:
            power = _dot3(p_split, p_split, _B_NN)
    return inv


def _dn_out(o, z, norm_g):
    o = o * lax.rsqrt(jnp.mean(o * o, -1, keepdims=True) + NORM_EPS) * norm_g
    return o * _silu(z)


def _dn_scan_kernel(q_ref, k_ref, v_ref, z_ref, bg_ref, bgt_ref, ng_ref, y_ref, s_out_ref, s_ref, *, chunks):
    t = pl.program_id(1)

    @pl.when(t == 0)
    def _():
        s_ref[...] = jnp.zeros_like(s_ref)

    c, d = DN_CHUNK, HEAD_DIM
    units = [(ci, h) for ci in range(chunks) for h in range(HEADS)]
    rows = lambda ci: slice(ci * c, (ci + 1) * c)
    cols = lambda h: slice(h * d, (h + 1) * d)
    gather = lambda ref: jnp.stack([ref[rows(ci), cols(h)] for ci, h in units])
    q, k, v = gather(q_ref), gather(k_ref), gather(v_ref)
    beta = jnp.stack([bg_ref[rows(ci), h:h + 1] for ci, h in units])
    g_col = jnp.stack([bg_ref[rows(ci), HEADS + h:HEADS + h + 1] for ci, h in units])
    g_row = jnp.stack([bgt_ref[ci, HEADS + h:HEADS + h + 1, :] for ci, h in units])

    r = lax.broadcasted_iota(jnp.int32, (c, c), 0)
    s_ = lax.broadcasted_iota(jnp.int32, (c, c), 1)
    incl, strict, eye = (r >= s_)[None], (r > s_)[None], (r == s_).astype(F32)[None]
    decay = jnp.where(incl, jnp.exp(jnp.where(incl, g_col - g_row, 0.0)), 0.0)
    qk_kk = _dot3(_split_bf16(jnp.concatenate([q, k], axis=1)), _split_bf16(k), _B_NT)
    qk = qk_kk[:, :c] * decay
    a_mat = jnp.where(strict, beta * qk_kk[:, c:] * decay, 0.0)
    inv = _unit_lower_inverse(a_mat, eye)
    e_g = jnp.exp(g_col)
    rhs = jnp.concatenate([beta * v, (beta * e_g) * k], axis=2)
    sol = _dot3(_split_bf16(inv), _split_bf16(rhs), _B_NN)
    u_base, w_mat = sol[:, :, :d], sol[:, :, d:]
    q_dec = (q * e_g).astype(BF16)
    g_last = g_col[:, c - 1:c, :]
    k_dec = k * jnp.exp(g_last - g_col)
    keep = jnp.exp(g_last)

    norm_g = ng_ref[...]
    s = s_ref[...]
    for ci in range(chunks):
        sl = slice(ci * HEADS, (ci + 1) * HEADS)
        s_split = _split_bf16(s)
        u = u_base[sl] - _dot3(_split_bf16(w_mat[sl]), s_split, _B_NN)
        o = (lax.dot_general(q_dec[sl], s_split[0], _B_NN, preferred_element_type=F32)
             + lax.dot_general(qk[sl].astype(BF16), u.astype(BF16), _B_NN, preferred_element_type=F32))
        s = s * keep[sl] + _dot3(_split_bf16(k_dec[sl]), _split_bf16(u), _B_TN)
        for h in range(HEADS):
            y_ref[rows(ci), cols(h)] = _dn_out(o[h], z_ref[rows(ci), cols(h)], norm_g)
    s_ref[...] = s

    @pl.when(t == pl.num_programs(1) - 1)
    def _():
        s_out_ref[0] = s_ref[...]


def _dn_scan(q, k, v, proj, bg, bgt, norm_g, *, batch, seq, chunks):
    n = q.shape[0]
    rows = chunks * DN_CHUNK
    steps = seq // rows
    spec = pl.BlockSpec((rows, G_W), lambda b, t: (b * steps + t, 0))
    ng = norm_g.reshape(1, HEAD_DIM)
    return pl.pallas_call(
        functools.partial(_dn_scan_kernel, chunks=chunks),
        out_shape=(jax.ShapeDtypeStruct((n, G_W), F32),
                   jax.ShapeDtypeStruct((batch, HEADS, HEAD_DIM, HEAD_DIM), F32)),
        grid=(batch, steps),
        in_specs=[spec, spec, spec,
                  pl.BlockSpec((rows, G_W), lambda b, t: (b * steps + t, COL_DN_Z)),
                  pl.BlockSpec((rows, BG_LANES), lambda b, t: (b * steps + t, 0)),
                  pl.BlockSpec((chunks, 2 * HEADS, DN_CHUNK), lambda b, t: (b * steps + t, 0, 0)),
                  pl.BlockSpec((1, HEAD_DIM), lambda b, t: (0, 0))],
        out_specs=(spec, pl.BlockSpec((1, HEADS, HEAD_DIM, HEAD_DIM), lambda b, t: (b, 0, 0, 0))),
        scratch_shapes=[pltpu.VMEM((HEADS, HEAD_DIM, HEAD_DIM), F32)],
        compiler_params=_params("parallel", "arbitrary"),
        name="dn_scan",
    )(q, k, v, proj, bg, bgt, ng)


def _sample_mix_kernel(pool_ref, u_ref, v_ref, qkv_ref, z_ref, bgr_ref, sp_ref, sc_ref, sd_ref,
                       pw_ref, ps_ref, sw0_ref, sb0_ref, cw_ref, na_ref, dt_ref, ng_ref,
                       yp_ref, ys_ref, yd_ref, np_ref, nc_ref, nd_ref, *, bt, pos0):
    new = pool_ref[...]
    for gi, w in enumerate(POOL_WINDOWS):
        cols = slice(gi * HEAD_DIM, (gi + 1) * HEAD_DIM)
        s = new[:, cols]
        for j in range(1, w):
            s = s + sp_ref[POOL_BUF - j, :, cols]
        d = s / float(min(pos0 + 1, w)) - new[:, cols]
        yp_ref[:, cols] = _hdot(d, pw_ref[gi]) * ps_ref[:, cols]
    np_ref[0:POOL_BUF - 1] = sp_ref[1:POOL_BUF]
    np_ref[POOL_BUF - 1] = new

    ys_ref[...] = u_ref[...] * (sw0_ref[...] * v_ref[...] + sb0_ref[...])

    raw = qkv_ref[...]
    conv = raw * cw_ref[CONV_W - 1:CONV_W, :]
    for j in range(CONV_W - 1):
        conv = conv + sc_ref[j] * cw_ref[j:j + 1, :]
    nc_ref[0:CONV_W - 2] = sc_ref[1:CONV_W - 1]
    nc_ref[CONV_W - 2] = raw
    act = _silu(conv)
    bg = _beta_and_log_decay(bgr_ref[...], na_ref[...], dt_ref[...])
    norm_g = ng_ref[...]
    qs, ks = [], []
    for h in range(HEADS):
        qs.append(_l2_normalize(act[:, h * HEAD_DIM:(h + 1) * HEAD_DIM]) * (HEAD_DIM ** -0.5))
        ks.append(_l2_normalize(act[:, G_W + h * HEAD_DIM:G_W + (h + 1) * HEAD_DIM]))
    pad = jnp.zeros((HEAD_DIM - bt, HEAD_DIM), F32)
    for h in range(HEADS):
        cols = slice(h * HEAD_DIM, (h + 1) * HEAD_DIM)
        v_h = act[:, 2 * G_W + h * HEAD_DIM:2 * G_W + (h + 1) * HEAD_DIM]
        z_h = z_ref[:, cols]
        k_t = jnp.concatenate([ks[h], pad], axis=0).T
        q_t = jnp.concatenate([qs[h], pad], axis=0).T
        for b in range(bt):
            s0 = sd_ref[b, h]
            k_col = jnp.broadcast_to(k_t[:, b:b + 1], (HEAD_DIM, HEAD_DIM))
            q_col = jnp.broadcast_to(q_t[:, b:b + 1], (HEAD_DIM, HEAD_DIM))
            beta = bg[b:b + 1, h:h + 1]
            a = jnp.exp(bg[b:b + 1, HEADS + h:HEADS + h + 1])
            k_s = jnp.sum(k_col * s0, axis=0, keepdims=True)
            u = beta * (v_h[b:b + 1, :] - a * k_s)
            s1 = a * s0 + k_col * u
            nd_ref[b, h] = s1
            o = jnp.sum(q_col * s1, axis=0, keepdims=True)
            yd_ref[b:b + 1, cols] = _dn_out(o, z_h[b:b + 1, :], norm_g)


def _sample_mix(proj, bg_raw, state_pool, state_conv, state_delta, pool_w, pool_scale, sgu_w0, sgu_b0,
                conv_w, neg_a_row, dt_row, norm_g, *, bt, pos0):
    bs = proj.shape[0]
    qkv_w = 3 * G_W
    col = lambda c: pl.BlockSpec((bt, G_W), lambda i: (i, c))
    whole = lambda a: pl.BlockSpec(a.shape, lambda i: (0,) * a.ndim)
    row = pl.BlockSpec((bt, G_W), lambda i: (i, 0))
    ps = pool_scale.reshape(1, G_W)
    ng = norm_g.reshape(1, HEAD_DIM)
    consts = (pool_w, ps, sgu_w0, sgu_b0, conv_w, neg_a_row, dt_row, ng)
    out_sds = lambda a: jax.ShapeDtypeStruct(a.shape, F32)
    pool_spec = pl.BlockSpec((POOL_BUF, bt, G_W), lambda i: (0, i, 0))
    conv_spec = pl.BlockSpec((CONV_W - 1, bt, qkv_w), lambda i: (0, i, 0))
    delta_spec = pl.BlockSpec((bt, HEADS, HEAD_DIM, HEAD_DIM), lambda i: (i, 0, 0, 0))
    y = jax.ShapeDtypeStruct((bs, G_W), F32)
    return pl.pallas_call(
        functools.partial(_sample_mix_kernel, bt=bt, pos0=pos0),
        out_shape=(y, y, y, out_sds(state_pool), out_sds(state_conv), out_sds(state_delta)),
        grid=(bs // bt,),
        in_specs=[col(COL_POOL), col(COL_SGU_U), col(COL_SGU_V),
                  pl.BlockSpec((bt, qkv_w), lambda i: (i, COL_DN_QKV * G_W // qkv_w)),
                  col(COL_DN_Z),
                  pl.BlockSpec((bt, BG_LANES), lambda i: (i, 0)),
                  pool_spec, conv_spec, delta_spec] + [whole(a) for a in consts],
        out_specs=(row, row, row, pool_spec, conv_spec, delta_spec),
        compiler_params=_params("parallel"),
        name="sample_mix",
    )(proj, proj, proj, proj, proj, bg_raw, state_pool, state_conv, state_delta, *consts)


def _lane_row(values, offset):
    return jnp.zeros((1, BG_LANES), F32).at[0, offset:offset + values.shape[0]].set(values)


def kernel(x_prompt, x_sample, cache_k, cache_v, page_table, state_pool, state_conv, state_delta,
           ln_g, ln_b, w_ffn1_in, w_ffn1_out, w_ffn2_in, w_ffn2_out, w_in, w_out,
           pool_w, pool_scale, sgu_w, sgu_b, sb_bias, dn_conv_w, dn_a_log, dn_dt_bias, dn_norm_g):
    depth = ln_g.shape[0]
    alpha = (2.0 * depth) ** 0.25
    bp, seq, d_model = x_prompt.shape
    bs, dec_seq, _ = x_sample.shape
    assert dec_seq == 1
    n_phys, page = cache_k.shape[1], cache_k.shape[2]
    n_pages = page_table.shape[1]
    pos0 = n_pages * page
    ck = cache_k.reshape(depth * n_phys * page * HEADS, HEAD_DIM)
    cv = cache_v.reshape(depth * n_phys * page * HEADS, HEAD_DIM)
    hp = x_prompt.reshape(bp * seq, d_model)
    hs = x_sample.reshape(bs, d_model)
    pool_tm = jnp.swapaxes(state_pool, 1, 2)
    conv_tm = jnp.swapaxes(state_conv, 1, 2)
    tm_p = 512
    tm_s = bs

    outs = {name: [] for name in ("kp", "vp", "poolp", "convp", "sp", "ks", "vs", "pools", "convs", "ss", "sgus")}
    for l in range(depth):
        w1i, w1o = w_ffn1_in[l].astype(BF16), w_ffn1_out[l].astype(BF16)
        w2i, w2o = w_ffn2_in[l].astype(BF16), w_ffn2_out[l].astype(BF16)
        wi = w_in[l].astype(BF16)
        wi_main = wi[:, :N_MAIN_COLS]
        wi_tail = jnp.pad(wi[:, N_MAIN_COLS:], ((0, 0), (0, BG_LANES - (wi.shape[1] - N_MAIN_COLS))))
        wo = w_out[l].astype(BF16)
        neg_a_row = _lane_row(-jnp.exp(dn_a_log[l]), HEADS)
        dt_row = _lane_row(dn_dt_bias[l], HEADS)

        h1 = _ffn_ln(hp, w1i, w1o, ln_g[l, 0], ln_b[l, 0], alpha=alpha, tm=tm_p, tf=512)
        proj, bg_raw, k_rows, v_rows = _in_proj(h1, wi_main, wi_tail, tm=256)
        y_pool, y_sgu = _pool_sgu(proj, pool_w[l], pool_scale[l], sgu_w[l], sgu_b[l], seq=seq)
        y_sb = _sb_attn(proj, sb_bias[l], batch=bp, seq=seq, blk=256)
        dq, dk, dv, bg, bgt = _dn_prep(proj, bg_raw, dn_conv_w[l], neg_a_row, dt_row, seq=seq, rows=256)
        y_dn, s_end = _dn_scan(dq, dk, dv, proj, bg, bgt, dn_norm_g[l], batch=bp, seq=seq, chunks=4)
        h2 = _out_proj_ln(h1, (y_pool, y_sgu, y_sb, y_dn), wo, ln_g[l, 1], ln_b[l, 1], alpha=alpha, tm=256)
        hp = _ffn_ln(h2, w2i, w2o, ln_g[l, 2], ln_b[l, 2], alpha=alpha, tm=tm_p, tf=512)
        proj3 = proj.reshape(bp, seq, N_MAIN_COLS)
        outs["kp"].append(k_rows.reshape(bp, seq, HEADS, HEAD_DIM))
        outs["vp"].append(v_rows.reshape(bp, seq, HEADS, HEAD_DIM))
        outs["poolp"].append(proj3[:, seq - POOL_BUF:, :G_W])
        outs["convp"].append(proj3[:, seq - (CONV_W - 1):, COL_DN_QKV * G_W:(COL_DN_QKV + 3) * G_W])
        outs["sp"].append(s_end)

        g1 = _ffn_ln(hs, w1i, w1o, ln_g[l, 0], ln_b[l, 0], alpha=alpha, tm=tm_s, tf=512)
        sproj, sbg_raw, sk_rows, sv_rows = _in_proj(g1, wi_main, wi_tail, tm=tm_s)
        sgu_w0 = jnp.repeat(sgu_w[l, :, 0, 0], HEAD_DIM).reshape(1, G_W)
        sgu_b0 = jnp.repeat(sgu_b[l, :, 0], HEAD_DIM).reshape(1, G_W)
        sy_pool, sy_sgu, sy_dn, new_pool, new_conv, new_delta = _sample_mix(
            sproj, sbg_raw, pool_tm[l], conv_tm[l], state_delta[l], pool_w[l], pool_scale[l],
            sgu_w0, sgu_b0, dn_conv_w[l], neg_a_row, dt_row, dn_norm_g[l], bt=8, pos0=pos0)
        bias_row = jnp.tile(sb_bias[l], page).reshape(1, page * HEADS)
        phys = (page_table + l * n_phys).reshape(-1)
        sy_sb = _sb_decode(sproj.reshape(bs, 1, N_MAIN_COLS), bias_row, ck, cv, phys,
                           n_pages=n_pages, width=page * HEADS)
        g2 = _out_proj_ln(g1, (sy_pool, sy_sgu, sy_sb, sy_dn), wo, ln_g[l, 1], ln_b[l, 1], alpha=alpha, tm=tm_s)
        hs = _ffn_ln(g2, w2i, w2o, ln_g[l, 2], ln_b[l, 2], alpha=alpha, tm=tm_s, tf=512)
        outs["ks"].append(sk_rows.reshape(bs, 1, HEADS, HEAD_DIM))
        outs["vs"].append(sv_rows.reshape(bs, 1, HEADS, HEAD_DIM))
        outs["pools"].append(new_pool)
        outs["convs"].append(new_conv)
        outs["ss"].append(new_delta)
        outs["sgus"].append(sproj[:, COL_SGU_V * G_W:(COL_SGU_V + 1) * G_W].reshape(bs, 1, G_W))

    st = lambda name: jnp.stack(outs[name])
    return (hp.reshape(bp, seq, d_model), hs.reshape(bs, 1, d_model),
            st("kp"), st("vp"), st("poolp"), st("convp"), st("sp"),
            st("ks"), st("vs"), jnp.swapaxes(st("pools"), 1, 2), jnp.swapaxes(st("convs"), 1, 2),
            st("ss"), st("sgus"))
```

```python
import functools

import jax
import jax.numpy as jnp
import numpy as np
from jax import lax
from jax.experimental import pallas as pl
from jax.experimental.pallas import tpu as pltpu

F32 = jnp.float32
BF16 = jnp.bfloat16
HIGHEST = lax.Precision.HIGHEST

HEADS = 4
HEAD_DIM = 128
G_W = HEADS * HEAD_DIM
POOL_WINDOWS = (2, 4, 8, 16)
POOL_BUF = max(POOL_WINDOWS) - 1
POOL_HALO = 16
SGU_CHUNK = 128
DN_CHUNK = 64
CONV_W = 4
CONV_HALO = 8
LN_EPS = 1e-5
NORM_EPS = 1e-6
VMEM_LIMIT_BYTES = 48 * 1024 * 1024

COL_POOL, COL_SGU_U, COL_SGU_V, COL_SB_Q, COL_SB_K, COL_SB_V, COL_DN_QKV, COL_DN_Z = 0, 1, 2, 3, 4, 5, 6, 9
N_MAIN_COLS = 10 * G_W
BG_LANES = 128


def _params(*sem):
    return pltpu.CompilerParams(dimension_semantics=sem, vmem_limit_bytes=VMEM_LIMIT_BYTES)


def _hdot(a, b):
    return jnp.dot(a, b, precision=HIGHEST, preferred_element_type=F32)


def _hdot_nt(a, b):
    return lax.dot_general(a, b, (((1,), (1,)), ((), ())), precision=HIGHEST, preferred_element_type=F32)


def _bdot(a, b):
    return jnp.dot(a.astype(BF16), b.astype(BF16), preferred_element_type=F32)


def _bdot_nt(a, b):
    return lax.dot_general(a.astype(BF16), b.astype(BF16), (((1,), (1,)), ((), ())),
                           preferred_element_type=F32)


def _silu(x):
    return x * jax.nn.sigmoid(x)


def _softplus(x):
    return jnp.maximum(x, 0.0) + jnp.log1p(jnp.exp(-jnp.abs(x)))


def _layer_norm(y, g, b):
    mu = jnp.mean(y, -1, keepdims=True)
    d = y - mu
    var = jnp.mean(d * d, -1, keepdims=True)
    return d * lax.rsqrt(var + LN_EPS) * g + b


def _split_bf16(x):
    hi = x.astype(BF16)
    lo = (x - hi.astype(F32)).astype(BF16)
    return hi, lo


def _dot3(a, b, dims=(((1,), (0,)), ((), ()))):
    (ah, al), (bh, bl) = a, b
    dot = functools.partial(lax.dot_general, dimension_numbers=dims, preferred_element_type=F32)
    return dot(ah, bh) + dot(al, bh) + dot(ah, bl)


_NT_DIMS = (((1,), (1,)), ((), ()))
_B_NN = (((2,), (1,)), ((0,), (0,)))
_B_NT = (((2,), (2,)), ((0,), (0,)))
_B_TN = (((1,), (1,)), ((0,), (0,)))


def _ffn_ln_kernel(x_ref, wg_ref, wu_ref, wo_ref, g_ref, b_ref, o_ref, *rest, alpha, emit_bf16):
    if emit_bf16:
        wg_out, wu_out, wo_out, xb_ref, acc_ref = rest
    else:
        xb_ref, acc_ref = rest
    f = pl.program_id(1)

    @pl.when(f == 0)
    def _():
        xb_ref[...] = x_ref[...].astype(BF16)
        acc_ref[...] = jnp.zeros_like(acc_ref)

    wg, wu, wo = wg_ref[...].astype(BF16), wu_ref[...].astype(BF16), wo_ref[...].astype(BF16)
    if emit_bf16:
        wg_out[...], wu_out[...], wo_out[...] = wg, wu, wo
    xb = xb_ref[...]
    gate = jnp.dot(xb, wg, preferred_element_type=F32)
    up = jnp.dot(xb, wu, preferred_element_type=F32)
    act = (_silu(gate) * up).astype(BF16)
    acc_ref[...] += jnp.dot(act, wo, preferred_element_type=F32)

    @pl.when(f == pl.num_programs(1) - 1)
    def _():
        o_ref[...] = _layer_norm(alpha * x_ref[...] + 0.5 * acc_ref[...], g_ref[...], b_ref[...])


def _ffn_ln(x, weights, ln_g, ln_b, *, layer, which, alpha, tm, tf):
    n, d = x.shape
    emit_bf16 = len(weights) == 2
    if emit_bf16:
        w_in, w_out = weights
        dff = w_out.shape[1]
        nf = dff // tf
        operands = (w_in, w_in, w_out)
        w_specs = [pl.BlockSpec((None, d, tf), lambda i, f: (layer, 0, f)),
                   pl.BlockSpec((None, d, tf), lambda i, f: (layer, 0, f + nf)),
                   pl.BlockSpec((None, tf, d), lambda i, f: (layer, f, 0))]
    else:
        operands = weights
        dff = weights[2].shape[0]
        nf = dff // tf
        w_specs = [pl.BlockSpec((d, tf), lambda i, f: (0, f)),
                   pl.BlockSpec((d, tf), lambda i, f: (0, f)),
                   pl.BlockSpec((tf, d), lambda i, f: (f, 0))]
    ln_spec = pl.BlockSpec((None, None, 1, d), lambda i, f: (layer, which, 0, 0))
    out_shape = [jax.ShapeDtypeStruct((n, d), F32)]
    out_specs = [pl.BlockSpec((tm, d), lambda i, f: (i, 0))]
    if emit_bf16:
        assert n == tm
        out_shape += [jax.ShapeDtypeStruct((d, dff), BF16), jax.ShapeDtypeStruct((d, dff), BF16),
                      jax.ShapeDtypeStruct((dff, d), BF16)]
        out_specs += [pl.BlockSpec((d, tf), lambda i, f: (0, f)), pl.BlockSpec((d, tf), lambda i, f: (0, f)),
                      pl.BlockSpec((tf, d), lambda i, f: (f, 0))]
    return pl.pallas_call(
        functools.partial(_ffn_ln_kernel, alpha=alpha, emit_bf16=emit_bf16),
        out_shape=tuple(out_shape),
        grid=(n // tm, nf),
        in_specs=[pl.BlockSpec((tm, d), lambda i, f: (i, 0))] + w_specs + [ln_spec, ln_spec],
        out_specs=tuple(out_specs),
        scratch_shapes=[pltpu.VMEM((tm, d), BF16), pltpu.VMEM((tm, d), F32)],
        compiler_params=_params("parallel", "arbitrary"),
        name="ffn_ln_cast" if emit_bf16 else "ffn_ln",
    )(x, *operands, ln_g.reshape(ln_g.shape[0], ln_g.shape[1], 1, d), ln_b.reshape(ln_b.shape[0], ln_b.shape[1], 1, d))


def _store_head_rows(dst_ref, block, tm):
    for h in range(HEADS):
        dst_ref[pl.ds(h, tm, stride=HEADS), :] = block[:, h * HEAD_DIM:(h + 1) * HEAD_DIM]


def _in_proj_kernel(*refs, tm, aliased):
    h_ref, w_ref, wt_ref = refs[:3]
    o_ref, bg_ref, k_ref, v_ref = refs[3 + aliased:]
    hb = h_ref[...].astype(BF16)
    bg_ref[...] = jnp.dot(hb, wt_ref[...], preferred_element_type=F32)
    o = jnp.dot(hb, w_ref[...], preferred_element_type=F32)
    o_ref[...] = o
    _store_head_rows(k_ref, o[:, COL_SB_K * G_W:(COL_SB_K + 1) * G_W], tm)
    _store_head_rows(v_ref, o[:, COL_SB_V * G_W:(COL_SB_V + 1) * G_W], tm)


def _in_proj(h, w_main, w_tail, kv_prev, *, layer, depth, tm):
    n, d = h.shape
    steps = n // tm
    resident = pl.Buffered(1)
    rows_kv = jax.ShapeDtypeStruct((depth * n * HEADS, HEAD_DIM), F32)
    kv_spec = pl.BlockSpec((tm * HEADS, HEAD_DIM), lambda i: (layer * steps + i, 0))
    aliased = 0 if kv_prev is None else 2
    prev = () if kv_prev is None else tuple(kv_prev)
    return pl.pallas_call(
        functools.partial(_in_proj_kernel, tm=tm, aliased=aliased),
        out_shape=(jax.ShapeDtypeStruct((n, N_MAIN_COLS), F32), jax.ShapeDtypeStruct((n, BG_LANES), F32),
                   rows_kv, rows_kv),
        grid=(steps,),
        in_specs=[
            pl.BlockSpec((tm, d), lambda i: (i, 0)),
            pl.BlockSpec((d, N_MAIN_COLS), lambda i: (0, 0), pipeline_mode=resident),
            pl.BlockSpec((None, d, BG_LANES), lambda i: (layer, 0, 0), pipeline_mode=resident),
        ] + [pl.BlockSpec(memory_space=pl.ANY)] * aliased,
        out_specs=(pl.BlockSpec((tm, N_MAIN_COLS), lambda i: (i, 0)),
                   pl.BlockSpec((tm, BG_LANES), lambda i: (i, 0)), kv_spec, kv_spec),
        input_output_aliases={3: 2, 4: 3} if aliased else {},
        compiler_params=_params("parallel"),
        name="in_proj",
    )(h, w_main, w_tail, *prev)


def _in_proj_cast_kernel(h_ref, w_ref, wt_ref, o_ref, bg_ref, k_ref, v_ref, wb_ref, *, tm):
    j = pl.program_id(0)
    hb = h_ref[...].astype(BF16)
    wb = w_ref[...].T.astype(BF16)
    wb_ref[...] = wb
    o = jnp.dot(hb, wb, preferred_element_type=F32)
    o_ref[...] = o

    @pl.when(j == 0)
    def _():
        bg_ref[...] = jnp.dot(hb, wt_ref[...], preferred_element_type=F32)

    @pl.when(j == COL_SB_K)
    def _():
        _store_head_rows(k_ref, o, tm)

    @pl.when(j == COL_SB_V)
    def _():
        _store_head_rows(v_ref, o, tm)


def _in_proj_cast(h, w_in_t, w_tail, *, layer):
    n, d = h.shape
    rows_kv = jax.ShapeDtypeStruct((n * HEADS, HEAD_DIM), F32)
    kv_spec = pl.BlockSpec((n * HEADS, HEAD_DIM), lambda j: (0, 0))
    return pl.pallas_call(
        functools.partial(_in_proj_cast_kernel, tm=n),
        out_shape=(jax.ShapeDtypeStruct((n, N_MAIN_COLS), F32), jax.ShapeDtypeStruct((n, BG_LANES), F32),
                   rows_kv, rows_kv, jax.ShapeDtypeStruct((d, N_MAIN_COLS), BF16)),
        grid=(N_MAIN_COLS // G_W,),
        in_specs=[
            pl.BlockSpec((n, d), lambda j: (0, 0)),
            pl.BlockSpec((None, G_W, d), lambda j: (layer, j, 0)),
            pl.BlockSpec((None, d, BG_LANES), lambda j: (layer, 0, 0)),
        ],
        out_specs=(pl.BlockSpec((n, G_W), lambda j: (0, j)),
                   pl.BlockSpec((n, BG_LANES), lambda j: (0, 0)), kv_spec, kv_spec,
                   pl.BlockSpec((d, G_W), lambda j: (0, j))),
        compiler_params=_params("arbitrary"),
        name="in_proj_cast",
    )(h, w_in_t, w_tail)


def _out_proj_ln_kernel(h_ref, y0_ref, y1_ref, y2_ref, y3_ref, w_ref, g_ref, b_ref, o_ref, *, alpha):
    acc = None
    for m, y_ref in enumerate((y0_ref, y1_ref, y2_ref, y3_ref)):
        part = jnp.dot(y_ref[...].astype(BF16), w_ref[m * G_W:(m + 1) * G_W, :], preferred_element_type=F32)
        acc = part if acc is None else acc + part
    o_ref[...] = _layer_norm(alpha * h_ref[...] + acc, g_ref[...], b_ref[...])


def _out_proj_ln(h, ys, w_out, ln_g, ln_b, *, layer, alpha, tm):
    n, d = h.shape
    y_spec = pl.BlockSpec((tm, G_W), lambda i: (i, 0))
    ln_spec = pl.BlockSpec((None, None, 1, d), lambda i: (layer, 1, 0, 0))
    return pl.pallas_call(
        functools.partial(_out_proj_ln_kernel, alpha=alpha),
        out_shape=jax.ShapeDtypeStruct((n, d), F32),
        grid=(n // tm,),
        in_specs=[pl.BlockSpec((tm, d), lambda i: (i, 0)), y_spec, y_spec, y_spec, y_spec,
                  pl.BlockSpec((None, 4 * G_W, d), lambda i: (layer, 0, 0)), ln_spec, ln_spec],
        out_specs=pl.BlockSpec((tm, d), lambda i: (i, 0)),
        compiler_params=_params("parallel"),
        name="out_proj_ln",
    )(h, *ys, w_out, ln_g.reshape(ln_g.shape[0], ln_g.shape[1], 1, d), ln_b.reshape(ln_b.shape[0], ln_b.shape[1], 1, d))


def _pool_sgu_kernel(cur_ref, halo_ref, u_ref, v_ref, pw_ref, ps_ref, sw_ref, sbc_ref,
                     yp_ref, ys_ref, ext_ref, *, chunks_per_seq):
    c = pl.program_id(0) % chunks_per_seq
    cur = cur_ref[...]
    ext_ref[0:POOL_HALO, :] = jnp.where(c == 0, 0.0, halo_ref[...])
    ext_ref[POOL_HALO:, :] = cur
    rows = SGU_CHUNK
    pos = c * rows + lax.broadcasted_iota(jnp.int32, (rows, 1), 0)
    for gi, w in enumerate(POOL_WINDOWS):
        cols = slice(gi * HEAD_DIM, (gi + 1) * HEAD_DIM)
        s = cur[:, cols]
        for j in range(1, w):
            s = s + ext_ref[POOL_HALO - j:POOL_HALO - j + rows, cols]
        cnt = jnp.minimum(pos + 1, w).astype(F32)
        d = s / cnt - cur[:, cols]
        yp_ref[:, cols] = _hdot(d, pw_ref[gi]) * ps_ref[:, cols]

    r = lax.broadcasted_iota(jnp.int32, (rows, rows), 0)
    s_ = lax.broadcasted_iota(jnp.int32, (rows, rows), 1)
    causal = r >= s_
    for h in range(HEADS):
        cols = slice(h * HEAD_DIM, (h + 1) * HEAD_DIM)
        wm = jnp.where(causal, sw_ref[h], 0.0)
        mixed = _hdot(wm, v_ref[:, cols]) + sbc_ref[:, h:h + 1]
        ys_ref[:, cols] = u_ref[:, cols] * mixed


def _pool_sgu(proj, pool_w, pool_scale, sgu_w, sgu_b, *, seq):
    n = proj.shape[0]
    rows = SGU_CHUNK
    halo_blocks = rows // POOL_HALO
    blk = lambda col: pl.BlockSpec((rows, G_W), lambda g: (g, col))
    whole = lambda a: pl.BlockSpec(a.shape, lambda g: (0,) * a.ndim)
    ps = pool_scale.reshape(1, G_W)
    sbc = sgu_b.T
    return pl.pallas_call(
        functools.partial(_pool_sgu_kernel, chunks_per_seq=seq // rows),
        out_shape=(jax.ShapeDtypeStruct((n, G_W), F32), jax.ShapeDtypeStruct((n, G_W), F32)),
        grid=(n // rows,),
        in_specs=[blk(COL_POOL),
                  pl.BlockSpec((POOL_HALO, G_W), lambda g: (jnp.maximum(g * halo_blocks - 1, 0), COL_POOL)),
                  blk(COL_SGU_U), blk(COL_SGU_V), whole(pool_w), whole(ps), whole(sgu_w), whole(sbc)],
        out_specs=(pl.BlockSpec((rows, G_W), lambda g: (g, 0)), pl.BlockSpec((rows, G_W), lambda g: (g, 0))),
        scratch_shapes=[pltpu.VMEM((POOL_HALO + rows, G_W), F32)],
        compiler_params=_params("parallel"),
        name="pool_sgu",
    )(proj, proj, proj, proj, pool_w, ps, sgu_w, sbc)


def _sb_attn_kernel(bias_ref, q_ref, k_ref, v_ref, o_ref, *, blk, scale):
    qi = pl.program_id(1)
    heads = lambda x: jnp.stack([x[:, h * HEAD_DIM:(h + 1) * HEAD_DIM] for h in range(HEADS)])
    q = heads(q_ref[...]).astype(BF16)
    bias = bias_ref[...]
    row = lax.broadcasted_iota(jnp.int32, (blk, blk), 0)
    col = lax.broadcasted_iota(jnp.int32, (blk, blk), 1)
    after = (row > col).astype(BF16)

    def visit(j, carry, valid):
        o, run = carry
        start = pl.multiple_of(j * blk, blk)
        kb = heads(k_ref[pl.ds(start, blk), :]).astype(BF16)
        vb = heads(v_ref[pl.ds(start, blk), :]).astype(BF16)
        z = lax.dot_general(q, kb, _B_NT, preferred_element_type=F32) * scale + bias
        sp = _softplus(z)
        log_fail = -sp if valid is None else jnp.where(valid, -sp, 0.0)
        hi, lo = _split_bf16(log_fail.reshape(HEADS * blk, blk))
        later = (jnp.dot(hi, after, preferred_element_type=F32)
                 + jnp.dot(lo, after, preferred_element_type=F32)).reshape(HEADS, blk, blk) + run
        att = jnp.exp(z - sp + later)
        if valid is not None:
            att = jnp.where(valid, att, 0.0)
        o = o + lax.dot_general(att.astype(BF16), vb, _B_NN, preferred_element_type=F32)
        run = run + jnp.sum(log_fail, axis=-1, keepdims=True)
        return o, run

    init = (jnp.zeros((HEADS, blk, HEAD_DIM), F32), jnp.zeros((HEADS, blk, 1), F32))
    carry = visit(qi, init, (col < row)[None])
    o, _ = lax.fori_loop(0, qi, lambda step, carry: visit(qi - 1 - step, carry, None), carry)
    for h in range(HEADS):
        o_ref[:, h * HEAD_DIM:(h + 1) * HEAD_DIM] = o[h]


def _sb_attn(proj, sb_bias, *, batch, seq, blk):
    n = proj.shape[0]
    nq = seq // blk
    return pl.pallas_call(
        functools.partial(_sb_attn_kernel, blk=blk, scale=HEAD_DIM ** -0.5),
        out_shape=jax.ShapeDtypeStruct((n, G_W), F32),
        grid=(batch, nq),
        in_specs=[pl.BlockSpec((HEADS, 1, 1), lambda b, i: (0, 0, 0)),
                  pl.BlockSpec((blk, G_W), lambda b, i: (b * nq + i, COL_SB_Q)),
                  pl.BlockSpec((seq, G_W), lambda b, i: (b, COL_SB_K)),
                  pl.BlockSpec((seq, G_W), lambda b, i: (b, COL_SB_V))],
        out_specs=pl.BlockSpec((blk, G_W), lambda b, i: (b * nq + i, 0)),
        compiler_params=_params("parallel", "arbitrary"),
        name="sb_attn",
    )(sb_bias.reshape(HEADS, 1, 1), proj, proj, proj)


def _sb_decode_kernel(pt_ref, q_ref, bias_ref, cum_ref, *refs, n_pages, width, scale):
    del pt_ref
    k_refs = refs[:n_pages]
    v_refs = refs[n_pages:2 * n_pages]
    o_ref = refs[2 * n_pages]
    rows = 8
    q = q_ref[0]
    q_rows = jnp.concatenate([q[:, h * HEAD_DIM:(h + 1) * HEAD_DIM] for h in range(HEADS)]
                             + [jnp.zeros((rows - HEADS, HEAD_DIM), F32)], axis=0).astype(BF16)
    rid = lax.broadcasted_iota(jnp.int32, (rows, width), 0)
    cid = lax.broadcasted_iota(jnp.int32, (rows, width), 1)
    own = (cid & (HEADS - 1)) == rid

    z_rows = []
    for p in range(n_pages):
        zf = _bdot_nt(q_rows, k_refs[p][...])
        z_rows.append(jnp.sum(jnp.where(own, zf, 0.0), axis=0, keepdims=True))
    z = jnp.concatenate(z_rows, axis=0) * scale + bias_ref[...]
    sp = _softplus(z)
    hi, lo = _split_bf16(-sp)
    cum = (jnp.dot(hi, cum_ref[...], preferred_element_type=F32)
           + jnp.dot(lo, cum_ref[...], preferred_element_type=F32))
    in_page, page_tot = cum[:, :width], cum[:, width:]
    pr = lax.broadcasted_iota(jnp.int32, (n_pages, n_pages), 0)
    pc = lax.broadcasted_iota(jnp.int32, (n_pages, n_pages), 1)
    later_pages = _hdot((pc > pr).astype(F32), page_tot)
    att = jnp.exp(z - sp + in_page + later_pages)

    acc = jnp.zeros((rows, HEAD_DIM), F32)
    for p in range(n_pages):
        a_p = jnp.where(own, jnp.broadcast_to(att[p:p + 1, :], (rows, width)), 0.0)
        acc = acc + _bdot(a_p, v_refs[p][...])
    for h in range(HEADS):
        o_ref[0, :, h * HEAD_DIM:(h + 1) * HEAD_DIM] = acc[h:h + 1, :]


def _same_head_sums(width):
    src = np.arange(width)[:, None]
    dst = np.arange(width)[None, :]
    same = (src - dst) % HEADS == 0
    return jnp.asarray(np.concatenate([same & (src > dst), same], axis=1), BF16)


def _sb_decode(proj3, bias_row, cache_k, cache_v, phys_pages, *, n_pages, width):
    bs = proj3.shape[0]

    def page_spec(p):
        return pl.BlockSpec((width, HEAD_DIM), lambda b, pt: (pt[b * n_pages + p], 0))

    grid_spec = pltpu.PrefetchScalarGridSpec(
        num_scalar_prefetch=1,
        grid=(bs,),
        in_specs=[pl.BlockSpec((1, 1, G_W), lambda b, pt: (b, 0, COL_SB_Q)),
                  pl.BlockSpec((1, width), lambda b, pt: (0, 0)),
                  pl.BlockSpec((width, 2 * width), lambda b, pt: (0, 0))]
                 + [page_spec(p) for p in range(n_pages)] + [page_spec(p) for p in range(n_pages)],
        out_specs=pl.BlockSpec((1, 1, G_W), lambda b, pt: (b, 0, 0)),
    )
    out = pl.pallas_call(
        functools.partial(_sb_decode_kernel, n_pages=n_pages, width=width, scale=HEAD_DIM ** -0.5),
        out_shape=jax.ShapeDtypeStruct((bs, 1, G_W), F32),
        grid_spec=grid_spec,
        compiler_params=_params("arbitrary"),
        name="sb_decode",
    )(phys_pages, proj3, bias_row, _same_head_sums(width), *([cache_k] * n_pages), *([cache_v] * n_pages))
    return out.reshape(bs, G_W)


def _beta_and_log_decay(raw, neg_a_row, dt_row):
    lane = lax.broadcasted_iota(jnp.int32, raw.shape, 1)
    beta = jax.nn.sigmoid(raw)
    g = neg_a_row * _softplus(raw + dt_row)
    return jnp.where(lane < HEADS, beta, g)


def _l2_normalize(x):
    return x * lax.rsqrt(jnp.sum(x * x, -1, keepdims=True) + NORM_EPS)


def _dn_prep_kernel(raw_ref, halo_ref, bgr_ref, cw_ref, na_ref, dt_ref,
                    q_ref, k_ref, v_ref, bg_ref, bgt_ref, ext_ref, *, tiles_per_seq, rows):
    first = (pl.program_id(0) % tiles_per_seq) == 0
    ext_ref[0:CONV_HALO, :] = jnp.where(first, 0.0, halo_ref[...])
    ext_ref[CONV_HALO:, :] = raw_ref[...]
    conv = None
    for j in range(CONV_W):
        off = CONV_HALO - (CONV_W - 1) + j
        term = ext_ref[off:off + rows, :] * cw_ref[j:j + 1, :]
        conv = term if conv is None else conv + term
    act = _silu(conv)
    for h in range(HEADS):
        cols = slice(h * HEAD_DIM, (h + 1) * HEAD_DIM)
        q_ref[:, cols] = _l2_normalize(act[:, h * HEAD_DIM:(h + 1) * HEAD_DIM]) * (HEAD_DIM ** -0.5)
        k_ref[:, cols] = _l2_normalize(act[:, G_W + h * HEAD_DIM:G_W + (h + 1) * HEAD_DIM])
    v_ref[...] = act[:, 2 * G_W:]
    bg = _beta_and_log_decay(bgr_ref[...], na_ref[...], dt_ref[...])
    r = lax.broadcasted_iota(jnp.int32, (rows, rows), 0)
    s = lax.broadcasted_iota(jnp.int32, (rows, rows), 1)
    shift = DN_CHUNK.bit_length() - 1
    chunk_prefix = ((r >= s) & (lax.shift_right_logical(r, shift) == lax.shift_right_logical(s, shift)))
    lane = lax.broadcasted_iota(jnp.int32, bg.shape, 1)
    bg = jnp.where(lane < HEADS, bg, _hdot(chunk_prefix.astype(F32), bg))
    bg_ref[...] = bg
    per_tile = BG_LANES // DN_CHUNK
    for t in range(rows // BG_LANES):
        tile_t = bg[t * BG_LANES:(t + 1) * BG_LANES, :].T
        for c in range(per_tile):
            bgt_ref[t * per_tile + c] = tile_t[0:2 * HEADS, c * DN_CHUNK:(c + 1) * DN_CHUNK]


def _dn_prep(proj, bg_raw, conv_w, neg_a_row, dt_row, *, seq, rows):
    n = proj.shape[0]
    halo_blocks = rows // CONV_HALO
    qkv_w = 3 * G_W
    col = COL_DN_QKV * G_W // qkv_w
    whole = lambda a: pl.BlockSpec(a.shape, lambda g: (0,) * a.ndim)
    out = jax.ShapeDtypeStruct((n, G_W), F32)
    o_spec = pl.BlockSpec((rows, G_W), lambda g: (g, 0))
    return pl.pallas_call(
        functools.partial(_dn_prep_kernel, tiles_per_seq=seq // rows, rows=rows),
        out_shape=(out, out, out, jax.ShapeDtypeStruct((n, BG_LANES), F32),
                   jax.ShapeDtypeStruct((n // DN_CHUNK, 2 * HEADS, DN_CHUNK), F32)),
        grid=(n // rows,),
        in_specs=[pl.BlockSpec((rows, qkv_w), lambda g: (g, col)),
                  pl.BlockSpec((CONV_HALO, qkv_w), lambda g: (jnp.maximum(g * halo_blocks - 1, 0), col)),
                  pl.BlockSpec((rows, BG_LANES), lambda g: (g, 0)),
                  whole(conv_w), whole(neg_a_row), whole(dt_row)],
        out_specs=(o_spec, o_spec, o_spec, pl.BlockSpec((rows, BG_LANES), lambda g: (g, 0)),
                   pl.BlockSpec((rows // DN_CHUNK, 2 * HEADS, DN_CHUNK), lambda g: (g, 0, 0))),
        scratch_shapes=[pltpu.VMEM((CONV_HALO + rows, qkv_w), F32)],
        compiler_params=_params("parallel"),
        name="dn_prep",
    )(proj, proj, bg_raw, conv_w, neg_a_row, dt_row)


def _unit_lower_inverse(a_mat, eye):
    n = a_mat.shape[-1]
    inv = eye - a_mat
    a_split = _split_bf16(a_mat)
    power = _dot3(a_split, a_split, _B_NN)
    span = 2
    while span < n:
        p_split = _split_bf16(power)
        inv = inv + _dot3(_split_bf16(inv), p_split, _B_NN)
        span *= 2
        if span < n:
            power = _dot3(p_split, p_split, _B_NN)
    return inv


def _dn_out(o, z, norm_g):
    o = o * lax.rsqrt(jnp.mean(o * o, -1, keepdims=True) + NORM_EPS) * norm_g
    return o * _silu(z)


def _dn_scan_kernel(q_ref, k_ref, v_ref, z_ref, bg_ref, bgt_ref, ng_ref, y_ref, s_out_ref, s_ref, *, chunks):
    t = pl.program_id(1)

    @pl.when(t == 0)
    def _():
        s_ref[...] = jnp.zeros_like(s_ref)

    c, d = DN_CHUNK, HEAD_DIM
    units = [(ci, h) for ci in range(chunks) for h in range(HEADS)]
    rows = lambda ci: slice(ci * c, (ci + 1) * c)
    cols = lambda h: slice(h * d, (h + 1) * d)
    gather = lambda ref: jnp.stack([ref[rows(ci), cols(h)] for ci, h in units])
    q, k, v = gather(q_ref), gather(k_ref), gather(v_ref)
    beta = jnp.stack([bg_ref[rows(ci), h:h + 1] for ci, h in units])
    g_col = jnp.stack([bg_ref[rows(ci), HEADS + h:HEADS + h + 1] for ci, h in units])
    g_row = jnp.stack([bgt_ref[ci, HEADS + h:HEADS + h + 1, :] for ci, h in units])

    r = lax.broadcasted_iota(jnp.int32, (c, c), 0)
    s_ = lax.broadcasted_iota(jnp.int32, (c, c), 1)
    incl, strict, eye = (r >= s_)[None], (r > s_)[None], (r == s_).astype(F32)[None]
    decay = jnp.where(incl, jnp.exp(jnp.where(incl, g_col - g_row, 0.0)), 0.0)
    qk_kk = _dot3(_split_bf16(jnp.concatenate([q, k], axis=1)), _split_bf16(k), _B_NT)
    qk = qk_kk[:, :c] * decay
    a_mat = jnp.where(strict, beta * qk_kk[:, c:] * decay, 0.0)
    inv = _unit_lower_inverse(a_mat, eye)
    e_g = jnp.exp(g_col)
    rhs = jnp.concatenate([beta * v, (beta * e_g) * k], axis=2)
    sol = _dot3(_split_bf16(inv), _split_bf16(rhs), _B_NN)
    u_base, w_mat = sol[:, :, :d], sol[:, :, d:]
    q_dec = (q * e_g).astype(BF16)
    g_last = g_col[:, c - 1:c, :]
    k_dec = k * jnp.exp(g_last - g_col)
    keep = jnp.exp(g_last)

    norm_g = ng_ref[...]
    s = s_ref[...]
    for ci in range(chunks):
        sl = slice(ci * HEADS, (ci + 1) * HEADS)
        s_split = _split_bf16(s)
        u = u_base[sl] - _dot3(_split_bf16(w_mat[sl]), s_split, _B_NN)
        o = (lax.dot_general(q_dec[sl], s_split[0], _B_NN, preferred_element_type=F32)
             + lax.dot_general(qk[sl].astype(BF16), u.astype(BF16), _B_NN, preferred_element_type=F32))
        s = s * keep[sl] + _dot3(_split_bf16(k_dec[sl]), _split_bf16(u), _B_TN)
        for h in range(HEADS):
            y_ref[rows(ci), cols(h)] = _dn_out(o[h], z_ref[rows(ci), cols(h)], norm_g)
    s_ref[...] = s

    @pl.when(t == pl.num_programs(1) - 1)
    def _():
        s_out_ref[0] = s_ref[...]


def _dn_scan(q, k, v, proj, bg, bgt, norm_g, *, batch, seq, chunks):
    n = q.shape[0]
    rows = chunks * DN_CHUNK
    steps = seq // rows
    spec = pl.BlockSpec((rows, G_W), lambda b, t: (b * steps + t, 0))
    ng = norm_g.reshape(1, HEAD_DIM)
    return pl.pallas_call(
        functools.partial(_dn_scan_kernel, chunks=chunks),
        out_shape=(jax.ShapeDtypeStruct((n, G_W), F32),
                   jax.ShapeDtypeStruct((batch, HEADS, HEAD_DIM, HEAD_DIM), F32)),
        grid=(batch, steps),
        in_specs=[spec, spec, spec,
                  pl.BlockSpec((rows, G_W), lambda b, t: (b * steps + t, COL_DN_Z)),
                  pl.BlockSpec((rows, BG_LANES), lambda b, t: (b * steps + t, 0)),
                  pl.BlockSpec((chunks, 2 * HEADS, DN_CHUNK), lambda b, t: (b * steps + t, 0, 0)),
                  pl.BlockSpec((1, HEAD_DIM), lambda b, t: (0, 0))],
        out_specs=(spec, pl.BlockSpec((1, HEADS, HEAD_DIM, HEAD_DIM), lambda b, t: (b, 0, 0, 0))),
        scratch_shapes=[pltpu.VMEM((HEADS, HEAD_DIM, HEAD_DIM), F32)],
        compiler_params=_params("parallel", "arbitrary"),
        name="dn_scan",
    )(q, k, v, proj, bg, bgt, ng)


def _sample_mix_kernel(pool_ref, u_ref, v_ref, qkv_ref, z_ref, bgr_ref, sp_ref, sc_ref, sd_ref,
                       pw_ref, ps_ref, sw0_ref, sb0_ref, cw_ref, na_ref, dt_ref, ng_ref, *rest, bt, pos0):
    yp_ref, ys_ref, yd_ref, np_ref, nc_ref, nd_ref = rest[-6:]
    new = pool_ref[...]
    for gi, w in enumerate(POOL_WINDOWS):
        cols = slice(gi * HEAD_DIM, (gi + 1) * HEAD_DIM)
        s = new[:, cols]
        for j in range(1, w):
            s = s + sp_ref[POOL_BUF - j, :, cols]
        d = s / float(min(pos0 + 1, w)) - new[:, cols]
        yp_ref[:, cols] = _hdot(d, pw_ref[gi]) * ps_ref[:, cols]
    np_ref[0:POOL_BUF - 1] = sp_ref[1:POOL_BUF]
    np_ref[POOL_BUF - 1] = new

    ys_ref[...] = u_ref[...] * (sw0_ref[...] * v_ref[...] + sb0_ref[...])

    raw = qkv_ref[...]
    conv = raw * cw_ref[CONV_W - 1:CONV_W, :]
    for j in range(CONV_W - 1):
        conv = conv + sc_ref[j] * cw_ref[j:j + 1, :]
    nc_ref[0:CONV_W - 2] = sc_ref[1:CONV_W - 1]
    nc_ref[CONV_W - 2] = raw
    act = _silu(conv)
    bg = _beta_and_log_decay(bgr_ref[...], na_ref[...], dt_ref[...])
    norm_g = ng_ref[...]
    qs, ks = [], []
    for h in range(HEADS):
        qs.append(_l2_normalize(act[:, h * HEAD_DIM:(h + 1) * HEAD_DIM]) * (HEAD_DIM ** -0.5))
        ks.append(_l2_normalize(act[:, G_W + h * HEAD_DIM:G_W + (h + 1) * HEAD_DIM]))
    pad = jnp.zeros((HEAD_DIM - bt, HEAD_DIM), F32)
    for h in range(HEADS):
        cols = slice(h * HEAD_DIM, (h + 1) * HEAD_DIM)
        v_h = act[:, 2 * G_W + h * HEAD_DIM:2 * G_W + (h + 1) * HEAD_DIM]
        z_h = z_ref[:, cols]
        k_t = jnp.concatenate([ks[h], pad], axis=0).T
        q_t = jnp.concatenate([qs[h], pad], axis=0).T
        for b in range(bt):
            s0 = sd_ref[b, h]
            k_col = jnp.broadcast_to(k_t[:, b:b + 1], (HEAD_DIM, HEAD_DIM))
            q_col = jnp.broadcast_to(q_t[:, b:b + 1], (HEAD_DIM, HEAD_DIM))
            beta = bg[b:b + 1, h:h + 1]
            a = jnp.exp(bg[b:b + 1, HEADS + h:HEADS + h + 1])
            k_s = jnp.sum(k_col * s0, axis=0, keepdims=True)
            u = beta * (v_h[b:b + 1, :] - a * k_s)
            s1 = a * s0 + k_col * u
            nd_ref[b, h] = s1
            o = jnp.sum(q_col * s1, axis=0, keepdims=True)
            yd_ref[b:b + 1, cols] = _dn_out(o, z_h[b:b + 1, :], norm_g)


def _sample_mix(proj, bg_raw, state_pool, state_conv, state_delta, new_prev, pool_w, pool_scale, sgu_w0, sgu_b0,
                conv_w, neg_a_row, dt_row, norm_g, *, layer, bt, pos0):
    bs = proj.shape[0]
    qkv_w = 3 * G_W
    col = lambda c: pl.BlockSpec((bt, G_W), lambda i: (i, c))
    whole = lambda a: pl.BlockSpec(a.shape, lambda i: (0,) * a.ndim)
    row = pl.BlockSpec((bt, G_W), lambda i: (i, 0))
    ps = pool_scale.reshape(1, G_W)
    ng = norm_g.reshape(1, HEAD_DIM)
    consts = (pool_w, ps, sgu_w0, sgu_b0, conv_w, neg_a_row, dt_row, ng)
    out_sds = lambda a: jax.ShapeDtypeStruct(a.shape, F32)
    pool_spec = pl.BlockSpec((None, POOL_BUF, bt, G_W), lambda i: (layer, 0, i, 0))
    conv_spec = pl.BlockSpec((None, CONV_W - 1, bt, qkv_w), lambda i: (layer, 0, i, 0))
    delta_spec = pl.BlockSpec((None, bt, HEADS, HEAD_DIM, HEAD_DIM), lambda i: (layer, i, 0, 0, 0))
    y = jax.ShapeDtypeStruct((bs, G_W), F32)
    prev = () if new_prev is None else tuple(new_prev)
    n_in = 9 + len(consts)
    return pl.pallas_call(
        functools.partial(_sample_mix_kernel, bt=bt, pos0=pos0),
        out_shape=(y, y, y, out_sds(state_pool), out_sds(state_conv), out_sds(state_delta)),
        grid=(bs // bt,),
        in_specs=[col(COL_POOL), col(COL_SGU_U), col(COL_SGU_V),
                  pl.BlockSpec((bt, qkv_w), lambda i: (i, COL_DN_QKV * G_W // qkv_w)),
                  col(COL_DN_Z),
                  pl.BlockSpec((bt, BG_LANES), lambda i: (i, 0)),
                  pool_spec, conv_spec, delta_spec] + [whole(a) for a in consts]
                 + [pl.BlockSpec(memory_space=pl.ANY)] * len(prev),
        out_specs=(row, row, row, pool_spec, conv_spec, delta_spec),
        input_output_aliases={n_in + k: 3 + k for k in range(len(prev))},
        compiler_params=_params("parallel"),
        name="sample_mix",
    )(proj, proj, proj, proj, proj, bg_raw, state_pool, state_conv, state_delta, *consts, *prev)


def _lane_row(values, offset):
    return jnp.zeros((1, BG_LANES), F32).at[0, offset:offset + values.shape[0]].set(values)


def kernel(x_prompt, x_sample, cache_k, cache_v, page_table, state_pool, state_conv, state_delta,
           ln_g, ln_b, w_ffn1_in, w_ffn1_out, w_ffn2_in, w_ffn2_out, w_in, w_out,
           pool_w, pool_scale, sgu_w, sgu_b, sb_bias, dn_conv_w, dn_a_log, dn_dt_bias, dn_norm_g):
    depth = ln_g.shape[0]
    alpha = (2.0 * depth) ** 0.25
    bp, seq, d_model = x_prompt.shape
    bs, dec_seq, _ = x_sample.shape
    assert dec_seq == 1
    n_phys, page = cache_k.shape[1], cache_k.shape[2]
    n_pages = page_table.shape[1]
    pos0 = n_pages * page
    ck = cache_k.reshape(depth * n_phys * page * HEADS, HEAD_DIM)
    cv = cache_v.reshape(depth * n_phys * page * HEADS, HEAD_DIM)
    hp = x_prompt.reshape(bp * seq, d_model)
    hs = x_sample.reshape(bs, d_model)
    pool_tm = jnp.swapaxes(state_pool, 1, 2)
    conv_tm = jnp.swapaxes(state_conv, 1, 2)
    tm_p = 512
    tm_s = bs
    tail_w = w_in.shape[2] - N_MAIN_COLS
    w_in_tail = jnp.pad(w_in[:, :, N_MAIN_COLS:].astype(BF16), ((0, 0), (0, 0), (0, BG_LANES - tail_w)))
    w_in_t = jnp.swapaxes(w_in, 1, 2)
    w_out_bf = w_out.astype(BF16)

    outs = {name: [] for name in ("poolp", "convp", "sp", "ks", "vs", "sgus")}
    kv_rows = None
    new_states = None
    for l in range(depth):
        neg_a_row = _lane_row(-jnp.exp(dn_a_log[l]), HEADS)
        dt_row = _lane_row(dn_dt_bias[l], HEADS)
        ffn = functools.partial(_ffn_ln, ln_g=ln_g, ln_b=ln_b, layer=l, alpha=alpha, tf=512)
        out_proj = functools.partial(_out_proj_ln, w_out=w_out_bf, ln_g=ln_g, ln_b=ln_b, layer=l, alpha=alpha)

        g1, *w1 = ffn(hs, (w_ffn1_in, w_ffn1_out), which=0, tm=tm_s)
        sproj, sbg_raw, sk_rows, sv_rows, wi_main = _in_proj_cast(g1, w_in_t, w_in_tail, layer=l)
        sgu_w0 = jnp.repeat(sgu_w[l, :, 0, 0], HEAD_DIM).reshape(1, G_W)
        sgu_b0 = jnp.repeat(sgu_b[l, :, 0], HEAD_DIM).reshape(1, G_W)
        sy_pool, sy_sgu, sy_dn, *new_states = _sample_mix(
            sproj, sbg_raw, pool_tm, conv_tm, state_delta, new_states, pool_w[l], pool_scale[l],
            sgu_w0, sgu_b0, dn_conv_w[l], neg_a_row, dt_row, dn_norm_g[l], layer=l, bt=8, pos0=pos0)
        bias_row = jnp.tile(sb_bias[l], page).reshape(1, page * HEADS)
        phys = (page_table + l * n_phys).reshape(-1)
        sy_sb = _sb_decode(sproj.reshape(bs, 1, N_MAIN_COLS), bias_row, ck, cv, phys,
                           n_pages=n_pages, width=page * HEADS)
        g2 = out_proj(g1, (sy_pool, sy_sgu, sy_sb, sy_dn), tm=tm_s)
        hs, *w2 = ffn(g2, (w_ffn2_in, w_ffn2_out), which=2, tm=tm_s)
        outs["ks"].append(sk_rows.reshape(bs, 1, HEADS, HEAD_DIM))
        outs["vs"].append(sv_rows.reshape(bs, 1, HEADS, HEAD_DIM))
        outs["sgus"].append(sproj[:, COL_SGU_V * G_W:(COL_SGU_V + 1) * G_W].reshape(bs, 1, G_W))

        h1, = ffn(hp, tuple(w1), which=0, tm=tm_p)
        proj, bg_raw, *kv_rows = _in_proj(h1, wi_main, w_in_tail, kv_rows, layer=l, depth=depth, tm=256)
        y_pool, y_sgu = _pool_sgu(proj, pool_w[l], pool_scale[l], sgu_w[l], sgu_b[l], seq=seq)
        y_sb = _sb_attn(proj, sb_bias[l], batch=bp, seq=seq, blk=256)
        dq, dk, dv, bg, bgt = _dn_prep(proj, bg_raw, dn_conv_w[l], neg_a_row, dt_row, seq=seq, rows=256)
        y_dn, s_end = _dn_scan(dq, dk, dv, proj, bg, bgt, dn_norm_g[l], batch=bp, seq=seq, chunks=4)
        h2 = out_proj(h1, (y_pool, y_sgu, y_sb, y_dn), tm=256)
        hp, = ffn(h2, tuple(w2), which=2, tm=tm_p)
        proj3 = proj.reshape(bp, seq, N_MAIN_COLS)
        outs["poolp"].append(proj3[:, seq - POOL_BUF:, :G_W])
        outs["convp"].append(proj3[:, seq - (CONV_W - 1):, COL_DN_QKV * G_W:(COL_DN_QKV + 3) * G_W])
        outs["sp"].append(s_end)

    st = lambda name: jnp.stack(outs[name])
    new_pool, new_conv, new_delta = new_states
    kp, vp = (a.reshape(depth, bp, seq, HEADS, HEAD_DIM) for a in kv_rows)
    return (hp.reshape(bp, seq, d_model), hs.reshape(bs, 1, d_model),
            kp, vp, st("poolp"), st("convp"), st("sp"),
            st("ks"), st("vs"), jnp.swapaxes(new_pool, 1, 2), jnp.swapaxes(new_conv, 1, 2),
            new_delta, st("sgus"))
```

```python
import functools

import jax
import jax.numpy as jnp
import numpy as np
from jax import lax
from jax.experimental import pallas as pl
from jax.experimental.pallas import tpu as pltpu

F32 = jnp.float32
BF16 = jnp.bfloat16
HIGHEST = lax.Precision.HIGHEST

HEADS = 4
HEAD_DIM = 128
G_W = HEADS * HEAD_DIM
POOL_WINDOWS = (2, 4, 8, 16)
POOL_BUF = max(POOL_WINDOWS) - 1
POOL_HALO = 16
SGU_CHUNK = 128
DN_CHUNK = 64
CONV_W = 4
CONV_HALO = 8
LN_EPS = 1e-5
NORM_EPS = 1e-6
VMEM_LIMIT_BYTES = 48 * 1024 * 1024
FFN_VMEM_LIMIT_BYTES = 58 * 1024 * 1024

COL_POOL, COL_SGU_U, COL_SGU_V, COL_SB_Q, COL_SB_K, COL_SB_V, COL_DN_QKV, COL_DN_Z = 0, 1, 2, 3, 4, 5, 6, 9
N_MAIN_COLS = 10 * G_W
BG_LANES = 128


def _params(*sem):
    return pltpu.CompilerParams(dimension_semantics=sem, vmem_limit_bytes=VMEM_LIMIT_BYTES)


def _hdot(a, b):
    return jnp.dot(a, b, precision=HIGHEST, preferred_element_type=F32)


def _bdot(a, b):
    return jnp.dot(a.astype(BF16), b.astype(BF16), preferred_element_type=F32)


def _bdot_nt(a, b):
    return lax.dot_general(a.astype(BF16), b.astype(BF16), (((1,), (1,)), ((), ())),
                           preferred_element_type=F32)


def _silu(x):
    return x * jax.nn.sigmoid(x)


def _softplus(x):
    return jnp.maximum(x, 0.0) + jnp.log1p(jnp.exp(-jnp.abs(x)))


def _layer_norm(y, g, b):
    mu = jnp.mean(y, -1, keepdims=True)
    d = y - mu
    var = jnp.mean(d * d, -1, keepdims=True)
    return d * lax.rsqrt(var + LN_EPS) * g + b


def _split_bf16(x):
    hi = x.astype(BF16)
    lo = (x - hi.astype(F32)).astype(BF16)
    return hi, lo


_B_NN = (((2,), (1,)), ((0,), (0,)))
_B_NT = (((2,), (2,)), ((0,), (0,)))
_B_TN = (((1,), (1,)), ((0,), (0,)))


def _ffn_ln_kernel(x_ref, wg_ref, wu_ref, wo_ref, g_ref, b_ref, o_ref, *rest, alpha, emit_bf16, tf):
    if emit_bf16:
        wg_out, wu_out, wo_out = rest
    f = pl.program_id(1)

    @pl.when(f == 0)
    def _():
        o_ref[...] = jnp.zeros_like(o_ref)

    wg, wu, wo = wg_ref[...].astype(BF16), wu_ref[...].astype(BF16), wo_ref[...].astype(BF16)
    if emit_bf16:
        wg_out[...], wu_out[...], wo_out[...] = wg, wu, wo
    xb = x_ref[...].astype(BF16)
    gate = jnp.dot(xb, wg, preferred_element_type=F32)
    up = jnp.dot(xb, wu, preferred_element_type=F32)
    act = (_silu(gate) * up).astype(BF16)
    for c in range(o_ref.shape[1] // tf):
        cols = slice(c * tf, (c + 1) * tf)
        o_ref[:, cols] += jnp.dot(act, wo[:, cols], preferred_element_type=F32)

    @pl.when(f == pl.num_programs(1) - 1)
    def _():
        o_ref[...] = _layer_norm(alpha * x_ref[...] + 0.5 * o_ref[...], g_ref[...], b_ref[...])


def _ffn_ln(x, weights, ln_g, ln_b, *, layer, which, alpha, tm, tf):
    n, d = x.shape
    emit_bf16 = len(weights) == 2
    if emit_bf16:
        w_in, w_out = weights
        dff = w_out.shape[1]
        nf = dff // tf
        operands = (w_in, w_in, w_out)
        w_specs = [pl.BlockSpec((None, d, tf), lambda i, f: (layer, 0, f)),
                   pl.BlockSpec((None, d, tf), lambda i, f: (layer, 0, f + nf)),
                   pl.BlockSpec((None, tf, d), lambda i, f: (layer, f, 0))]
    else:
        operands = weights
        dff = weights[2].shape[0]
        nf = dff // tf
        w_specs = [pl.BlockSpec((d, tf), lambda i, f: (0, f)),
                   pl.BlockSpec((d, tf), lambda i, f: (0, f)),
                   pl.BlockSpec((tf, d), lambda i, f: (f, 0))]
    ln_spec = pl.BlockSpec((None, None, 1, d), lambda i, f: (layer, which, 0, 0))
    out_shape = [jax.ShapeDtypeStruct((n, d), F32)]
    out_specs = [pl.BlockSpec((tm, d), lambda i, f: (i, 0))]
    if emit_bf16:
        assert n == tm
        out_shape += [jax.ShapeDtypeStruct((d, dff), BF16), jax.ShapeDtypeStruct((d, dff), BF16),
                      jax.ShapeDtypeStruct((dff, d), BF16)]
        out_specs += [pl.BlockSpec((d, tf), lambda i, f: (0, f)), pl.BlockSpec((d, tf), lambda i, f: (0, f)),
                      pl.BlockSpec((tf, d), lambda i, f: (f, 0))]
    return pl.pallas_call(
        functools.partial(_ffn_ln_kernel, alpha=alpha, emit_bf16=emit_bf16, tf=tf),
        out_shape=tuple(out_shape),
        grid=(n // tm, nf),
        in_specs=[pl.BlockSpec((tm, d), lambda i, f: (i, 0), pipeline_mode=pl.Buffered(1))]
                 + w_specs + [ln_spec, ln_spec],
        out_specs=tuple(out_specs),
        compiler_params=pltpu.CompilerParams(dimension_semantics=("parallel", "arbitrary"),
                                             vmem_limit_bytes=FFN_VMEM_LIMIT_BYTES),
        name="ffn_ln_cast" if emit_bf16 else "ffn_ln",
    )(x, *operands, ln_g.reshape(ln_g.shape[0], ln_g.shape[1], 1, d), ln_b.reshape(ln_b.shape[0], ln_b.shape[1], 1, d))


def _store_head_rows(dst_ref, block, tm):
    for h in range(HEADS):
        dst_ref[pl.ds(h, tm, stride=HEADS), :] = block[:, h * HEAD_DIM:(h + 1) * HEAD_DIM]


def _in_proj_kernel(*refs, tm, aliased):
    h_ref, w_ref, wt_ref = refs[:3]
    o_ref, bg_ref, k_ref, v_ref = refs[3 + aliased:]
    hb = h_ref[...].astype(BF16)
    bg_ref[...] = jnp.dot(hb, wt_ref[...], preferred_element_type=F32)
    o = jnp.dot(hb, w_ref[...], preferred_element_type=F32)
    o_ref[...] = o
    _store_head_rows(k_ref, o[:, COL_SB_K * G_W:(COL_SB_K + 1) * G_W], tm)
    _store_head_rows(v_ref, o[:, COL_SB_V * G_W:(COL_SB_V + 1) * G_W], tm)


def _in_proj(h, w_main, w_tail, kv_prev, *, layer, depth, tm):
    n, d = h.shape
    steps = n // tm
    resident = pl.Buffered(1)
    rows_kv = jax.ShapeDtypeStruct((depth * n * HEADS, HEAD_DIM), F32)
    kv_spec = pl.BlockSpec((tm * HEADS, HEAD_DIM), lambda i: (layer * steps + i, 0))
    aliased = 0 if kv_prev is None else 2
    prev = () if kv_prev is None else tuple(kv_prev)
    return pl.pallas_call(
        functools.partial(_in_proj_kernel, tm=tm, aliased=aliased),
        out_shape=(jax.ShapeDtypeStruct((n, N_MAIN_COLS), F32), jax.ShapeDtypeStruct((n, BG_LANES), F32),
                   rows_kv, rows_kv),
        grid=(steps,),
        in_specs=[
            pl.BlockSpec((tm, d), lambda i: (i, 0)),
            pl.BlockSpec((d, N_MAIN_COLS), lambda i: (0, 0), pipeline_mode=resident),
            pl.BlockSpec((None, d, BG_LANES), lambda i: (layer, 0, 0), pipeline_mode=resident),
        ] + [pl.BlockSpec(memory_space=pl.ANY)] * aliased,
        out_specs=(pl.BlockSpec((tm, N_MAIN_COLS), lambda i: (i, 0)),
                   pl.BlockSpec((tm, BG_LANES), lambda i: (i, 0)), kv_spec, kv_spec),
        input_output_aliases={3: 2, 4: 3} if aliased else {},
        compiler_params=_params("parallel"),
        name="in_proj",
    )(h, w_main, w_tail, *prev)


def _in_proj_cast_kernel(h_ref, w_ref, wt_ref, o_ref, bg_ref, k_ref, v_ref, wb_ref, *, tm):
    j = pl.program_id(0)
    hb = h_ref[...].astype(BF16)
    wb = w_ref[...].T.astype(BF16)
    wb_ref[...] = wb
    o = jnp.dot(hb, wb, preferred_element_type=F32)
    o_ref[...] = o

    @pl.when(j == 0)
    def _():
        bg_ref[...] = jnp.dot(hb, wt_ref[...], preferred_element_type=F32)

    @pl.when(j == COL_SB_K)
    def _():
        _store_head_rows(k_ref, o, tm)

    @pl.when(j == COL_SB_V)
    def _():
        _store_head_rows(v_ref, o, tm)


def _in_proj_cast(h, w_in_t, w_tail, *, layer):
    n, d = h.shape
    rows_kv = jax.ShapeDtypeStruct((n * HEADS, HEAD_DIM), F32)
    kv_spec = pl.BlockSpec((n * HEADS, HEAD_DIM), lambda j: (0, 0))
    return pl.pallas_call(
        functools.partial(_in_proj_cast_kernel, tm=n),
        out_shape=(jax.ShapeDtypeStruct((n, N_MAIN_COLS), F32), jax.ShapeDtypeStruct((n, BG_LANES), F32),
                   rows_kv, rows_kv, jax.ShapeDtypeStruct((d, N_MAIN_COLS), BF16)),
        grid=(N_MAIN_COLS // G_W,),
        in_specs=[
            pl.BlockSpec((n, d), lambda j: (0, 0)),
            pl.BlockSpec((None, G_W, d), lambda j: (layer, j, 0)),
            pl.BlockSpec((None, d, BG_LANES), lambda j: (layer, 0, 0)),
        ],
        out_specs=(pl.BlockSpec((n, G_W), lambda j: (0, j)),
                   pl.BlockSpec((n, BG_LANES), lambda j: (0, 0)), kv_spec, kv_spec,
                   pl.BlockSpec((d, G_W), lambda j: (0, j))),
        compiler_params=_params("arbitrary"),
        name="in_proj_cast",
    )(h, w_in_t, w_tail)


def _out_proj_ln_kernel(h_ref, y0_ref, y1_ref, y2_ref, y3_ref, w_ref, g_ref, b_ref, o_ref, *, alpha):
    acc = None
    for m, y_ref in enumerate((y0_ref, y1_ref, y2_ref, y3_ref)):
        part = jnp.dot(y_ref[...].astype(BF16), w_ref[m * G_W:(m + 1) * G_W, :], preferred_element_type=F32)
        acc = part if acc is None else acc + part
    o_ref[...] = _layer_norm(alpha * h_ref[...] + acc, g_ref[...], b_ref[...])


def _out_proj_ln(h, ys, w_out, ln_g, ln_b, *, layer, alpha, tm):
    n, d = h.shape
    y_spec = pl.BlockSpec((tm, G_W), lambda i: (i, 0))
    ln_spec = pl.BlockSpec((None, None, 1, d), lambda i: (layer, 1, 0, 0))
    return pl.pallas_call(
        functools.partial(_out_proj_ln_kernel, alpha=alpha),
        out_shape=jax.ShapeDtypeStruct((n, d), F32),
        grid=(n // tm,),
        in_specs=[pl.BlockSpec((tm, d), lambda i: (i, 0)), y_spec, y_spec, y_spec, y_spec,
                  pl.BlockSpec((None, 4 * G_W, d), lambda i: (layer, 0, 0)), ln_spec, ln_spec],
        out_specs=pl.BlockSpec((tm, d), lambda i: (i, 0)),
        compiler_params=_params("parallel"),
        name="out_proj_ln",
    )(h, *ys, w_out, ln_g.reshape(ln_g.shape[0], ln_g.shape[1], 1, d), ln_b.reshape(ln_b.shape[0], ln_b.shape[1], 1, d))


def _pool_sgu_kernel(cur_ref, halo_ref, u_ref, v_ref, pw_ref, ps_ref, sw_ref, sbc_ref,
                     yp_ref, ys_ref, ext_ref, *, chunks_per_seq):
    c = pl.program_id(0) % chunks_per_seq
    cur = cur_ref[...]
    ext_ref[0:POOL_HALO, :] = jnp.where(c == 0, 0.0, halo_ref[...])
    ext_ref[POOL_HALO:, :] = cur
    rows = SGU_CHUNK
    pos = c * rows + lax.broadcasted_iota(jnp.int32, (rows, 1), 0)
    for gi, w in enumerate(POOL_WINDOWS):
        cols = slice(gi * HEAD_DIM, (gi + 1) * HEAD_DIM)
        s = cur[:, cols]
        for j in range(1, w):
            s = s + ext_ref[POOL_HALO - j:POOL_HALO - j + rows, cols]
        cnt = jnp.minimum(pos + 1, w).astype(F32)
        d = s / cnt - cur[:, cols]
        yp_ref[:, cols] = _bdot(d, pw_ref[gi]) * ps_ref[:, cols]

    r = lax.broadcasted_iota(jnp.int32, (rows, rows), 0)
    s_ = lax.broadcasted_iota(jnp.int32, (rows, rows), 1)
    causal = r >= s_
    for h in range(HEADS):
        cols = slice(h * HEAD_DIM, (h + 1) * HEAD_DIM)
        wm = jnp.where(causal, sw_ref[h], 0.0)
        mixed = _bdot(wm, v_ref[:, cols]) + sbc_ref[:, h:h + 1]
        ys_ref[:, cols] = u_ref[:, cols] * mixed


def _pool_sgu(proj, pool_w, pool_scale, sgu_w, sgu_b, *, seq):
    n = proj.shape[0]
    rows = SGU_CHUNK
    halo_blocks = rows // POOL_HALO
    blk = lambda col: pl.BlockSpec((rows, G_W), lambda g: (g, col))
    whole = lambda a: pl.BlockSpec(a.shape, lambda g: (0,) * a.ndim)
    ps = pool_scale.reshape(1, G_W)
    sbc = sgu_b.T
    return pl.pallas_call(
        functools.partial(_pool_sgu_kernel, chunks_per_seq=seq // rows),
        out_shape=(jax.ShapeDtypeStruct((n, G_W), F32), jax.ShapeDtypeStruct((n, G_W), F32)),
        grid=(n // rows,),
        in_specs=[blk(COL_POOL),
                  pl.BlockSpec((POOL_HALO, G_W), lambda g: (jnp.maximum(g * halo_blocks - 1, 0), COL_POOL)),
                  blk(COL_SGU_U), blk(COL_SGU_V), whole(pool_w), whole(ps), whole(sgu_w), whole(sbc)],
        out_specs=(pl.BlockSpec((rows, G_W), lambda g: (g, 0)), pl.BlockSpec((rows, G_W), lambda g: (g, 0))),
        scratch_shapes=[pltpu.VMEM((POOL_HALO + rows, G_W), F32)],
        compiler_params=_params("parallel"),
        name="pool_sgu",
    )(proj, proj, proj, proj, pool_w, ps, sgu_w, sbc)


def _sb_attn_kernel(bias_ref, q_ref, k_ref, v_ref, o_ref, *, blk, scale):
    qi = pl.program_id(1)
    heads = lambda x: jnp.stack([x[:, h * HEAD_DIM:(h + 1) * HEAD_DIM] for h in range(HEADS)])
    q = heads(q_ref[...]).astype(BF16)
    bias = bias_ref[...]
    row = lax.broadcasted_iota(jnp.int32, (blk, blk), 0)
    col = lax.broadcasted_iota(jnp.int32, (blk, blk), 1)
    after = (row > col).astype(BF16)

    def visit(j, carry, valid):
        o, run = carry
        start = pl.multiple_of(j * blk, blk)
        kb = heads(k_ref[pl.ds(start, blk), :]).astype(BF16)
        vb = heads(v_ref[pl.ds(start, blk), :]).astype(BF16)
        z = lax.dot_general(q, kb, _B_NT, preferred_element_type=F32) * scale + bias
        sp = _softplus(z)
        log_fail = -sp if valid is None else jnp.where(valid, -sp, 0.0)
        hi, lo = _split_bf16(log_fail.reshape(HEADS * blk, blk))
        later = (jnp.dot(hi, after, preferred_element_type=F32)
                 + jnp.dot(lo, after, preferred_element_type=F32)).reshape(HEADS, blk, blk) + run
        att = jnp.exp(z - sp + later)
        if valid is not None:
            att = jnp.where(valid, att, 0.0)
        o = o + lax.dot_general(att.astype(BF16), vb, _B_NN, preferred_element_type=F32)
        run = run + jnp.sum(log_fail, axis=-1, keepdims=True)
        return o, run

    init = (jnp.zeros((HEADS, blk, HEAD_DIM), F32), jnp.zeros((HEADS, blk, 1), F32))
    carry = visit(qi, init, (col < row)[None])
    o, _ = lax.fori_loop(0, qi, lambda step, carry: visit(qi - 1 - step, carry, None), carry)
    for h in range(HEADS):
        o_ref[:, h * HEAD_DIM:(h + 1) * HEAD_DIM] = o[h]


def _sb_attn(proj, sb_bias, *, batch, seq, blk):
    n = proj.shape[0]
    nq = seq // blk
    return pl.pallas_call(
        functools.partial(_sb_attn_kernel, blk=blk, scale=HEAD_DIM ** -0.5),
        out_shape=jax.ShapeDtypeStruct((n, G_W), F32),
        grid=(batch, nq),
        in_specs=[pl.BlockSpec((HEADS, 1, 1), lambda b, i: (0, 0, 0)),
                  pl.BlockSpec((blk, G_W), lambda b, i: (b * nq + i, COL_SB_Q)),
                  pl.BlockSpec((seq, G_W), lambda b, i: (b, COL_SB_K)),
                  pl.BlockSpec((seq, G_W), lambda b, i: (b, COL_SB_V))],
        out_specs=pl.BlockSpec((blk, G_W), lambda b, i: (b * nq + i, 0)),
        compiler_params=_params("parallel", "arbitrary"),
        name="sb_attn",
    )(sb_bias.reshape(HEADS, 1, 1), proj, proj, proj)


def _sb_decode_kernel(pt_ref, q_ref, bias_ref, cum_ref, *refs, n_pages, width, scale):
    del pt_ref
    k_refs = refs[:n_pages]
    v_refs = refs[n_pages:2 * n_pages]
    o_ref = refs[2 * n_pages]
    rows = 8
    q = q_ref[0]
    q_rows = jnp.concatenate([q[:, h * HEAD_DIM:(h + 1) * HEAD_DIM] for h in range(HEADS)]
                             + [jnp.zeros((rows - HEADS, HEAD_DIM), F32)], axis=0).astype(BF16)
    rid = lax.broadcasted_iota(jnp.int32, (rows, width), 0)
    cid = lax.broadcasted_iota(jnp.int32, (rows, width), 1)
    own = (cid & (HEADS - 1)) == rid

    z_rows = []
    for p in range(n_pages):
        zf = _bdot_nt(q_rows, k_refs[p][...])
        z_rows.append(jnp.sum(jnp.where(own, zf, 0.0), axis=0, keepdims=True))
    z = jnp.concatenate(z_rows, axis=0) * scale + bias_ref[...]
    sp = _softplus(z)
    hi, lo = _split_bf16(-sp)
    cum = (jnp.dot(hi, cum_ref[...], preferred_element_type=F32)
           + jnp.dot(lo, cum_ref[...], preferred_element_type=F32))
    in_page, page_tot = cum[:, :width], cum[:, width:]
    pr = lax.broadcasted_iota(jnp.int32, (n_pages, n_pages), 0)
    pc = lax.broadcasted_iota(jnp.int32, (n_pages, n_pages), 1)
    later_pages = _hdot((pc > pr).astype(F32), page_tot)
    att = jnp.exp(z - sp + in_page + later_pages)

    acc = jnp.zeros((rows, HEAD_DIM), F32)
    for p in range(n_pages):
        a_p = jnp.where(own, jnp.broadcast_to(att[p:p + 1, :], (rows, width)), 0.0)
        acc = acc + _bdot(a_p, v_refs[p][...])
    for h in range(HEADS):
        o_ref[0, :, h * HEAD_DIM:(h + 1) * HEAD_DIM] = acc[h:h + 1, :]


def _same_head_sums(width):
    src = np.arange(width)[:, None]
    dst = np.arange(width)[None, :]
    same = (src - dst) % HEADS == 0
    return jnp.asarray(np.concatenate([same & (src > dst), same], axis=1), BF16)


def _sb_decode(proj3, bias_row, cache_k, cache_v, phys_pages, *, n_pages, width):
    bs = proj3.shape[0]

    def page_spec(p):
        return pl.BlockSpec((width, HEAD_DIM), lambda b, pt: (pt[b * n_pages + p], 0))

    grid_spec = pltpu.PrefetchScalarGridSpec(
        num_scalar_prefetch=1,
        grid=(bs,),
        in_specs=[pl.BlockSpec((1, 1, G_W), lambda b, pt: (b, 0, COL_SB_Q)),
                  pl.BlockSpec((1, width), lambda b, pt: (0, 0)),
                  pl.BlockSpec((width, 2 * width), lambda b, pt: (0, 0))]
                 + [page_spec(p) for p in range(n_pages)] + [page_spec(p) for p in range(n_pages)],
        out_specs=pl.BlockSpec((1, 1, G_W), lambda b, pt: (b, 0, 0)),
    )
    out = pl.pallas_call(
        functools.partial(_sb_decode_kernel, n_pages=n_pages, width=width, scale=HEAD_DIM ** -0.5),
        out_shape=jax.ShapeDtypeStruct((bs, 1, G_W), F32),
        grid_spec=grid_spec,
        compiler_params=_params("arbitrary"),
        name="sb_decode",
    )(phys_pages, proj3, bias_row, _same_head_sums(width), *([cache_k] * n_pages), *([cache_v] * n_pages))
    return out.reshape(bs, G_W)


def _beta_and_log_decay(raw, neg_a_row, dt_row):
    lane = lax.broadcasted_iota(jnp.int32, raw.shape, 1)
    beta = jax.nn.sigmoid(raw)
    g = neg_a_row * _softplus(raw + dt_row)
    return jnp.where(lane < HEADS, beta, g)


def _l2_normalize(x):
    return x * lax.rsqrt(jnp.sum(x * x, -1, keepdims=True) + NORM_EPS)


def _dn_prep_kernel(raw_ref, halo_ref, bgr_ref, cw_ref, na_ref, dt_ref,
                    q_ref, k_ref, v_ref, bg_ref, bgt_ref, ext_ref, *, tiles_per_seq, rows):
    first = (pl.program_id(0) % tiles_per_seq) == 0
    ext_ref[0:CONV_HALO, :] = jnp.where(first, 0.0, halo_ref[...])
    ext_ref[CONV_HALO:, :] = raw_ref[...]
    conv = None
    for j in range(CONV_W):
        off = CONV_HALO - (CONV_W - 1) + j
        term = ext_ref[off:off + rows, :] * cw_ref[j:j + 1, :]
        conv = term if conv is None else conv + term
    act = _silu(conv)
    for h in range(HEADS):
        cols = slice(h * HEAD_DIM, (h + 1) * HEAD_DIM)
        q_ref[:, cols] = _l2_normalize(act[:, h * HEAD_DIM:(h + 1) * HEAD_DIM]) * (HEAD_DIM ** -0.5)
        k_ref[:, cols] = _l2_normalize(act[:, G_W + h * HEAD_DIM:G_W + (h + 1) * HEAD_DIM])
    v_ref[...] = act[:, 2 * G_W:]
    bg = _beta_and_log_decay(bgr_ref[...], na_ref[...], dt_ref[...])
    r = lax.broadcasted_iota(jnp.int32, (rows, rows), 0)
    s = lax.broadcasted_iota(jnp.int32, (rows, rows), 1)
    shift = DN_CHUNK.bit_length() - 1
    chunk_prefix = ((r >= s) & (lax.shift_right_logical(r, shift) == lax.shift_right_logical(s, shift)))
    lane = lax.broadcasted_iota(jnp.int32, bg.shape, 1)
    bg = jnp.where(lane < HEADS, bg, _hdot(chunk_prefix.astype(F32), bg))
    bg_ref[...] = bg
    per_tile = BG_LANES // DN_CHUNK
    for t in range(rows // BG_LANES):
        tile_t = bg[t * BG_LANES:(t + 1) * BG_LANES, :].T
        for c in range(per_tile):
            bgt_ref[t * per_tile + c] = tile_t[0:2 * HEADS, c * DN_CHUNK:(c + 1) * DN_CHUNK]


def _dn_prep(proj, bg_raw, conv_w, neg_a_row, dt_row, *, seq, rows):
    n = proj.shape[0]
    halo_blocks = rows // CONV_HALO
    qkv_w = 3 * G_W
    col = COL_DN_QKV * G_W // qkv_w
    whole = lambda a: pl.BlockSpec(a.shape, lambda g: (0,) * a.ndim)
    out = jax.ShapeDtypeStruct((n, G_W), F32)
    o_spec = pl.BlockSpec((rows, G_W), lambda g: (g, 0))
    return pl.pallas_call(
        functools.partial(_dn_prep_kernel, tiles_per_seq=seq // rows, rows=rows),
        out_shape=(out, out, out, jax.ShapeDtypeStruct((n, BG_LANES), F32),
                   jax.ShapeDtypeStruct((n // DN_CHUNK, 2 * HEADS, DN_CHUNK), F32)),
        grid=(n // rows,),
        in_specs=[pl.BlockSpec((rows, qkv_w), lambda g: (g, col)),
                  pl.BlockSpec((CONV_HALO, qkv_w), lambda g: (jnp.maximum(g * halo_blocks - 1, 0), col)),
                  pl.BlockSpec((rows, BG_LANES), lambda g: (g, 0)),
                  whole(conv_w), whole(neg_a_row), whole(dt_row)],
        out_specs=(o_spec, o_spec, o_spec, pl.BlockSpec((rows, BG_LANES), lambda g: (g, 0)),
                   pl.BlockSpec((rows // DN_CHUNK, 2 * HEADS, DN_CHUNK), lambda g: (g, 0, 0))),
        scratch_shapes=[pltpu.VMEM((CONV_HALO + rows, qkv_w), F32)],
        compiler_params=_params("parallel"),
        name="dn_prep",
    )(proj, proj, bg_raw, conv_w, neg_a_row, dt_row)


def _cat_lhs(x, axis):
    hi = x.astype(BF16).astype(F32)
    return jnp.concatenate([hi, x - hi, hi], axis=axis).astype(BF16)


def _cat_rhs(x, axis):
    hi = x.astype(BF16)
    lo = (x - hi.astype(F32)).astype(BF16)
    return jnp.concatenate([hi, hi, lo], axis=axis)


def _mm(lhs_cat, rhs_cat, dims=_B_NN):
    return lax.dot_general(lhs_cat, rhs_cat, dims, preferred_element_type=F32)


def _unit_lower_inverse(a_mat, eye):
    n = a_mat.shape[-1]
    inv = eye - a_mat
    power = _mm(_cat_lhs(a_mat, 2), _cat_rhs(a_mat, 1))
    span = 2
    while span < n:
        p_rhs = _cat_rhs(power, 1)
        inv = inv + _mm(_cat_lhs(inv, 2), p_rhs)
        span *= 2
        if span < n:
            power = _mm(_cat_lhs(power, 2), p_rhs)
    return inv


def _dn_out(o, z, norm_g):
    o = o * lax.rsqrt(jnp.mean(o * o, -1, keepdims=True) + NORM_EPS) * norm_g
    return o * _silu(z)


def _dn_scan_kernel(q_ref, k_ref, v_ref, z_ref, bg_ref, bgt_ref, ng_ref, y_ref, s_out_ref, s_ref, *, chunks):
    t = pl.program_id(1)

    @pl.when(t == 0)
    def _():
        s_ref[...] = jnp.zeros_like(s_ref)

    c, d = DN_CHUNK, HEAD_DIM
    units = [(ci, h) for ci in range(chunks) for h in range(HEADS)]
    rows = lambda ci: slice(ci * c, (ci + 1) * c)
    cols = lambda h: slice(h * d, (h + 1) * d)
    gather = lambda ref: jnp.stack([ref[rows(ci), cols(h)] for ci, h in units])
    q, k, v = gather(q_ref), gather(k_ref), gather(v_ref)
    beta = jnp.stack([bg_ref[rows(ci), h:h + 1] for ci, h in units])
    g_col = jnp.stack([bg_ref[rows(ci), HEADS + h:HEADS + h + 1] for ci, h in units])
    g_row = jnp.stack([bgt_ref[ci, HEADS + h:HEADS + h + 1, :] for ci, h in units])

    r = lax.broadcasted_iota(jnp.int32, (c, c), 0)
    s_ = lax.broadcasted_iota(jnp.int32, (c, c), 1)
    incl, strict, eye = (r >= s_)[None], (r > s_)[None], (r == s_).astype(F32)[None]
    decay = jnp.where(incl, jnp.exp(jnp.where(incl, g_col - g_row, 0.0)), 0.0)
    qk_hi, qk_lo = _split_bf16(jnp.concatenate([q, k], axis=1))
    k_hi, k_lo = _split_bf16(k)
    qk_kk = (_mm(jnp.concatenate([qk_hi, qk_lo], axis=2), jnp.concatenate([k_hi, k_hi], axis=2), _B_NT)
             + _mm(qk_hi, k_lo, _B_NT))
    qk = qk_kk[:, :c] * decay
    a_mat = jnp.where(strict, beta * qk_kk[:, c:] * decay, 0.0)
    inv = _unit_lower_inverse(a_mat, eye)
    e_g = jnp.exp(g_col)
    rhs = jnp.concatenate([beta * v, (beta * e_g) * k], axis=2)
    sol = _mm(_cat_lhs(inv, 2), _cat_rhs(rhs, 1))
    u_base, w_mat = sol[:, :, :d], sol[:, :, d:]
    w_hi, w_lo = _split_bf16(w_mat)
    w_cat = jnp.concatenate([w_hi, w_lo], axis=2)
    q_dec = (q * e_g).astype(BF16)
    g_last = g_col[:, c - 1:c, :]
    k_dec = k * jnp.exp(g_last - g_col)
    keep = jnp.exp(g_last)

    norm_g = ng_ref[...]
    s = s_ref[...]
    for ci in range(chunks):
        sl = slice(ci * HEADS, (ci + 1) * HEADS)
        s_hi, s_lo = _split_bf16(s)
        u = u_base[sl] - (_mm(w_cat[sl], jnp.concatenate([s_hi, s_hi], axis=1)) + _mm(w_hi[sl], s_lo))
        o = _mm(q_dec[sl], s_hi) + _mm(qk[sl].astype(BF16), u.astype(BF16))
        s = s * keep[sl] + _mm(_cat_lhs(k_dec[sl], 1), _cat_rhs(u, 1), _B_TN)
        for h in range(HEADS):
            y_ref[rows(ci), cols(h)] = _dn_out(o[h], z_ref[rows(ci), cols(h)], norm_g)
    s_ref[...] = s

    @pl.when(t == pl.num_programs(1) - 1)
    def _():
        s_out_ref[0] = s_ref[...]


def _dn_scan(q, k, v, proj, bg, bgt, norm_g, *, batch, seq, chunks):
    n = q.shape[0]
    rows = chunks * DN_CHUNK
    steps = seq // rows
    spec = pl.BlockSpec((rows, G_W), lambda b, t: (b * steps + t, 0))
    ng = norm_g.reshape(1, HEAD_DIM)
    return pl.pallas_call(
        functools.partial(_dn_scan_kernel, chunks=chunks),
        out_shape=(jax.ShapeDtypeStruct((n, G_W), F32),
                   jax.ShapeDtypeStruct((batch, HEADS, HEAD_DIM, HEAD_DIM), F32)),
        grid=(batch, steps),
        in_specs=[spec, spec, spec,
                  pl.BlockSpec((rows, G_W), lambda b, t: (b * steps + t, COL_DN_Z)),
                  pl.BlockSpec((rows, BG_LANES), lambda b, t: (b * steps + t, 0)),
                  pl.BlockSpec((chunks, 2 * HEADS, DN_CHUNK), lambda b, t: (b * steps + t, 0, 0)),
                  pl.BlockSpec((1, HEAD_DIM), lambda b, t: (0, 0))],
        out_specs=(spec, pl.BlockSpec((1, HEADS, HEAD_DIM, HEAD_DIM), lambda b, t: (b, 0, 0, 0))),
        scratch_shapes=[pltpu.VMEM((HEADS, HEAD_DIM, HEAD_DIM), F32)],
        compiler_params=_params("parallel", "arbitrary"),
        name="dn_scan",
    )(q, k, v, proj, bg, bgt, ng)


def _sample_mix_kernel(pool_ref, u_ref, v_ref, qkv_ref, z_ref, bgr_ref, sp_ref, sc_ref, sd_ref,
                       pw_ref, ps_ref, sw0_ref, sb0_ref, cw_ref, na_ref, dt_ref, ng_ref, *rest, bt, pos0):
    yp_ref, ys_ref, yd_ref, np_ref, nc_ref, nd_ref = rest[-6:]
    new = pool_ref[...]
    for gi, w in enumerate(POOL_WINDOWS):
        cols = slice(gi * HEAD_DIM, (gi + 1) * HEAD_DIM)
        s = new[:, cols]
        for j in range(1, w):
            s = s + sp_ref[POOL_BUF - j, :, cols]
        d = s / float(min(pos0 + 1, w)) - new[:, cols]
        yp_ref[:, cols] = _hdot(d, pw_ref[gi]) * ps_ref[:, cols]
    np_ref[0:POOL_BUF - 1] = sp_ref[1:POOL_BUF]
    np_ref[POOL_BUF - 1] = new

    ys_ref[...] = u_ref[...] * (sw0_ref[...] * v_ref[...] + sb0_ref[...])

    raw = qkv_ref[...]
    conv = raw * cw_ref[CONV_W - 1:CONV_W, :]
    for j in range(CONV_W - 1):
        conv = conv + sc_ref[j] * cw_ref[j:j + 1, :]
    nc_ref[0:CONV_W - 2] = sc_ref[1:CONV_W - 1]
    nc_ref[CONV_W - 2] = raw
    act = _silu(conv)
    bg = _beta_and_log_decay(bgr_ref[...], na_ref[...], dt_ref[...])
    norm_g = ng_ref[...]
    qs, ks = [], []
    for h in range(HEADS):
        qs.append(_l2_normalize(act[:, h * HEAD_DIM:(h + 1) * HEAD_DIM]) * (HEAD_DIM ** -0.5))
        ks.append(_l2_normalize(act[:, G_W + h * HEAD_DIM:G_W + (h + 1) * HEAD_DIM]))
    pad = jnp.zeros((HEAD_DIM - bt, HEAD_DIM), F32)
    for h in range(HEADS):
        cols = slice(h * HEAD_DIM, (h + 1) * HEAD_DIM)
        v_h = act[:, 2 * G_W + h * HEAD_DIM:2 * G_W + (h + 1) * HEAD_DIM]
        z_h = z_ref[:, cols]
        k_t = jnp.concatenate([ks[h], pad], axis=0).T
        q_t = jnp.concatenate([qs[h], pad], axis=0).T
        for b in range(bt):
            s0 = sd_ref[b, h]
            k_col = jnp.broadcast_to(k_t[:, b:b + 1], (HEAD_DIM, HEAD_DIM))
            q_col = jnp.broadcast_to(q_t[:, b:b + 1], (HEAD_DIM, HEAD_DIM))
            beta = bg[b:b + 1, h:h + 1]
            a = jnp.exp(bg[b:b + 1, HEADS + h:HEADS + h + 1])
            k_s = jnp.sum(k_col * s0, axis=0, keepdims=True)
            u = beta * (v_h[b:b + 1, :] - a * k_s)
            s1 = a * s0 + k_col * u
            nd_ref[b, h] = s1
            o = jnp.sum(q_col * s1, axis=0, keepdims=True)
            yd_ref[b:b + 1, cols] = _dn_out(o, z_h[b:b + 1, :], norm_g)


def _sample_mix(proj, bg_raw, state_pool, state_conv, state_delta, new_prev, pool_w, pool_scale, sgu_w0, sgu_b0,
                conv_w, neg_a_row, dt_row, norm_g, *, layer, bt, pos0):
    bs = proj.shape[0]
    qkv_w = 3 * G_W
    col = lambda c: pl.BlockSpec((bt, G_W), lambda i: (i, c))
    whole = lambda a: pl.BlockSpec(a.shape, lambda i: (0,) * a.ndim)
    row = pl.BlockSpec((bt, G_W), lambda i: (i, 0))
    ps = pool_scale.reshape(1, G_W)
    ng = norm_g.reshape(1, HEAD_DIM)
    consts = (pool_w, ps, sgu_w0, sgu_b0, conv_w, neg_a_row, dt_row, ng)
    out_sds = lambda a: jax.ShapeDtypeStruct(a.shape, F32)
    pool_spec = pl.BlockSpec((None, POOL_BUF, bt, G_W), lambda i: (layer, 0, i, 0))
    conv_spec = pl.BlockSpec((None, CONV_W - 1, bt, qkv_w), lambda i: (layer, 0, i, 0))
    delta_spec = pl.BlockSpec((None, bt, HEADS, HEAD_DIM, HEAD_DIM), lambda i: (layer, i, 0, 0, 0))
    y = jax.ShapeDtypeStruct((bs, G_W), F32)
    prev = () if new_prev is None else tuple(new_prev)
    n_in = 9 + len(consts)
    return pl.pallas_call(
        functools.partial(_sample_mix_kernel, bt=bt, pos0=pos0),
        out_shape=(y, y, y, out_sds(state_pool), out_sds(state_conv), out_sds(state_delta)),
        grid=(bs // bt,),
        in_specs=[col(COL_POOL), col(COL_SGU_U), col(COL_SGU_V),
                  pl.BlockSpec((bt, qkv_w), lambda i: (i, COL_DN_QKV * G_W // qkv_w)),
                  col(COL_DN_Z),
                  pl.BlockSpec((bt, BG_LANES), lambda i: (i, 0)),
                  pool_spec, conv_spec, delta_spec] + [whole(a) for a in consts]
                 + [pl.BlockSpec(memory_space=pl.ANY)] * len(prev),
        out_specs=(row, row, row, pool_spec, conv_spec, delta_spec),
        input_output_aliases={n_in + k: 3 + k for k in range(len(prev))},
        compiler_params=_params("parallel"),
        name="sample_mix",
    )(proj, proj, proj, proj, proj, bg_raw, state_pool, state_conv, state_delta, *consts, *prev)


def _lane_row(values, offset):
    return jnp.zeros((1, BG_LANES), F32).at[0, offset:offset + values.shape[0]].set(values)


def kernel(x_prompt, x_sample, cache_k, cache_v, page_table, state_pool, state_conv, state_delta,
           ln_g, ln_b, w_ffn1_in, w_ffn1_out, w_ffn2_in, w_ffn2_out, w_in, w_out,
           pool_w, pool_scale, sgu_w, sgu_b, sb_bias, dn_conv_w, dn_a_log, dn_dt_bias, dn_norm_g):
    depth = ln_g.shape[0]
    alpha = (2.0 * depth) ** 0.25
    bp, seq, d_model = x_prompt.shape
    bs, dec_seq, _ = x_sample.shape
    assert dec_seq == 1
    n_phys, page = cache_k.shape[1], cache_k.shape[2]
    n_pages = page_table.shape[1]
    pos0 = n_pages * page
    ck = cache_k.reshape(depth * n_phys * page * HEADS, HEAD_DIM)
    cv = cache_v.reshape(depth * n_phys * page * HEADS, HEAD_DIM)
    hp = x_prompt.reshape(bp * seq, d_model)
    hs = x_sample.reshape(bs, d_model)
    pool_tm = jnp.swapaxes(state_pool, 1, 2)
    conv_tm = jnp.swapaxes(state_conv, 1, 2)
    tm_p = 1024
    tm_s = bs
    tail_w = w_in.shape[2] - N_MAIN_COLS
    w_in_tail = jnp.pad(w_in[:, :, N_MAIN_COLS:].astype(BF16), ((0, 0), (0, 0), (0, BG_LANES - tail_w)))
    w_in_t = jnp.swapaxes(w_in, 1, 2)
    w_out_bf = w_out.astype(BF16)

    outs = {name: [] for name in ("poolp", "convp", "sp", "ks", "vs", "sgus")}
    kv_rows = None
    new_states = None
    for l in range(depth):
        neg_a_row = _lane_row(-jnp.exp(dn_a_log[l]), HEADS)
        dt_row = _lane_row(dn_dt_bias[l], HEADS)
        ffn = functools.partial(_ffn_ln, ln_g=ln_g, ln_b=ln_b, layer=l, alpha=alpha, tf=512)
        out_proj = functools.partial(_out_proj_ln, w_out=w_out_bf, ln_g=ln_g, ln_b=ln_b, layer=l, alpha=alpha)

        g1, *w1 = ffn(hs, (w_ffn1_in, w_ffn1_out), which=0, tm=tm_s)
        sproj, sbg_raw, sk_rows, sv_rows, wi_main = _in_proj_cast(g1, w_in_t, w_in_tail, layer=l)
        sgu_w0 = jnp.repeat(sgu_w[l, :, 0, 0], HEAD_DIM).reshape(1, G_W)
        sgu_b0 = jnp.repeat(sgu_b[l, :, 0], HEAD_DIM).reshape(1, G_W)
        sy_pool, sy_sgu, sy_dn, *new_states = _sample_mix(
            sproj, sbg_raw, pool_tm, conv_tm, state_delta, new_states, pool_w[l], pool_scale[l],
            sgu_w0, sgu_b0, dn_conv_w[l], neg_a_row, dt_row, dn_norm_g[l], layer=l, bt=8, pos0=pos0)
        bias_row = jnp.tile(sb_bias[l], page).reshape(1, page * HEADS)
        phys = (page_table + l * n_phys).reshape(-1)
        sy_sb = _sb_decode(sproj.reshape(bs, 1, N_MAIN_COLS), bias_row, ck, cv, phys,
                           n_pages=n_pages, width=page * HEADS)
        g2 = out_proj(g1, (sy_pool, sy_sgu, sy_sb, sy_dn), tm=tm_s)
        hs, *w2 = ffn(g2, (w_ffn2_in, w_ffn2_out), which=2, tm=tm_s)
        outs["ks"].append(sk_rows.reshape(bs, 1, HEADS, HEAD_DIM))
        outs["vs"].append(sv_rows.reshape(bs, 1, HEADS, HEAD_DIM))
        outs["sgus"].append(sproj[:, COL_SGU_V * G_W:(COL_SGU_V + 1) * G_W].reshape(bs, 1, G_W))

        h1, = ffn(hp, tuple(w1), which=0, tm=tm_p)
        proj, bg_raw, *kv_rows = _in_proj(h1, wi_main, w_in_tail, kv_rows, layer=l, depth=depth, tm=256)
        y_pool, y_sgu = _pool_sgu(proj, pool_w[l], pool_scale[l], sgu_w[l], sgu_b[l], seq=seq)
        y_sb = _sb_attn(proj, sb_bias[l], batch=bp, seq=seq, blk=256)
        dq, dk, dv, bg, bgt = _dn_prep(proj, bg_raw, dn_conv_w[l], neg_a_row, dt_row, seq=seq, rows=256)
        y_dn, s_end = _dn_scan(dq, dk, dv, proj, bg, bgt, dn_norm_g[l], batch=bp, seq=seq, chunks=4)
        h2 = out_proj(h1, (y_pool, y_sgu, y_sb, y_dn), tm=256)
        hp, = ffn(h2, tuple(w2), which=2, tm=tm_p)
        proj3 = proj.reshape(bp, seq, N_MAIN_COLS)
        outs["poolp"].append(proj3[:, seq - POOL_BUF:, :G_W])
        outs["convp"].append(proj3[:, seq - (CONV_W - 1):, COL_DN_QKV * G_W:(COL_DN_QKV + 3) * G_W])
        outs["sp"].append(s_end)

    st = lambda name: jnp.stack(outs[name])
    new_pool, new_conv, new_delta = new_states
    kp, vp = (a.reshape(depth, bp, seq, HEADS, HEAD_DIM) for a in kv_rows)
    return (hp.reshape(bp, seq, d_model), hs.reshape(bs, 1, d_model),
            kp, vp, st("poolp"), st("convp"), st("sp"),
            st("ks"), st("vs"), jnp.swapaxes(new_pool, 1, 2), jnp.swapaxes(new_conv, 1, 2),
            new_delta, st("sgus"))
```

```python
import functools

import jax
import jax.numpy as jnp
import numpy as np
from jax import lax
from jax.experimental import pallas as pl
from jax.experimental.pallas import tpu as pltpu

F32 = jnp.float32
BF16 = jnp.bfloat16
HIGHEST = lax.Precision.HIGHEST

HEADS = 4
HEAD_DIM = 128
G_W = HEADS * HEAD_DIM
POOL_WINDOWS = (2, 4, 8, 16)
POOL_BUF = max(POOL_WINDOWS) - 1
POOL_HALO = 16
SGU_CHUNK = 128
DN_CHUNK = 64
CONV_W = 4
CONV_HALO = 8
LN_EPS = 1e-5
NORM_EPS = 1e-6
VMEM_LIMIT_BYTES = 48 * 1024 * 1024

COL_POOL, COL_SGU_U, COL_SGU_V, COL_SB_Q, COL_SB_K, COL_SB_V, COL_DN_QKV, COL_DN_Z = 0, 1, 2, 3, 4, 5, 6, 9
N_MAIN_COLS = 10 * G_W
BG_LANES = 128


def _params(*sem):
    return pltpu.CompilerParams(dimension_semantics=sem, vmem_limit_bytes=VMEM_LIMIT_BYTES)


def _hdot(a, b):
    return jnp.dot(a, b, precision=HIGHEST, preferred_element_type=F32)


def _bdot(a, b):
    return jnp.dot(a.astype(BF16), b.astype(BF16), preferred_element_type=F32)


def _bdot_nt(a, b):
    return lax.dot_general(a.astype(BF16), b.astype(BF16), (((1,), (1,)), ((), ())),
                           preferred_element_type=F32)


def _silu(x):
    return x * jax.nn.sigmoid(x)


def _softplus(x):
    return jnp.maximum(x, 0.0) + jnp.log1p(jnp.exp(-jnp.abs(x)))


def _layer_norm(y, g, b):
    mu = jnp.mean(y, -1, keepdims=True)
    d = y - mu
    var = jnp.mean(d * d, -1, keepdims=True)
    return d * lax.rsqrt(var + LN_EPS) * g + b


def _split_bf16(x):
    hi = x.astype(BF16)
    lo = (x - hi.astype(F32)).astype(BF16)
    return hi, lo


_B_NN = (((2,), (1,)), ((0,), (0,)))
_B_NT = (((2,), (2,)), ((0,), (0,)))
_B_TN = (((1,), (1,)), ((0,), (0,)))


def _ffn_ln_kernel(x_ref, wg_ref, wu_ref, wo_ref, g_ref, b_ref, o_ref, *rest, alpha, emit_bf16):
    if emit_bf16:
        wg_out, wu_out, wo_out, xb_ref, acc_ref = rest
    else:
        xb_ref, acc_ref = rest
    f = pl.program_id(1)

    @pl.when(f == 0)
    def _():
        xb_ref[...] = x_ref[...].astype(BF16)
        acc_ref[...] = jnp.zeros_like(acc_ref)

    wg, wu, wo = wg_ref[...].astype(BF16), wu_ref[...].astype(BF16), wo_ref[...].astype(BF16)
    if emit_bf16:
        wg_out[...], wu_out[...], wo_out[...] = wg, wu, wo
    xb = xb_ref[...]
    gate = jnp.dot(xb, wg, preferred_element_type=F32)
    up = jnp.dot(xb, wu, preferred_element_type=F32)
    act = (_silu(gate) * up).astype(BF16)
    acc_ref[...] += jnp.dot(act, wo, preferred_element_type=F32)

    @pl.when(f == pl.num_programs(1) - 1)
    def _():
        o_ref[...] = _layer_norm(alpha * x_ref[...] + 0.5 * acc_ref[...], g_ref[...], b_ref[...])


def _ffn_ln(x, weights, ln_g, ln_b, *, layer, which, alpha, tm, tf):
    n, d = x.shape
    emit_bf16 = len(weights) == 2
    if emit_bf16:
        w_in, w_out = weights
        dff = w_out.shape[1]
        nf = dff // tf
        operands = (w_in, w_in, w_out)
        w_specs = [pl.BlockSpec((None, d, tf), lambda i, f: (layer, 0, f)),
                   pl.BlockSpec((None, d, tf), lambda i, f: (layer, 0, f + nf)),
                   pl.BlockSpec((None, tf, d), lambda i, f: (layer, f, 0))]
    else:
        operands = weights
        dff = weights[2].shape[0]
        nf = dff // tf
        w_specs = [pl.BlockSpec((d, tf), lambda i, f: (0, f)),
                   pl.BlockSpec((d, tf), lambda i, f: (0, f)),
                   pl.BlockSpec((tf, d), lambda i, f: (f, 0))]
    ln_spec = pl.BlockSpec((None, None, 1, d), lambda i, f: (layer, which, 0, 0))
    out_shape = [jax.ShapeDtypeStruct((n, d), F32)]
    out_specs = [pl.BlockSpec((tm, d), lambda i, f: (i, 0))]
    if emit_bf16:
        assert n == tm
        out_shape += [jax.ShapeDtypeStruct((d, dff), BF16), jax.ShapeDtypeStruct((d, dff), BF16),
                      jax.ShapeDtypeStruct((dff, d), BF16)]
        out_specs += [pl.BlockSpec((d, tf), lambda i, f: (0, f)), pl.BlockSpec((d, tf), lambda i, f: (0, f)),
                      pl.BlockSpec((tf, d), lambda i, f: (f, 0))]
    return pl.pallas_call(
        functools.partial(_ffn_ln_kernel, alpha=alpha, emit_bf16=emit_bf16),
        out_shape=tuple(out_shape),
        grid=(n // tm, nf),
        in_specs=[pl.BlockSpec((tm, d), lambda i, f: (i, 0))] + w_specs + [ln_spec, ln_spec],
        out_specs=tuple(out_specs),
        scratch_shapes=[pltpu.VMEM((tm, d), BF16), pltpu.VMEM((tm, d), F32)],
        compiler_params=_params("parallel", "arbitrary"),
        name="ffn_ln_cast" if emit_bf16 else "ffn_ln",
    )(x, *operands, ln_g.reshape(ln_g.shape[0], ln_g.shape[1], 1, d), ln_b.reshape(ln_b.shape[0], ln_b.shape[1], 1, d))


def _store_head_rows(dst_ref, block, tm):
    for h in range(HEADS):
        dst_ref[pl.ds(h, tm, stride=HEADS), :] = block[:, h * HEAD_DIM:(h + 1) * HEAD_DIM]


def _in_proj_kernel(*refs, tm, aliased):
    h_ref, w_ref, wt_ref = refs[:3]
    o_ref, bg_ref, k_ref, v_ref = refs[3 + aliased:]
    hb = h_ref[...].astype(BF16)
    bg_ref[...] = jnp.dot(hb, wt_ref[...], preferred_element_type=F32)
    o = jnp.dot(hb, w_ref[...], preferred_element_type=F32)
    o_ref[...] = o
    _store_head_rows(k_ref, o[:, COL_SB_K * G_W:(COL_SB_K + 1) * G_W], tm)
    _store_head_rows(v_ref, o[:, COL_SB_V * G_W:(COL_SB_V + 1) * G_W], tm)


def _in_proj(h, w_main, w_tail, kv_prev, *, layer, depth, tm):
    n, d = h.shape
    steps = n // tm
    resident = pl.Buffered(1)
    rows_kv = jax.ShapeDtypeStruct((depth * n * HEADS, HEAD_DIM), F32)
    kv_spec = pl.BlockSpec((tm * HEADS, HEAD_DIM), lambda i: (layer * steps + i, 0))
    aliased = 0 if kv_prev is None else 2
    prev = () if kv_prev is None else tuple(kv_prev)
    return pl.pallas_call(
        functools.partial(_in_proj_kernel, tm=tm, aliased=aliased),
        out_shape=(jax.ShapeDtypeStruct((n, N_MAIN_COLS), F32), jax.ShapeDtypeStruct((n, BG_LANES), F32),
                   rows_kv, rows_kv),
        grid=(steps,),
        in_specs=[
            pl.BlockSpec((tm, d), lambda i: (i, 0)),
            pl.BlockSpec((d, N_MAIN_COLS), lambda i: (0, 0), pipeline_mode=resident),
            pl.BlockSpec((None, d, BG_LANES), lambda i: (layer, 0, 0), pipeline_mode=resident),
        ] + [pl.BlockSpec(memory_space=pl.ANY)] * aliased,
        out_specs=(pl.BlockSpec((tm, N_MAIN_COLS), lambda i: (i, 0)),
                   pl.BlockSpec((tm, BG_LANES), lambda i: (i, 0)), kv_spec, kv_spec),
        input_output_aliases={3: 2, 4: 3} if aliased else {},
        compiler_params=_params("parallel"),
        name="in_proj",
    )(h, w_main, w_tail, *prev)


def _in_proj_cast_kernel(h_ref, w_ref, wt_ref, o_ref, bg_ref, k_ref, v_ref, wb_ref, *, tm):
    j = pl.program_id(0)
    hb = h_ref[...].astype(BF16)
    wb = w_ref[...].T.astype(BF16)
    wb_ref[...] = wb
    o = jnp.dot(hb, wb, preferred_element_type=F32)
    o_ref[...] = o

    @pl.when(j == 0)
    def _():
        bg_ref[...] = jnp.dot(hb, wt_ref[...], preferred_element_type=F32)

    @pl.when(j == COL_SB_K)
    def _():
        _store_head_rows(k_ref, o, tm)

    @pl.when(j == COL_SB_V)
    def _():
        _store_head_rows(v_ref, o, tm)


def _in_proj_cast(h, w_in_t, w_tail, *, layer):
    n, d = h.shape
    rows_kv = jax.ShapeDtypeStruct((n * HEADS, HEAD_DIM), F32)
    kv_spec = pl.BlockSpec((n * HEADS, HEAD_DIM), lambda j: (0, 0))
    return pl.pallas_call(
        functools.partial(_in_proj_cast_kernel, tm=n),
        out_shape=(jax.ShapeDtypeStruct((n, N_MAIN_COLS), F32), jax.ShapeDtypeStruct((n, BG_LANES), F32),
                   rows_kv, rows_kv, jax.ShapeDtypeStruct((d, N_MAIN_COLS), BF16)),
        grid=(N_MAIN_COLS // G_W,),
        in_specs=[
            pl.BlockSpec((n, d), lambda j: (0, 0)),
            pl.BlockSpec((None, G_W, d), lambda j: (layer, j, 0)),
            pl.BlockSpec((None, d, BG_LANES), lambda j: (layer, 0, 0)),
        ],
        out_specs=(pl.BlockSpec((n, G_W), lambda j: (0, j)),
                   pl.BlockSpec((n, BG_LANES), lambda j: (0, 0)), kv_spec, kv_spec,
                   pl.BlockSpec((d, G_W), lambda j: (0, j))),
        compiler_params=_params("arbitrary"),
        name="in_proj_cast",
    )(h, w_in_t, w_tail)


def _out_proj_ln_kernel(h_ref, y0_ref, y1_ref, y2_ref, y3_ref, w_ref, g_ref, b_ref, o_ref, *, alpha):
    mixed = jnp.concatenate([y_ref[...].astype(BF16) for y_ref in (y0_ref, y1_ref, y2_ref, y3_ref)], axis=1)
    acc = jnp.dot(mixed, w_ref[...], preferred_element_type=F32)
    o_ref[...] = _layer_norm(alpha * h_ref[...] + acc, g_ref[...], b_ref[...])


def _out_proj_ln(h, ys, w_out, ln_g, ln_b, *, layer, alpha, tm):
    n, d = h.shape
    y_spec = pl.BlockSpec((tm, G_W), lambda i: (i, 0))
    ln_spec = pl.BlockSpec((None, None, 1, d), lambda i: (layer, 1, 0, 0))
    return pl.pallas_call(
        functools.partial(_out_proj_ln_kernel, alpha=alpha),
        out_shape=jax.ShapeDtypeStruct((n, d), F32),
        grid=(n // tm,),
        in_specs=[pl.BlockSpec((tm, d), lambda i: (i, 0)), y_spec, y_spec, y_spec, y_spec,
                  pl.BlockSpec((None, 4 * G_W, d), lambda i: (layer, 0, 0)), ln_spec, ln_spec],
        out_specs=pl.BlockSpec((tm, d), lambda i: (i, 0)),
        compiler_params=_params("parallel"),
        name="out_proj_ln",
    )(h, *ys, w_out, ln_g.reshape(ln_g.shape[0], ln_g.shape[1], 1, d), ln_b.reshape(ln_b.shape[0], ln_b.shape[1], 1, d))


def _pool_sgu_kernel(cur_ref, halo_ref, u_ref, v_ref, pw_ref, ps_ref, sw_ref, sbc_ref,
                     yp_ref, ys_ref, ext_ref, *, chunks_per_seq):
    c = pl.program_id(0) % chunks_per_seq
    cur = cur_ref[...]
    ext_ref[0:POOL_HALO, :] = jnp.where(c == 0, 0.0, halo_ref[...])
    ext_ref[POOL_HALO:, :] = cur
    rows = SGU_CHUNK
    pos = c * rows + lax.broadcasted_iota(jnp.int32, (rows, 1), 0)
    for gi, w in enumerate(POOL_WINDOWS):
        cols = slice(gi * HEAD_DIM, (gi + 1) * HEAD_DIM)
        s = cur[:, cols]
        for j in range(1, w):
            s = s + ext_ref[POOL_HALO - j:POOL_HALO - j + rows, cols]
        cnt = jnp.minimum(pos + 1, w).astype(F32)
        d = s / cnt - cur[:, cols]
        yp_ref[:, cols] = _bdot(d, pw_ref[gi]) * ps_ref[:, cols]

    r = lax.broadcasted_iota(jnp.int32, (rows, rows), 0)
    s_ = lax.broadcasted_iota(jnp.int32, (rows, rows), 1)
    causal = r >= s_
    for h in range(HEADS):
        cols = slice(h * HEAD_DIM, (h + 1) * HEAD_DIM)
        wm = jnp.where(causal, sw_ref[h], 0.0)
        mixed = _bdot(wm, v_ref[:, cols]) + sbc_ref[:, h:h + 1]
        ys_ref[:, cols] = u_ref[:, cols] * mixed


def _pool_sgu(proj, pool_w, pool_scale, sgu_w, sgu_b, *, seq):
    n = proj.shape[0]
    rows = SGU_CHUNK
    halo_blocks = rows // POOL_HALO
    blk = lambda col: pl.BlockSpec((rows, G_W), lambda g: (g, col))
    whole = lambda a: pl.BlockSpec(a.shape, lambda g: (0,) * a.ndim)
    ps = pool_scale.reshape(1, G_W)
    sbc = sgu_b.T
    return pl.pallas_call(
        functools.partial(_pool_sgu_kernel, chunks_per_seq=seq // rows),
        out_shape=(jax.ShapeDtypeStruct((n, G_W), F32), jax.ShapeDtypeStruct((n, G_W), F32)),
        grid=(n // rows,),
        in_specs=[blk(COL_POOL),
                  pl.BlockSpec((POOL_HALO, G_W), lambda g: (jnp.maximum(g * halo_blocks - 1, 0), COL_POOL)),
                  blk(COL_SGU_U), blk(COL_SGU_V), whole(pool_w), whole(ps), whole(sgu_w), whole(sbc)],
        out_specs=(pl.BlockSpec((rows, G_W), lambda g: (g, 0)), pl.BlockSpec((rows, G_W), lambda g: (g, 0))),
        scratch_shapes=[pltpu.VMEM((POOL_HALO + rows, G_W), F32)],
        compiler_params=_params("parallel"),
        name="pool_sgu",
    )(proj, proj, proj, proj, pool_w, ps, sgu_w, sbc)


def _sb_attn_kernel(bias_ref, q_ref, k_ref, v_ref, o_ref, *, blk, scale):
    qi = pl.program_id(1)
    heads = lambda x: jnp.stack([x[:, h * HEAD_DIM:(h + 1) * HEAD_DIM] for h in range(HEADS)])
    q = heads(q_ref[...]).astype(BF16)
    bias = bias_ref[...]
    row = lax.broadcasted_iota(jnp.int32, (blk, blk), 0)
    col = lax.broadcasted_iota(jnp.int32, (blk, blk), 1)
    after = (row > col).astype(BF16)

    def visit(j, carry, valid):
        o, run = carry
        start = pl.multiple_of(j * blk, blk)
        kb = heads(k_ref[pl.ds(start, blk), :]).astype(BF16)
        vb = heads(v_ref[pl.ds(start, blk), :]).astype(BF16)
        z = lax.dot_general(q, kb, _B_NT, preferred_element_type=F32) * scale + bias
        sp = jnp.maximum(z, 0.0) + jnp.log(1.0 + jnp.exp(-jnp.abs(z)))
        log_fail = -sp if valid is None else jnp.where(valid, -sp, 0.0)
        hi, lo = _split_bf16(log_fail.reshape(HEADS * blk, blk))
        later = (jnp.dot(hi, after, preferred_element_type=F32)
                 + jnp.dot(lo, after, preferred_element_type=F32)).reshape(HEADS, blk, blk) + run
        att = jnp.exp(z - sp + later)
        if valid is not None:
            att = jnp.where(valid, att, 0.0)
        o = o + lax.dot_general(att.astype(BF16), vb, _B_NN, preferred_element_type=F32)
        run = run + jnp.sum(log_fail, axis=-1, keepdims=True)
        return o, run

    init = (jnp.zeros((HEADS, blk, HEAD_DIM), F32), jnp.zeros((HEADS, blk, 1), F32))
    carry = visit(qi, init, (col < row)[None])
    o, _ = lax.fori_loop(0, qi, lambda step, carry: visit(qi - 1 - step, carry, None), carry)
    for h in range(HEADS):
        o_ref[:, h * HEAD_DIM:(h + 1) * HEAD_DIM] = o[h]


def _sb_attn(proj, sb_bias, *, batch, seq, blk):
    n = proj.shape[0]
    nq = seq // blk
    return pl.pallas_call(
        functools.partial(_sb_attn_kernel, blk=blk, scale=HEAD_DIM ** -0.5),
        out_shape=jax.ShapeDtypeStruct((n, G_W), F32),
        grid=(batch, nq),
        in_specs=[pl.BlockSpec((HEADS, 1, 1), lambda b, i: (0, 0, 0)),
                  pl.BlockSpec((blk, G_W), lambda b, i: (b * nq + i, COL_SB_Q)),
                  pl.BlockSpec((seq, G_W), lambda b, i: (b, COL_SB_K)),
                  pl.BlockSpec((seq, G_W), lambda b, i: (b, COL_SB_V))],
        out_specs=pl.BlockSpec((blk, G_W), lambda b, i: (b * nq + i, 0)),
        compiler_params=_params("parallel", "arbitrary"),
        name="sb_attn",
    )(sb_bias.reshape(HEADS, 1, 1), proj, proj, proj)


def _sb_decode_kernel(pt_ref, q_ref, bias_ref, cum_ref, *refs, n_pages, width, scale):
    del pt_ref
    k_refs = refs[:n_pages]
    v_refs = refs[n_pages:2 * n_pages]
    o_ref = refs[2 * n_pages]
    rows = 8
    q = q_ref[0]
    q_rows = jnp.concatenate([q[:, h * HEAD_DIM:(h + 1) * HEAD_DIM] for h in range(HEADS)]
                             + [jnp.zeros((rows - HEADS, HEAD_DIM), F32)], axis=0).astype(BF16)
    rid = lax.broadcasted_iota(jnp.int32, (rows, width), 0)
    cid = lax.broadcasted_iota(jnp.int32, (rows, width), 1)
    own = (cid & (HEADS - 1)) == rid

    z_rows = []
    for p in range(n_pages):
        zf = _bdot_nt(q_rows, k_refs[p][...])
        z_rows.append(jnp.sum(jnp.where(own, zf, 0.0), axis=0, keepdims=True))
    z = jnp.concatenate(z_rows, axis=0) * scale + bias_ref[...]
    sp = _softplus(z)
    hi, lo = _split_bf16(-sp)
    cum = (jnp.dot(hi, cum_ref[...], preferred_element_type=F32)
           + jnp.dot(lo, cum_ref[...], preferred_element_type=F32))
    in_page, page_tot = cum[:, :width], cum[:, width:]
    pr = lax.broadcasted_iota(jnp.int32, (n_pages, n_pages), 0)
    pc = lax.broadcasted_iota(jnp.int32, (n_pages, n_pages), 1)
    later_pages = _hdot((pc > pr).astype(F32), page_tot)
    att = jnp.exp(z - sp + in_page + later_pages)

    acc = jnp.zeros((rows, HEAD_DIM), F32)
    for p in range(n_pages):
        a_p = jnp.where(own, jnp.broadcast_to(att[p:p + 1, :], (rows, width)), 0.0)
        acc = acc + _bdot(a_p, v_refs[p][...])
    for h in range(HEADS):
        o_ref[0, :, h * HEAD_DIM:(h + 1) * HEAD_DIM] = acc[h:h + 1, :]


def _same_head_sums(width):
    src = np.arange(width)[:, None]
    dst = np.arange(width)[None, :]
    same = (src - dst) % HEADS == 0
    return jnp.asarray(np.concatenate([same & (src > dst), same], axis=1), BF16)


def _sb_decode(proj3, bias_row, cache_k, cache_v, phys_pages, *, n_pages, width):
    bs = proj3.shape[0]

    def page_spec(p):
        return pl.BlockSpec((width, HEAD_DIM), lambda b, pt: (pt[b * n_pages + p], 0))

    grid_spec = pltpu.PrefetchScalarGridSpec(
        num_scalar_prefetch=1,
        grid=(bs,),
        in_specs=[pl.BlockSpec((1, 1, G_W), lambda b, pt: (b, 0, COL_SB_Q)),
                  pl.BlockSpec((1, width), lambda b, pt: (0, 0)),
                  pl.BlockSpec((width, 2 * width), lambda b, pt: (0, 0))]
                 + [page_spec(p) for p in range(n_pages)] + [page_spec(p) for p in range(n_pages)],
        out_specs=pl.BlockSpec((1, 1, G_W), lambda b, pt: (b, 0, 0)),
    )
    out = pl.pallas_call(
        functools.partial(_sb_decode_kernel, n_pages=n_pages, width=width, scale=HEAD_DIM ** -0.5),
        out_shape=jax.ShapeDtypeStruct((bs, 1, G_W), F32),
        grid_spec=grid_spec,
        compiler_params=_params("arbitrary"),
        name="sb_decode",
    )(phys_pages, proj3, bias_row, _same_head_sums(width), *([cache_k] * n_pages), *([cache_v] * n_pages))
    return out.reshape(bs, G_W)


def _beta_and_log_decay(raw, neg_a_row, dt_row):
    lane = lax.broadcasted_iota(jnp.int32, raw.shape, 1)
    beta = jax.nn.sigmoid(raw)
    g = neg_a_row * _softplus(raw + dt_row)
    return jnp.where(lane < HEADS, beta, g)


def _l2_normalize(x):
    return x * lax.rsqrt(jnp.sum(x * x, -1, keepdims=True) + NORM_EPS)


def _dn_prep_kernel(raw_ref, halo_ref, bgr_ref, cw_ref, na_ref, dt_ref,
                    q_ref, k_ref, v_ref, bg_ref, bgt_ref, ext_ref, *, tiles_per_seq, rows):
    first = (pl.program_id(0) % tiles_per_seq) == 0
    ext_ref[0:CONV_HALO, :] = jnp.where(first, 0.0, halo_ref[...])
    ext_ref[CONV_HALO:, :] = raw_ref[...]
    conv = None
    for j in range(CONV_W):
        off = CONV_HALO - (CONV_W - 1) + j
        term = ext_ref[off:off + rows, :] * cw_ref[j:j + 1, :]
        conv = term if conv is None else conv + term
    act = _silu(conv)
    for h in range(HEADS):
        cols = slice(h * HEAD_DIM, (h + 1) * HEAD_DIM)
        q_ref[:, cols] = _l2_normalize(act[:, h * HEAD_DIM:(h + 1) * HEAD_DIM]) * (HEAD_DIM ** -0.5)
        k_ref[:, cols] = _l2_normalize(act[:, G_W + h * HEAD_DIM:G_W + (h + 1) * HEAD_DIM])
    v_ref[...] = act[:, 2 * G_W:]
    bg = _beta_and_log_decay(bgr_ref[...], na_ref[...], dt_ref[...])
    r = lax.broadcasted_iota(jnp.int32, (rows, rows), 0)
    s = lax.broadcasted_iota(jnp.int32, (rows, rows), 1)
    shift = DN_CHUNK.bit_length() - 1
    chunk_prefix = ((r >= s) & (lax.shift_right_logical(r, shift) == lax.shift_right_logical(s, shift)))
    lane = lax.broadcasted_iota(jnp.int32, bg.shape, 1)
    bg = jnp.where(lane < HEADS, bg, _hdot(chunk_prefix.astype(F32), bg))
    bg_ref[...] = bg
    per_tile = BG_LANES // DN_CHUNK
    for t in range(rows // BG_LANES):
        tile_t = bg[t * BG_LANES:(t + 1) * BG_LANES, :].T
        for c in range(per_tile):
            bgt_ref[t * per_tile + c] = tile_t[0:2 * HEADS, c * DN_CHUNK:(c + 1) * DN_CHUNK]


def _dn_prep(proj, bg_raw, conv_w, neg_a_row, dt_row, *, seq, rows):
    n = proj.shape[0]
    halo_blocks = rows // CONV_HALO
    qkv_w = 3 * G_W
    col = COL_DN_QKV * G_W // qkv_w
    whole = lambda a: pl.BlockSpec(a.shape, lambda g: (0,) * a.ndim)
    out = jax.ShapeDtypeStruct((n, G_W), F32)
    o_spec = pl.BlockSpec((rows, G_W), lambda g: (g, 0))
    return pl.pallas_call(
        functools.partial(_dn_prep_kernel, tiles_per_seq=seq // rows, rows=rows),
        out_shape=(out, out, out, jax.ShapeDtypeStruct((n, BG_LANES), F32),
                   jax.ShapeDtypeStruct((n // DN_CHUNK, 2 * HEADS, DN_CHUNK), F32)),
        grid=(n // rows,),
        in_specs=[pl.BlockSpec((rows, qkv_w), lambda g: (g, col)),
                  pl.BlockSpec((CONV_HALO, qkv_w), lambda g: (jnp.maximum(g * halo_blocks - 1, 0), col)),
                  pl.BlockSpec((rows, BG_LANES), lambda g: (g, 0)),
                  whole(conv_w), whole(neg_a_row), whole(dt_row)],
        out_specs=(o_spec, o_spec, o_spec, pl.BlockSpec((rows, BG_LANES), lambda g: (g, 0)),
                   pl.BlockSpec((rows // DN_CHUNK, 2 * HEADS, DN_CHUNK), lambda g: (g, 0, 0))),
        scratch_shapes=[pltpu.VMEM((CONV_HALO + rows, qkv_w), F32)],
        compiler_params=_params("parallel"),
        name="dn_prep",
    )(proj, proj, bg_raw, conv_w, neg_a_row, dt_row)


def _cat_lhs(x, axis):
    hi = x.astype(BF16).astype(F32)
    return jnp.concatenate([hi, x - hi, hi], axis=axis).astype(BF16)


def _cat_rhs(x, axis):
    hi = x.astype(BF16)
    lo = (x - hi.astype(F32)).astype(BF16)
    return jnp.concatenate([hi, hi, lo], axis=axis)


def _mm(lhs_cat, rhs_cat, dims=_B_NN):
    return lax.dot_general(lhs_cat, rhs_cat, dims, preferred_element_type=F32)


def _unit_lower_inverse(a_mat, eye):
    n = a_mat.shape[-1]
    inv = eye - a_mat
    power = _mm(_cat_lhs(a_mat, 2), _cat_rhs(a_mat, 1))
    span = 2
    while span < n:
        p_rhs = _cat_rhs(power, 1)
        inv = inv + _mm(_cat_lhs(inv, 2), p_rhs)
        span *= 2
        if span < n:
            power = _mm(_cat_lhs(power, 2), p_rhs)
    return inv


def _dn_out(o, z, norm_g):
    o = o * lax.rsqrt(jnp.mean(o * o, -1, keepdims=True) + NORM_EPS) * norm_g
    return o * _silu(z)


def _dn_scan_kernel(q_ref, k_ref, v_ref, z_ref, bg_ref, bgt_ref, ng_ref, y_ref, s_out_ref, s_ref, *, chunks):
    t = pl.program_id(1)

    @pl.when(t == 0)
    def _():
        s_ref[...] = jnp.zeros_like(s_ref)

    c, d = DN_CHUNK, HEAD_DIM
    units = [(ci, h) for ci in range(chunks) for h in range(HEADS)]
    rows = lambda ci: slice(ci * c, (ci + 1) * c)
    cols = lambda h: slice(h * d, (h + 1) * d)
    gather = lambda ref: jnp.stack([ref[rows(ci), cols(h)] for ci, h in units])
    q, k, v = gather(q_ref), gather(k_ref), gather(v_ref)
    beta = jnp.stack([bg_ref[rows(ci), h:h + 1] for ci, h in units])
    g_col = jnp.stack([bg_ref[rows(ci), HEADS + h:HEADS + h + 1] for ci, h in units])
    g_row = jnp.stack([bgt_ref[ci, HEADS + h:HEADS + h + 1, :] for ci, h in units])

    r = lax.broadcasted_iota(jnp.int32, (c, c), 0)
    s_ = lax.broadcasted_iota(jnp.int32, (c, c), 1)
    incl, strict, eye = (r >= s_)[None], (r > s_)[None], (r == s_).astype(F32)[None]
    decay = jnp.where(incl, jnp.exp(jnp.where(incl, g_col - g_row, 0.0)), 0.0)
    qk_hi, qk_lo = _split_bf16(jnp.concatenate([q, k], axis=1))
    k_hi, k_lo = _split_bf16(k)
    qk_kk = (_mm(jnp.concatenate([qk_hi, qk_lo], axis=2), jnp.concatenate([k_hi, k_hi], axis=2), _B_NT)
             + _mm(qk_hi, k_lo, _B_NT))
    qk = qk_kk[:, :c] * decay
    a_mat = jnp.where(strict, beta * qk_kk[:, c:] * decay, 0.0)
    inv = _unit_lower_inverse(a_mat, eye)
    e_g = jnp.exp(g_col)
    rhs = jnp.concatenate([beta * v, (beta * e_g) * k], axis=2)
    sol = _mm(_cat_lhs(inv, 2), _cat_rhs(rhs, 1))
    u_base, w_mat = sol[:, :, :d], sol[:, :, d:]
    w_hi, w_lo = _split_bf16(w_mat)
    w_cat = jnp.concatenate([w_hi, w_lo], axis=2)
    q_dec = (q * e_g).astype(BF16)
    g_last = g_col[:, c - 1:c, :]
    k_dec = k * jnp.exp(g_last - g_col)
    keep = jnp.exp(g_last)

    norm_g = ng_ref[...]
    s = s_ref[...]
    for ci in range(chunks):
        sl = slice(ci * HEADS, (ci + 1) * HEADS)
        s_hi, s_lo = _split_bf16(s)
        u = u_base[sl] - (_mm(w_cat[sl], jnp.concatenate([s_hi, s_hi], axis=1)) + _mm(w_hi[sl], s_lo))
        o = _mm(q_dec[sl], s_hi) + _mm(qk[sl].astype(BF16), u.astype(BF16))
        s = s * keep[sl] + _mm(_cat_lhs(k_dec[sl], 1), _cat_rhs(u, 1), _B_TN)
        for h in range(HEADS):
            y_ref[rows(ci), cols(h)] = _dn_out(o[h], z_ref[rows(ci), cols(h)], norm_g)
    s_ref[...] = s

    @pl.when(t == pl.num_programs(1) - 1)
    def _():
        s_out_ref[0] = s_ref[...]


def _dn_scan(q, k, v, proj, bg, bgt, norm_g, *, batch, seq, chunks):
    n = q.shape[0]
    rows = chunks * DN_CHUNK
    steps = seq // rows
    spec = pl.BlockSpec((rows, G_W), lambda b, t: (b * steps + t, 0))
    ng = norm_g.reshape(1, HEAD_DIM)
    return pl.pallas_call(
        functools.partial(_dn_scan_kernel, chunks=chunks),
        out_shape=(jax.ShapeDtypeStruct((n, G_W), F32),
                   jax.ShapeDtypeStruct((batch, HEADS, HEAD_DIM, HEAD_DIM), F32)),
        grid=(batch, steps),
        in_specs=[spec, spec, spec,
                  pl.BlockSpec((rows, G_W), lambda b, t: (b * steps + t, COL_DN_Z)),
                  pl.BlockSpec((rows, BG_LANES), lambda b, t: (b * steps + t, 0)),
                  pl.BlockSpec((chunks, 2 * HEADS, DN_CHUNK), lambda b, t: (b * steps + t, 0, 0)),
                  pl.BlockSpec((1, HEAD_DIM), lambda b, t: (0, 0))],
        out_specs=(spec, pl.BlockSpec((1, HEADS, HEAD_DIM, HEAD_DIM), lambda b, t: (b, 0, 0, 0))),
        scratch_shapes=[pltpu.VMEM((HEADS, HEAD_DIM, HEAD_DIM), F32)],
        compiler_params=_params("parallel", "arbitrary"),
        name="dn_scan",
    )(q, k, v, proj, bg, bgt, ng)


def _sample_mix_kernel(pool_ref, u_ref, v_ref, qkv_ref, z_ref, bgr_ref, sp_ref, sc_ref, sd_ref,
                       pw_ref, ps_ref, sw0_ref, sb0_ref, cw_ref, na_ref, dt_ref, ng_ref, *rest, bt, pos0):
    yp_ref, ys_ref, yd_ref, np_ref, nc_ref, nd_ref = rest[-6:]
    new = pool_ref[...]
    for gi, w in enumerate(POOL_WINDOWS):
        cols = slice(gi * HEAD_DIM, (gi + 1) * HEAD_DIM)
        s = new[:, cols]
        for j in range(1, w):
            s = s + sp_ref[POOL_BUF - j, :, cols]
        d = s / float(min(pos0 + 1, w)) - new[:, cols]
        yp_ref[:, cols] = _hdot(d, pw_ref[gi]) * ps_ref[:, cols]
    np_ref[0:POOL_BUF - 1] = sp_ref[1:POOL_BUF]
    np_ref[POOL_BUF - 1] = new

    ys_ref[...] = u_ref[...] * (sw0_ref[...] * v_ref[...] + sb0_ref[...])

    raw = qkv_ref[...]
    conv = raw * cw_ref[CONV_W - 1:CONV_W, :]
    for j in range(CONV_W - 1):
        conv = conv + sc_ref[j] * cw_ref[j:j + 1, :]
    nc_ref[0:CONV_W - 2] = sc_ref[1:CONV_W - 1]
    nc_ref[CONV_W - 2] = raw
    act = _silu(conv)
    bg = _beta_and_log_decay(bgr_ref[...], na_ref[...], dt_ref[...])
    norm_g = ng_ref[...]
    d = HEAD_DIM
    pad = jnp.zeros((d - bt, d), F32)
    k_t, q_t = [], []
    for h in range(HEADS):
        k_t.append(jnp.concatenate([_l2_normalize(act[:, G_W + h * d:G_W + (h + 1) * d]), pad], axis=0).T)
        q_t.append(jnp.concatenate([_l2_normalize(act[:, h * d:(h + 1) * d]) * (d ** -0.5), pad], axis=0).T)
    pairs = [(b, h) for b in range(bt) for h in range(HEADS)]
    v_h = [act[:, 2 * G_W + h * d:2 * G_W + (h + 1) * d] for h in range(HEADS)]
    z_h = [z_ref[:, h * d:(h + 1) * d] for h in range(HEADS)]
    a = [jnp.exp(bg[b:b + 1, HEADS + h:HEADS + h + 1]) for b, h in pairs]
    k_col = [jnp.broadcast_to(k_t[h][:, b:b + 1], (d, d)) for b, h in pairs]
    k_s = [jnp.sum(k_col[i] * sd_ref[b, h], axis=0, keepdims=True) for i, (b, h) in enumerate(pairs)]
    u = [bg[b:b + 1, h:h + 1] * (v_h[h][b:b + 1, :] - a[i] * k_s[i]) for i, (b, h) in enumerate(pairs)]
    s1 = [a[i] * sd_ref[b, h] + k_col[i] * u[i] for i, (b, h) in enumerate(pairs)]
    for i, (b, h) in enumerate(pairs):
        nd_ref[b, h] = s1[i]
    o = [jnp.sum(jnp.broadcast_to(q_t[h][:, b:b + 1], (d, d)) * s1[i], axis=0, keepdims=True)
         for i, (b, h) in enumerate(pairs)]
    for i, (b, h) in enumerate(pairs):
        yd_ref[b:b + 1, h * d:(h + 1) * d] = _dn_out(o[i], z_h[h][b:b + 1, :], norm_g)


def _sample_mix(proj, bg_raw, state_pool, state_conv, state_delta, new_prev, pool_w, pool_scale, sgu_w0, sgu_b0,
                conv_w, neg_a_row, dt_row, norm_g, *, layer, bt, pos0):
    bs = proj.shape[0]
    qkv_w = 3 * G_W
    col = lambda c: pl.BlockSpec((bt, G_W), lambda i: (i, c))
    whole = lambda a: pl.BlockSpec(a.shape, lambda i: (0,) * a.ndim)
    row = pl.BlockSpec((bt, G_W), lambda i: (i, 0))
    ps = pool_scale.reshape(1, G_W)
    ng = norm_g.reshape(1, HEAD_DIM)
    consts = (pool_w, ps, sgu_w0, sgu_b0, conv_w, neg_a_row, dt_row, ng)
    out_sds = lambda a: jax.ShapeDtypeStruct(a.shape, F32)
    pool_spec = pl.BlockSpec((None, POOL_BUF, bt, G_W), lambda i: (layer, 0, i, 0))
    conv_spec = pl.BlockSpec((None, CONV_W - 1, bt, qkv_w), lambda i: (layer, 0, i, 0))
    delta_spec = pl.BlockSpec((None, bt, HEADS, HEAD_DIM, HEAD_DIM), lambda i: (layer, i, 0, 0, 0))
    y = jax.ShapeDtypeStruct((bs, G_W), F32)
    prev = () if new_prev is None else tuple(new_prev)
    n_in = 9 + len(consts)
    return pl.pallas_call(
        functools.partial(_sample_mix_kernel, bt=bt, pos0=pos0),
        out_shape=(y, y, y, out_sds(state_pool), out_sds(state_conv), out_sds(state_delta)),
        grid=(bs // bt,),
        in_specs=[col(COL_POOL), col(COL_SGU_U), col(COL_SGU_V),
                  pl.BlockSpec((bt, qkv_w), lambda i: (i, COL_DN_QKV * G_W // qkv_w)),
                  col(COL_DN_Z),
                  pl.BlockSpec((bt, BG_LANES), lambda i: (i, 0)),
                  pool_spec, conv_spec, delta_spec] + [whole(a) for a in consts]
                 + [pl.BlockSpec(memory_space=pl.ANY)] * len(prev),
        out_specs=(row, row, row, pool_spec, conv_spec, delta_spec),
        input_output_aliases={n_in + k: 3 + k for k in range(len(prev))},
        compiler_params=_params("parallel"),
        name="sample_mix",
    )(proj, proj, proj, proj, proj, bg_raw, state_pool, state_conv, state_delta, *consts, *prev)


def _lane_row(values, offset):
    return jnp.zeros((1, BG_LANES), F32).at[0, offset:offset + values.shape[0]].set(values)


def kernel(x_prompt, x_sample, cache_k, cache_v, page_table, state_pool, state_conv, state_delta,
           ln_g, ln_b, w_ffn1_in, w_ffn1_out, w_ffn2_in, w_ffn2_out, w_in, w_out,
           pool_w, pool_scale, sgu_w, sgu_b, sb_bias, dn_conv_w, dn_a_log, dn_dt_bias, dn_norm_g):
    depth = ln_g.shape[0]
    alpha = (2.0 * depth) ** 0.25
    bp, seq, d_model = x_prompt.shape
    bs, dec_seq, _ = x_sample.shape
    assert dec_seq == 1
    n_phys, page = cache_k.shape[1], cache_k.shape[2]
    n_pages = page_table.shape[1]
    pos0 = n_pages * page
    ck = cache_k.reshape(depth * n_phys * page * HEADS, HEAD_DIM)
    cv = cache_v.reshape(depth * n_phys * page * HEADS, HEAD_DIM)
    hp = x_prompt.reshape(bp * seq, d_model)
    hs = x_sample.reshape(bs, d_model)
    pool_tm = jnp.swapaxes(state_pool, 1, 2)
    conv_tm = jnp.swapaxes(state_conv, 1, 2)
    tm_p = 512
    tm_s = bs
    tail_w = w_in.shape[2] - N_MAIN_COLS
    w_in_tail = jnp.pad(w_in[:, :, N_MAIN_COLS:].astype(BF16), ((0, 0), (0, 0), (0, BG_LANES - tail_w)))
    w_in_t = jnp.swapaxes(w_in, 1, 2)
    w_out_bf = w_out.astype(BF16)

    outs = {name: [] for name in ("poolp", "convp", "sp", "ks", "vs", "sgus")}
    kv_rows = None
    new_states = None
    for l in range(depth):
        neg_a_row = _lane_row(-jnp.exp(dn_a_log[l]), HEADS)
        dt_row = _lane_row(dn_dt_bias[l], HEADS)
        ffn = functools.partial(_ffn_ln, ln_g=ln_g, ln_b=ln_b, layer=l, alpha=alpha, tf=512)
        out_proj = functools.partial(_out_proj_ln, w_out=w_out_bf, ln_g=ln_g, ln_b=ln_b, layer=l, alpha=alpha)

        g1, *w1 = ffn(hs, (w_ffn1_in, w_ffn1_out), which=0, tm=tm_s)
        sproj, sbg_raw, sk_rows, sv_rows, wi_main = _in_proj_cast(g1, w_in_t, w_in_tail, layer=l)
        sgu_w0 = jnp.repeat(sgu_w[l, :, 0, 0], HEAD_DIM).reshape(1, G_W)
        sgu_b0 = jnp.repeat(sgu_b[l, :, 0], HEAD_DIM).reshape(1, G_W)
        sy_pool, sy_sgu, sy_dn, *new_states = _sample_mix(
            sproj, sbg_raw, pool_tm, conv_tm, state_delta, new_states, pool_w[l], pool_scale[l],
            sgu_w0, sgu_b0, dn_conv_w[l], neg_a_row, dt_row, dn_norm_g[l], layer=l, bt=8, pos0=pos0)
        bias_row = jnp.tile(sb_bias[l], page).reshape(1, page * HEADS)
        phys = (page_table + l * n_phys).reshape(-1)
        sy_sb = _sb_decode(sproj.reshape(bs, 1, N_MAIN_COLS), bias_row, ck, cv, phys,
                           n_pages=n_pages, width=page * HEADS)
        g2 = out_proj(g1, (sy_pool, sy_sgu, sy_sb, sy_dn), tm=tm_s)
        hs, *w2 = ffn(g2, (w_ffn2_in, w_ffn2_out), which=2, tm=tm_s)
        outs["ks"].append(sk_rows.reshape(bs, 1, HEADS, HEAD_DIM))
        outs["vs"].append(sv_rows.reshape(bs, 1, HEADS, HEAD_DIM))
        outs["sgus"].append(sproj[:, COL_SGU_V * G_W:(COL_SGU_V + 1) * G_W].reshape(bs, 1, G_W))

        h1, = ffn(hp, tuple(w1), which=0, tm=tm_p)
        proj, bg_raw, *kv_rows = _in_proj(h1, wi_main, w_in_tail, kv_rows, layer=l, depth=depth, tm=256)
        y_pool, y_sgu = _pool_sgu(proj, pool_w[l], pool_scale[l], sgu_w[l], sgu_b[l], seq=seq)
        y_sb = _sb_attn(proj, sb_bias[l], batch=bp, seq=seq, blk=256)
        dq, dk, dv, bg, bgt = _dn_prep(proj, bg_raw, dn_conv_w[l], neg_a_row, dt_row, seq=seq, rows=256)
        y_dn, s_end = _dn_scan(dq, dk, dv, proj, bg, bgt, dn_norm_g[l], batch=bp, seq=seq, chunks=4)
        h2 = out_proj(h1, (y_pool, y_sgu, y_sb, y_dn), tm=256)
        hp, = ffn(h2, tuple(w2), which=2, tm=tm_p)
        proj3 = proj.reshape(bp, seq, N_MAIN_COLS)
        outs["poolp"].append(proj3[:, seq - POOL_BUF:, :G_W])
        outs["convp"].append(proj3[:, seq - (CONV_W - 1):, COL_DN_QKV * G_W:(COL_DN_QKV + 3) * G_W])
        outs["sp"].append(s_end)

    st = lambda name: jnp.stack(outs[name])
    new_pool, new_conv, new_delta = new_states
    kp, vp = (a.reshape(depth, bp, seq, HEADS, HEAD_DIM) for a in kv_rows)
    return (hp.reshape(bp, seq, d_model), hs.reshape(bs, 1, d_model),
            kp, vp, st("poolp"), st("convp"), st("sp"),
            st("ks"), st("vs"), jnp.swapaxes(new_pool, 1, 2), jnp.swapaxes(new_conv, 1, 2),
            new_delta, st("sgus"))
```

```python
import functools

import jax
import jax.numpy as jnp
import numpy as np
from jax import lax
from jax.experimental import pallas as pl
from jax.experimental.pallas import tpu as pltpu

F32 = jnp.float32
BF16 = jnp.bfloat16
HIGHEST = lax.Precision.HIGHEST

HEADS = 4
HEAD_DIM = 128
G_W = HEADS * HEAD_DIM
POOL_WINDOWS = (2, 4, 8, 16)
POOL_BUF = max(POOL_WINDOWS) - 1
POOL_HALO = 16
SGU_CHUNK = 128
DN_CHUNK = 64
CONV_W = 4
CONV_HALO = 8
LN_EPS = 1e-5
NORM_EPS = 1e-6
VMEM_LIMIT_BYTES = 48 * 1024 * 1024
FUSED_VMEM_LIMIT_BYTES = 58 * 1024 * 1024

COL_POOL, COL_SGU_U, COL_SGU_V, COL_SB_Q, COL_SB_K, COL_SB_V, COL_DN_QKV, COL_DN_Z = 0, 1, 2, 3, 4, 5, 6, 9
N_MAIN_COLS = 10 * G_W
BG_LANES = 128


def _params(*sem):
    return pltpu.CompilerParams(dimension_semantics=sem, vmem_limit_bytes=VMEM_LIMIT_BYTES)


def _hdot(a, b):
    return jnp.dot(a, b, precision=HIGHEST, preferred_element_type=F32)


def _bdot(a, b):
    return jnp.dot(a.astype(BF16), b.astype(BF16), preferred_element_type=F32)


def _bdot_nt(a, b):
    return lax.dot_general(a.astype(BF16), b.astype(BF16), (((1,), (1,)), ((), ())),
                           preferred_element_type=F32)


def _silu(x):
    return x * jax.nn.sigmoid(x)


def _softplus(x):
    return jnp.maximum(x, 0.0) + jnp.log1p(jnp.exp(-jnp.abs(x)))


def _layer_norm(y, g, b):
    mu = jnp.mean(y, -1, keepdims=True)
    d = y - mu
    var = jnp.mean(d * d, -1, keepdims=True)
    return d * lax.rsqrt(var + LN_EPS) * g + b


def _split_bf16(x):
    hi = x.astype(BF16)
    lo = (x - hi.astype(F32)).astype(BF16)
    return hi, lo


_B_NN = (((2,), (1,)), ((0,), (0,)))
_B_NT = (((2,), (2,)), ((0,), (0,)))
_B_TN = (((1,), (1,)), ((0,), (0,)))


def _ffn_ln_kernel(x_ref, *refs, alpha, emit_bf16, tf):
    if emit_bf16:
        wg_ref, wu_ref, wo_ref, g_ref, b_ref, o_ref, wgu_out, wo_out, xb_ref, acc_ref = refs
    else:
        wgu_ref, wo_ref, g_ref, b_ref, o_ref, xb_ref, acc_ref = refs
    f = pl.program_id(1)

    @pl.when(f == 0)
    def _():
        xb_ref[...] = x_ref[...].astype(BF16)
        acc_ref[...] = jnp.zeros_like(acc_ref)

    xb = xb_ref[...]
    if emit_bf16:
        wg, wu, wo = wg_ref[...].astype(BF16), wu_ref[...].astype(BF16), wo_ref[...].astype(BF16)
        wgu_out[:, :tf], wgu_out[:, tf:], wo_out[...] = wg, wu, wo
        gate = jnp.dot(xb, wg, preferred_element_type=F32)
        up = jnp.dot(xb, wu, preferred_element_type=F32)
    else:
        wo = wo_ref[...]
        gate_up = jnp.dot(xb, wgu_ref[...], preferred_element_type=F32)
        gate, up = gate_up[:, :tf], gate_up[:, tf:]
    act = (_silu(gate) * up).astype(BF16)
    acc_ref[...] += jnp.dot(act, wo, preferred_element_type=F32)

    @pl.when(f == pl.num_programs(1) - 1)
    def _():
        o_ref[...] = _layer_norm(alpha * x_ref[...] + 0.5 * acc_ref[...], g_ref[...], b_ref[...])


def _ffn_ln(x, weights, ln_g, ln_b, *, layer, which, alpha, tm, tf):
    n, d = x.shape
    emit_bf16 = weights[0].dtype == F32
    if emit_bf16:
        w_in, w_out = weights
        dff = w_out.shape[1]
        nf = dff // tf
        operands = (w_in, w_in, w_out)
        w_specs = [pl.BlockSpec((None, d, tf), lambda i, f: (layer, 0, f)),
                   pl.BlockSpec((None, d, tf), lambda i, f: (layer, 0, f + nf)),
                   pl.BlockSpec((None, tf, d), lambda i, f: (layer, f, 0))]
    else:
        operands = weights
        dff = weights[1].shape[0]
        nf = dff // tf
        w_specs = [pl.BlockSpec((d, 2 * tf), lambda i, f: (0, f)),
                   pl.BlockSpec((tf, d), lambda i, f: (f, 0))]
    ln_spec = pl.BlockSpec((None, None, 1, d), lambda i, f: (layer, which, 0, 0))
    out_shape = [jax.ShapeDtypeStruct((n, d), F32)]
    out_specs = [pl.BlockSpec((tm, d), lambda i, f: (i, 0))]
    if emit_bf16:
        assert n == tm
        out_shape += [jax.ShapeDtypeStruct((d, 2 * dff), BF16), jax.ShapeDtypeStruct((dff, d), BF16)]
        out_specs += [pl.BlockSpec((d, 2 * tf), lambda i, f: (0, f)), pl.BlockSpec((tf, d), lambda i, f: (f, 0))]
    return pl.pallas_call(
        functools.partial(_ffn_ln_kernel, alpha=alpha, emit_bf16=emit_bf16, tf=tf),
        out_shape=tuple(out_shape),
        grid=(n // tm, nf),
        in_specs=[pl.BlockSpec((tm, d), lambda i, f: (i, 0))] + w_specs + [ln_spec, ln_spec],
        out_specs=tuple(out_specs),
        scratch_shapes=[pltpu.VMEM((tm, d), BF16), pltpu.VMEM((tm, d), F32)],
        compiler_params=_params("parallel", "arbitrary"),
        name="ffn_ln_cast" if emit_bf16 else "ffn_ln",
    )(x, *operands, ln_g.reshape(ln_g.shape[0], ln_g.shape[1], 1, d), ln_b.reshape(ln_b.shape[0], ln_b.shape[1], 1, d))


def _store_head_rows(dst_ref, block, tm):
    for h in range(HEADS):
        dst_ref[pl.ds(h, tm, stride=HEADS), :] = block[:, h * HEAD_DIM:(h + 1) * HEAD_DIM]


def _in_proj_kernel(*refs, tm, aliased):
    h_ref, w_ref, wt_ref = refs[:3]
    o_ref, bg_ref, k_ref, v_ref = refs[3 + aliased:]
    hb = h_ref[...].astype(BF16)
    bg_ref[...] = jnp.dot(hb, wt_ref[...], preferred_element_type=F32)
    o = jnp.dot(hb, w_ref[...], preferred_element_type=F32)
    o_ref[...] = o
    _store_head_rows(k_ref, o[:, COL_SB_K * G_W:(COL_SB_K + 1) * G_W], tm)
    _store_head_rows(v_ref, o[:, COL_SB_V * G_W:(COL_SB_V + 1) * G_W], tm)


def _in_proj(h, w_main, w_tail, kv_prev, *, layer, depth, tm):
    n, d = h.shape
    steps = n // tm
    resident = pl.Buffered(1)
    rows_kv = jax.ShapeDtypeStruct((depth * n * HEADS, HEAD_DIM), F32)
    kv_spec = pl.BlockSpec((tm * HEADS, HEAD_DIM), lambda i: (layer * steps + i, 0))
    aliased = 0 if kv_prev is None else 2
    prev = () if kv_prev is None else tuple(kv_prev)
    return pl.pallas_call(
        functools.partial(_in_proj_kernel, tm=tm, aliased=aliased),
        out_shape=(jax.ShapeDtypeStruct((n, N_MAIN_COLS), F32), jax.ShapeDtypeStruct((n, BG_LANES), F32),
                   rows_kv, rows_kv),
        grid=(steps,),
        in_specs=[
            pl.BlockSpec((tm, d), lambda i: (i, 0)),
            pl.BlockSpec((d, N_MAIN_COLS), lambda i: (0, 0), pipeline_mode=resident),
            pl.BlockSpec((None, d, BG_LANES), lambda i: (layer, 0, 0), pipeline_mode=resident),
        ] + [pl.BlockSpec(memory_space=pl.ANY)] * aliased,
        out_specs=(pl.BlockSpec((tm, N_MAIN_COLS), lambda i: (i, 0)),
                   pl.BlockSpec((tm, BG_LANES), lambda i: (i, 0)), kv_spec, kv_spec),
        input_output_aliases={3: 2, 4: 3} if aliased else {},
        compiler_params=_params("parallel"),
        name="in_proj",
    )(h, w_main, w_tail, *prev)


def _in_proj_cast_kernel(h_ref, w_ref, wt_ref, o_ref, bg_ref, k_ref, v_ref, wb_ref, *, tm):
    j = pl.program_id(0)
    hb = h_ref[...].astype(BF16)
    wb = w_ref[...].T.astype(BF16)
    wb_ref[...] = wb
    o = jnp.dot(hb, wb, preferred_element_type=F32)
    o_ref[...] = o

    @pl.when(j == 0)
    def _():
        bg_ref[...] = jnp.dot(hb, wt_ref[...], preferred_element_type=F32)

    @pl.when(j == COL_SB_K)
    def _():
        _store_head_rows(k_ref, o, tm)

    @pl.when(j == COL_SB_V)
    def _():
        _store_head_rows(v_ref, o, tm)


def _in_proj_cast(h, w_in_t, w_tail, *, layer):
    n, d = h.shape
    rows_kv = jax.ShapeDtypeStruct((n * HEADS, HEAD_DIM), F32)
    kv_spec = pl.BlockSpec((n * HEADS, HEAD_DIM), lambda j: (0, 0))
    return pl.pallas_call(
        functools.partial(_in_proj_cast_kernel, tm=n),
        out_shape=(jax.ShapeDtypeStruct((n, N_MAIN_COLS), F32), jax.ShapeDtypeStruct((n, BG_LANES), F32),
                   rows_kv, rows_kv, jax.ShapeDtypeStruct((d, N_MAIN_COLS), BF16)),
        grid=(N_MAIN_COLS // G_W,),
        in_specs=[
            pl.BlockSpec((n, d), lambda j: (0, 0)),
            pl.BlockSpec((None, G_W, d), lambda j: (layer, j, 0)),
            pl.BlockSpec((None, d, BG_LANES), lambda j: (layer, 0, 0)),
        ],
        out_specs=(pl.BlockSpec((n, G_W), lambda j: (0, j)),
                   pl.BlockSpec((n, BG_LANES), lambda j: (0, 0)), kv_spec, kv_spec,
                   pl.BlockSpec((d, G_W), lambda j: (0, j))),
        compiler_params=_params("arbitrary"),
        name="in_proj_cast",
    )(h, w_in_t, w_tail)


def _out_proj_ln_kernel(h_ref, y0_ref, y1_ref, y2_ref, y3_ref, w_ref, g_ref, b_ref, o_ref, *, alpha):
    mixed = jnp.concatenate([y_ref[...].astype(BF16) for y_ref in (y0_ref, y1_ref, y2_ref, y3_ref)], axis=1)
    acc = jnp.dot(mixed, w_ref[...], preferred_element_type=F32)
    o_ref[...] = _layer_norm(alpha * h_ref[...] + acc, g_ref[...], b_ref[...])


def _out_proj_ln(h, ys, w_out, ln_g, ln_b, *, layer, alpha, tm):
    n, d = h.shape
    y_spec = pl.BlockSpec((tm, G_W), lambda i: (i, 0))
    ln_spec = pl.BlockSpec((None, None, 1, d), lambda i: (layer, 1, 0, 0))
    return pl.pallas_call(
        functools.partial(_out_proj_ln_kernel, alpha=alpha),
        out_shape=jax.ShapeDtypeStruct((n, d), F32),
        grid=(n // tm,),
        in_specs=[pl.BlockSpec((tm, d), lambda i: (i, 0)), y_spec, y_spec, y_spec, y_spec,
                  pl.BlockSpec((None, 4 * G_W, d), lambda i: (layer, 0, 0)), ln_spec, ln_spec],
        out_specs=pl.BlockSpec((tm, d), lambda i: (i, 0)),
        compiler_params=_params("parallel"),
        name="out_proj_ln",
    )(h, *ys, w_out, ln_g.reshape(ln_g.shape[0], ln_g.shape[1], 1, d), ln_b.reshape(ln_b.shape[0], ln_b.shape[1], 1, d))


def _pool_sgu_kernel(cur_ref, halo_ref, u_ref, v_ref, pw_ref, ps_ref, sw_ref, sbc_ref,
                     yp_ref, ys_ref, ext_ref, *, chunks_per_seq):
    c = pl.program_id(0) % chunks_per_seq
    cur = cur_ref[...]
    ext_ref[0:POOL_HALO, :] = jnp.where(c == 0, 0.0, halo_ref[...])
    ext_ref[POOL_HALO:, :] = cur
    rows = SGU_CHUNK
    pos = c * rows + lax.broadcasted_iota(jnp.int32, (rows, 1), 0)
    for gi, w in enumerate(POOL_WINDOWS):
        cols = slice(gi * HEAD_DIM, (gi + 1) * HEAD_DIM)
        s = cur[:, cols]
        for j in range(1, w):
            s = s + ext_ref[POOL_HALO - j:POOL_HALO - j + rows, cols]
        cnt = jnp.minimum(pos + 1, w).astype(F32)
        d = s / cnt - cur[:, cols]
        yp_ref[:, cols] = _bdot(d, pw_ref[gi]) * ps_ref[:, cols]

    r = lax.broadcasted_iota(jnp.int32, (rows, rows), 0)
    s_ = lax.broadcasted_iota(jnp.int32, (rows, rows), 1)
    causal = r >= s_
    for h in range(HEADS):
        cols = slice(h * HEAD_DIM, (h + 1) * HEAD_DIM)
        wm = jnp.where(causal, sw_ref[h], 0.0)
        mixed = _bdot(wm, v_ref[:, cols]) + sbc_ref[:, h:h + 1]
        ys_ref[:, cols] = u_ref[:, cols] * mixed


def _pool_sgu(proj, pool_w, pool_scale, sgu_w, sgu_b, *, seq):
    n = proj.shape[0]
    rows = SGU_CHUNK
    halo_blocks = rows // POOL_HALO
    blk = lambda col: pl.BlockSpec((rows, G_W), lambda g: (g, col))
    whole = lambda a: pl.BlockSpec(a.shape, lambda g: (0,) * a.ndim)
    ps = pool_scale.reshape(1, G_W)
    sbc = sgu_b.T
    return pl.pallas_call(
        functools.partial(_pool_sgu_kernel, chunks_per_seq=seq // rows),
        out_shape=(jax.ShapeDtypeStruct((n, G_W), F32), jax.ShapeDtypeStruct((n, G_W), F32)),
        grid=(n // rows,),
        in_specs=[blk(COL_POOL),
                  pl.BlockSpec((POOL_HALO, G_W), lambda g: (jnp.maximum(g * halo_blocks - 1, 0), COL_POOL)),
                  blk(COL_SGU_U), blk(COL_SGU_V), whole(pool_w), whole(ps), whole(sgu_w), whole(sbc)],
        out_specs=(pl.BlockSpec((rows, G_W), lambda g: (g, 0)), pl.BlockSpec((rows, G_W), lambda g: (g, 0))),
        scratch_shapes=[pltpu.VMEM((POOL_HALO + rows, G_W), F32)],
        compiler_params=_params("parallel"),
        name="pool_sgu",
    )(proj, proj, proj, proj, pool_w, ps, sgu_w, sbc)


def _sb_attn_kernel(bias_ref, q_ref, k_ref, v_ref, o_ref, *, blk, scale):
    qi = pl.program_id(1)
    heads = lambda x: jnp.stack([x[:, h * HEAD_DIM:(h + 1) * HEAD_DIM] for h in range(HEADS)])
    q = heads(q_ref[...]).astype(BF16)
    bias = bias_ref[...]
    row = lax.broadcasted_iota(jnp.int32, (blk, blk), 0)
    col = lax.broadcasted_iota(jnp.int32, (blk, blk), 1)
    after = (row > col).astype(BF16)

    def visit(j, carry, valid):
        o, run = carry
        start = pl.multiple_of(j * blk, blk)
        kb = heads(k_ref[pl.ds(start, blk), :]).astype(BF16)
        vb = heads(v_ref[pl.ds(start, blk), :]).astype(BF16)
        z = lax.dot_general(q, kb, _B_NT, preferred_element_type=F32) * scale + bias
        sp = jnp.maximum(z, 0.0) + jnp.log(1.0 + jnp.exp(-jnp.abs(z)))
        log_fail = -sp if valid is None else jnp.where(valid, -sp, 0.0)
        hi, lo = _split_bf16(log_fail.reshape(HEADS * blk, blk))
        later = (jnp.dot(hi, after, preferred_element_type=F32)
                 + jnp.dot(lo, after, preferred_element_type=F32)).reshape(HEADS, blk, blk) + run
        att = jnp.exp(z - sp + later)
        if valid is not None:
            att = jnp.where(valid, att, 0.0)
        o = o + lax.dot_general(att.astype(BF16), vb, _B_NN, preferred_element_type=F32)
        run = run + jnp.sum(log_fail, axis=-1, keepdims=True)
        return o, run

    init = (jnp.zeros((HEADS, blk, HEAD_DIM), F32), jnp.zeros((HEADS, blk, 1), F32))
    carry = visit(qi, init, (col < row)[None])
    o, _ = lax.fori_loop(0, qi, lambda step, carry: visit(qi - 1 - step, carry, None), carry)
    for h in range(HEADS):
        o_ref[:, h * HEAD_DIM:(h + 1) * HEAD_DIM] = o[h]


def _sb_attn(proj, sb_bias, *, batch, seq, blk):
    n = proj.shape[0]
    nq = seq // blk
    return pl.pallas_call(
        functools.partial(_sb_attn_kernel, blk=blk, scale=HEAD_DIM ** -0.5),
        out_shape=jax.ShapeDtypeStruct((n, G_W), F32),
        grid=(batch, nq),
        in_specs=[pl.BlockSpec((HEADS, 1, 1), lambda b, i: (0, 0, 0)),
                  pl.BlockSpec((blk, G_W), lambda b, i: (b * nq + i, COL_SB_Q)),
                  pl.BlockSpec((seq, G_W), lambda b, i: (b, COL_SB_K)),
                  pl.BlockSpec((seq, G_W), lambda b, i: (b, COL_SB_V))],
        out_specs=pl.BlockSpec((blk, G_W), lambda b, i: (b * nq + i, 0)),
        compiler_params=_params("parallel", "arbitrary"),
        name="sb_attn",
    )(sb_bias.reshape(HEADS, 1, 1), proj, proj, proj)


def _sb_decode_body(q_ref, bias_ref, cum_ref, k_refs, v_refs, o_ref, *, width, scale):
    n_pages = len(k_refs)
    rows = 8
    q = q_ref[0]
    q_rows = jnp.concatenate([q[:, h * HEAD_DIM:(h + 1) * HEAD_DIM] for h in range(HEADS)]
                             + [jnp.zeros((rows - HEADS, HEAD_DIM), F32)], axis=0).astype(BF16)
    rid = lax.broadcasted_iota(jnp.int32, (rows, width), 0)
    cid = lax.broadcasted_iota(jnp.int32, (rows, width), 1)
    own = (cid & (HEADS - 1)) == rid

    z_rows = []
    for p in range(n_pages):
        zf = _bdot_nt(q_rows, k_refs[p][...])
        z_rows.append(jnp.sum(jnp.where(own, zf, 0.0), axis=0, keepdims=True))
    z = jnp.concatenate(z_rows, axis=0) * scale + bias_ref[...]
    sp = _softplus(z)
    hi, lo = _split_bf16(-sp)
    cum = (jnp.dot(hi, cum_ref[...], preferred_element_type=F32)
           + jnp.dot(lo, cum_ref[...], preferred_element_type=F32))
    in_page, page_tot = cum[:, :width], cum[:, width:]
    pr = lax.broadcasted_iota(jnp.int32, (n_pages, n_pages), 0)
    pc = lax.broadcasted_iota(jnp.int32, (n_pages, n_pages), 1)
    later_pages = _hdot((pc > pr).astype(F32), page_tot)
    att = jnp.exp(z - sp + in_page + later_pages)

    acc = jnp.zeros((rows, HEAD_DIM), F32)
    for p in range(n_pages):
        a_p = jnp.where(own, jnp.broadcast_to(att[p:p + 1, :], (rows, width)), 0.0)
        acc = acc + _bdot(a_p, v_refs[p][...])
    for h in range(HEADS):
        o_ref[0, :, h * HEAD_DIM:(h + 1) * HEAD_DIM] = acc[h:h + 1, :]


def _same_head_sums(width):
    src = np.arange(width)[:, None]
    dst = np.arange(width)[None, :]
    same = (src - dst) % HEADS == 0
    return jnp.asarray(np.concatenate([same & (src > dst), same], axis=1), BF16)


def _ffn_ln_decode_kernel(pt_ref, x_ref, wgu_ref, wo_ref, g_ref, b_ref, q_ref, bias_ref, cum_ref, *refs,
                          alpha, tf, n_pages, width, scale, seqs):
    del pt_ref
    k_refs, v_refs = refs[:n_pages], refs[n_pages:2 * n_pages]
    o_ref, y_ref, xb_ref = refs[2 * n_pages:]
    f = pl.program_id(1)

    @pl.when(f == 0)
    def _():
        xb_ref[...] = x_ref[...].astype(BF16)
        o_ref[...] = jnp.zeros_like(o_ref)

    gate_up = jnp.dot(xb_ref[...], wgu_ref[...], preferred_element_type=F32)
    act = (_silu(gate_up[:, :tf]) * gate_up[:, tf:]).astype(BF16)
    o_ref[...] += jnp.dot(act, wo_ref[...], preferred_element_type=F32)

    @pl.when(f < seqs)
    def _():
        _sb_decode_body(q_ref, bias_ref, cum_ref, k_refs, v_refs, y_ref, width=width, scale=scale)

    @pl.when(f == pl.num_programs(1) - 1)
    def _():
        o_ref[...] = _layer_norm(alpha * x_ref[...] + 0.5 * o_ref[...], g_ref[...], b_ref[...])


def _ffn_ln_decode(x, weights, ln_g, ln_b, proj3, bias_row, cache_k, cache_v, phys_pages, *,
                   layer, which, alpha, tm, tf, n_pages, width):
    n, d = x.shape
    bs = proj3.shape[0]
    w_gate_up, w_out = weights
    nf = w_out.shape[0] // tf
    tiles = n // tm
    seqs = bs // tiles
    assert bs == seqs * tiles and seqs <= nf
    seq = lambda i, f: i * seqs + jnp.minimum(f, seqs - 1)

    def page_spec(p):
        return pl.BlockSpec((width, HEAD_DIM), lambda i, f, pt: (pt[seq(i, f) * n_pages + p], 0))

    ln_spec = pl.BlockSpec((None, None, 1, d), lambda i, f, pt: (layer, which, 0, 0))
    grid_spec = pltpu.PrefetchScalarGridSpec(
        num_scalar_prefetch=1,
        grid=(tiles, nf),
        in_specs=[pl.BlockSpec((tm, d), lambda i, f, pt: (i, 0)),
                  pl.BlockSpec((d, 2 * tf), lambda i, f, pt: (0, f)),
                  pl.BlockSpec((tf, d), lambda i, f, pt: (f, 0)),
                  ln_spec, ln_spec,
                  pl.BlockSpec((1, 1, G_W), lambda i, f, pt: (seq(i, f), 0, COL_SB_Q)),
                  pl.BlockSpec((1, width), lambda i, f, pt: (0, 0)),
                  pl.BlockSpec((width, 2 * width), lambda i, f, pt: (0, 0), pipeline_mode=pl.Buffered(1))]
                 + [page_spec(p) for p in range(n_pages)] + [page_spec(p) for p in range(n_pages)],
        out_specs=(pl.BlockSpec((tm, d), lambda i, f, pt: (i, 0)),
                   pl.BlockSpec((1, 1, G_W), lambda i, f, pt: (seq(i, f), 0, 0))),
        scratch_shapes=[pltpu.VMEM((tm, d), BF16)],
    )
    out, y_sb = pl.pallas_call(
        functools.partial(_ffn_ln_decode_kernel, alpha=alpha, tf=tf, n_pages=n_pages, width=width,
                          scale=HEAD_DIM ** -0.5, seqs=seqs),
        out_shape=(jax.ShapeDtypeStruct((n, d), F32), jax.ShapeDtypeStruct((bs, 1, G_W), F32)),
        grid_spec=grid_spec,
        compiler_params=pltpu.CompilerParams(dimension_semantics=("arbitrary", "arbitrary"),
                                             vmem_limit_bytes=FUSED_VMEM_LIMIT_BYTES),
        name="ffn_ln_decode",
    )(phys_pages, x, w_gate_up, w_out,
      ln_g.reshape(ln_g.shape[0], ln_g.shape[1], 1, d), ln_b.reshape(ln_b.shape[0], ln_b.shape[1], 1, d),
      proj3, bias_row, _same_head_sums(width), *([cache_k] * n_pages), *([cache_v] * n_pages))
    return out, y_sb.reshape(bs, G_W)


def _beta_and_log_decay(raw, neg_a_row, dt_row):
    lane = lax.broadcasted_iota(jnp.int32, raw.shape, 1)
    beta = jax.nn.sigmoid(raw)
    g = neg_a_row * _softplus(raw + dt_row)
    return jnp.where(lane < HEADS, beta, g)


def _l2_normalize(x):
    return x * lax.rsqrt(jnp.sum(x * x, -1, keepdims=True) + NORM_EPS)


def _dn_prep_kernel(raw_ref, halo_ref, bgr_ref, cw_ref, na_ref, dt_ref,
                    q_ref, k_ref, v_ref, bg_ref, bgt_ref, ext_ref, *, tiles_per_seq, rows):
    first = (pl.program_id(0) % tiles_per_seq) == 0
    ext_ref[0:CONV_HALO, :] = jnp.where(first, 0.0, halo_ref[...])
    ext_ref[CONV_HALO:, :] = raw_ref[...]
    conv = None
    for j in range(CONV_W):
        off = CONV_HALO - (CONV_W - 1) + j
        term = ext_ref[off:off + rows, :] * cw_ref[j:j + 1, :]
        conv = term if conv is None else conv + term
    act = _silu(conv)
    for h in range(HEADS):
        cols = slice(h * HEAD_DIM, (h + 1) * HEAD_DIM)
        q_ref[:, cols] = _l2_normalize(act[:, h * HEAD_DIM:(h + 1) * HEAD_DIM]) * (HEAD_DIM ** -0.5)
        k_ref[:, cols] = _l2_normalize(act[:, G_W + h * HEAD_DIM:G_W + (h + 1) * HEAD_DIM])
    v_ref[...] = act[:, 2 * G_W:]
    bg = _beta_and_log_decay(bgr_ref[...], na_ref[...], dt_ref[...])
    r = lax.broadcasted_iota(jnp.int32, (rows, rows), 0)
    s = lax.broadcasted_iota(jnp.int32, (rows, rows), 1)
    shift = DN_CHUNK.bit_length() - 1
    chunk_prefix = ((r >= s) & (lax.shift_right_logical(r, shift) == lax.shift_right_logical(s, shift)))
    lane = lax.broadcasted_iota(jnp.int32, bg.shape, 1)
    bg = jnp.where(lane < HEADS, bg, _hdot(chunk_prefix.astype(F32), bg))
    bg_ref[...] = bg
    per_tile = BG_LANES // DN_CHUNK
    for t in range(rows // BG_LANES):
        tile_t = bg[t * BG_LANES:(t + 1) * BG_LANES, :].T
        for c in range(per_tile):
            bgt_ref[t * per_tile + c] = tile_t[0:2 * HEADS, c * DN_CHUNK:(c + 1) * DN_CHUNK]


def _dn_prep(proj, bg_raw, conv_w, neg_a_row, dt_row, *, seq, rows):
    n = proj.shape[0]
    halo_blocks = rows // CONV_HALO
    qkv_w = 3 * G_W
    col = COL_DN_QKV * G_W // qkv_w
    whole = lambda a: pl.BlockSpec(a.shape, lambda g: (0,) * a.ndim)
    out = jax.ShapeDtypeStruct((n, G_W), F32)
    o_spec = pl.BlockSpec((rows, G_W), lambda g: (g, 0))
    return pl.pallas_call(
        functools.partial(_dn_prep_kernel, tiles_per_seq=seq // rows, rows=rows),
        out_shape=(out, out, out, jax.ShapeDtypeStruct((n, BG_LANES), F32),
                   jax.ShapeDtypeStruct((n // DN_CHUNK, 2 * HEADS, DN_CHUNK), F32)),
        grid=(n // rows,),
        in_specs=[pl.BlockSpec((rows, qkv_w), lambda g: (g, col)),
                  pl.BlockSpec((CONV_HALO, qkv_w), lambda g: (jnp.maximum(g * halo_blocks - 1, 0), col)),
                  pl.BlockSpec((rows, BG_LANES), lambda g: (g, 0)),
                  whole(conv_w), whole(neg_a_row), whole(dt_row)],
        out_specs=(o_spec, o_spec, o_spec, pl.BlockSpec((rows, BG_LANES), lambda g: (g, 0)),
                   pl.BlockSpec((rows // DN_CHUNK, 2 * HEADS, DN_CHUNK), lambda g: (g, 0, 0))),
        scratch_shapes=[pltpu.VMEM((CONV_HALO + rows, qkv_w), F32)],
        compiler_params=_params("parallel"),
        name="dn_prep",
    )(proj, proj, bg_raw, conv_w, neg_a_row, dt_row)


def _cat_lhs(x, axis):
    hi = x.astype(BF16).astype(F32)
    return jnp.concatenate([hi, x - hi, hi], axis=axis).astype(BF16)


def _cat_rhs(x, axis):
    hi = x.astype(BF16)
    lo = (x - hi.astype(F32)).astype(BF16)
    return jnp.concatenate([hi, hi, lo], axis=axis)


def _mm(lhs_cat, rhs_cat, dims=_B_NN):
    return lax.dot_general(lhs_cat, rhs_cat, dims, preferred_element_type=F32)


def _unit_lower_inverse(a_mat, eye):
    n = a_mat.shape[-1]
    inv = eye - a_mat
    power = _mm(_cat_lhs(a_mat, 2), _cat_rhs(a_mat, 1))
    span = 2
    while span < n:
        p_rhs = _cat_rhs(power, 1)
        inv = inv + _mm(_cat_lhs(inv, 2), p_rhs)
        span *= 2
        if span < n:
            power = _mm(_cat_lhs(power, 2), p_rhs)
    return inv


def _dn_out(o, z, norm_g):
    o = o * lax.rsqrt(jnp.mean(o * o, -1, keepdims=True) + NORM_EPS) * norm_g
    return o * _silu(z)


def _dn_scan_kernel(q_ref, k_ref, v_ref, z_ref, bg_ref, bgt_ref, ng_ref, y_ref, s_out_ref, s_ref, *, chunks):
    t = pl.program_id(1)

    @pl.when(t == 0)
    def _():
        s_ref[...] = jnp.zeros_like(s_ref)

    c, d = DN_CHUNK, HEAD_DIM
    units = [(ci, h) for ci in range(chunks) for h in range(HEADS)]
    rows = lambda ci: slice(ci * c, (ci + 1) * c)
    cols = lambda h: slice(h * d, (h + 1) * d)
    gather = lambda ref: jnp.stack([ref[rows(ci), cols(h)] for ci, h in units])
    q, k, v = gather(q_ref), gather(k_ref), gather(v_ref)
    beta = jnp.stack([bg_ref[rows(ci), h:h + 1] for ci, h in units])
    g_col = jnp.stack([bg_ref[rows(ci), HEADS + h:HEADS + h + 1] for ci, h in units])
    g_row = jnp.stack([bgt_ref[ci, HEADS + h:HEADS + h + 1, :] for ci, h in units])

    r = lax.broadcasted_iota(jnp.int32, (c, c), 0)
    s_ = lax.broadcasted_iota(jnp.int32, (c, c), 1)
    incl, strict, eye = (r >= s_)[None], (r > s_)[None], (r == s_).astype(F32)[None]
    decay = jnp.where(incl, jnp.exp(jnp.where(incl, g_col - g_row, 0.0)), 0.0)
    qk_hi, qk_lo = _split_bf16(jnp.concatenate([q, k], axis=1))
    k_hi, k_lo = _split_bf16(k)
    qk_kk = (_mm(jnp.concatenate([qk_hi, qk_lo], axis=2), jnp.concatenate([k_hi, k_hi], axis=2), _B_NT)
             + _mm(qk_hi, k_lo, _B_NT))
    qk = qk_kk[:, :c] * decay
    a_mat = jnp.where(strict, beta * qk_kk[:, c:] * decay, 0.0)
    inv = _unit_lower_inverse(a_mat, eye)
    e_g = jnp.exp(g_col)
    rhs = jnp.concatenate([beta * v, (beta * e_g) * k], axis=2)
    sol = _mm(_cat_lhs(inv, 2), _cat_rhs(rhs, 1))
    u_base, w_mat = sol[:, :, :d], sol[:, :, d:]
    w_hi, w_lo = _split_bf16(w_mat)
    w_cat = jnp.concatenate([w_hi, w_lo], axis=2)
    q_dec = (q * e_g).astype(BF16)
    g_last = g_col[:, c - 1:c, :]
    k_dec = k * jnp.exp(g_last - g_col)
    keep = jnp.exp(g_last)

    norm_g = ng_ref[...]
    s = s_ref[...]
    for ci in range(chunks):
        sl = slice(ci * HEADS, (ci + 1) * HEADS)
        s_hi, s_lo = _split_bf16(s)
        u = u_base[sl] - (_mm(w_cat[sl], jnp.concatenate([s_hi, s_hi], axis=1)) + _mm(w_hi[sl], s_lo))
        o = _mm(q_dec[sl], s_hi) + _mm(qk[sl].astype(BF16), u.astype(BF16))
        s = s * keep[sl] + _mm(_cat_lhs(k_dec[sl], 1), _cat_rhs(u, 1), _B_TN)
        for h in range(HEADS):
            y_ref[rows(ci), cols(h)] = _dn_out(o[h], z_ref[rows(ci), cols(h)], norm_g)
    s_ref[...] = s

    @pl.when(t == pl.num_programs(1) - 1)
    def _():
        s_out_ref[0] = s_ref[...]


def _dn_scan(q, k, v, proj, bg, bgt, norm_g, *, batch, seq, chunks):
    n = q.shape[0]
    rows = chunks * DN_CHUNK
    steps = seq // rows
    spec = pl.BlockSpec((rows, G_W), lambda b, t: (b * steps + t, 0))
    ng = norm_g.reshape(1, HEAD_DIM)
    return pl.pallas_call(
        functools.partial(_dn_scan_kernel, chunks=chunks),
        out_shape=(jax.ShapeDtypeStruct((n, G_W), F32),
                   jax.ShapeDtypeStruct((batch, HEADS, HEAD_DIM, HEAD_DIM), F32)),
        grid=(batch, steps),
        in_specs=[spec, spec, spec,
                  pl.BlockSpec((rows, G_W), lambda b, t: (b * steps + t, COL_DN_Z)),
                  pl.BlockSpec((rows, BG_LANES), lambda b, t: (b * steps + t, 0)),
                  pl.BlockSpec((chunks, 2 * HEADS, DN_CHUNK), lambda b, t: (b * steps + t, 0, 0)),
                  pl.BlockSpec((1, HEAD_DIM), lambda b, t: (0, 0))],
        out_specs=(spec, pl.BlockSpec((1, HEADS, HEAD_DIM, HEAD_DIM), lambda b, t: (b, 0, 0, 0))),
        scratch_shapes=[pltpu.VMEM((HEADS, HEAD_DIM, HEAD_DIM), F32)],
        compiler_params=_params("parallel", "arbitrary"),
        name="dn_scan",
    )(q, k, v, proj, bg, bgt, ng)


def _sample_mix_kernel(pool_ref, u_ref, v_ref, qkv_ref, z_ref, bgr_ref, sp_ref, sc_ref, sd_ref,
                       pw_ref, ps_ref, sw0_ref, sb0_ref, cw_ref, na_ref, dt_ref, ng_ref, *rest, bt, pos0):
    yp_ref, ys_ref, yd_ref, np_ref, nc_ref, nd_ref = rest[-6:]
    new = pool_ref[...]
    for gi, w in enumerate(POOL_WINDOWS):
        cols = slice(gi * HEAD_DIM, (gi + 1) * HEAD_DIM)
        s = new[:, cols]
        for j in range(1, w):
            s = s + sp_ref[POOL_BUF - j, :, cols]
        d = s / float(min(pos0 + 1, w)) - new[:, cols]
        yp_ref[:, cols] = _hdot(d, pw_ref[gi]) * ps_ref[:, cols]
    np_ref[0:POOL_BUF - 1] = sp_ref[1:POOL_BUF]
    np_ref[POOL_BUF - 1] = new

    ys_ref[...] = u_ref[...] * (sw0_ref[...] * v_ref[...] + sb0_ref[...])

    raw = qkv_ref[...]
    conv = raw * cw_ref[CONV_W - 1:CONV_W, :]
    for j in range(CONV_W - 1):
        conv = conv + sc_ref[j] * cw_ref[j:j + 1, :]
    nc_ref[0:CONV_W - 2] = sc_ref[1:CONV_W - 1]
    nc_ref[CONV_W - 2] = raw
    act = _silu(conv)
    bg = _beta_and_log_decay(bgr_ref[...], na_ref[...], dt_ref[...])
    norm_g = ng_ref[...]
    d = HEAD_DIM
    pad = jnp.zeros((d - bt, d), F32)
    k_t, q_t = [], []
    for h in range(HEADS):
        k_t.append(jnp.concatenate([_l2_normalize(act[:, G_W + h * d:G_W + (h + 1) * d]), pad], axis=0).T)
        q_t.append(jnp.concatenate([_l2_normalize(act[:, h * d:(h + 1) * d]) * (d ** -0.5), pad], axis=0).T)
    pairs = [(b, h) for b in range(bt) for h in range(HEADS)]
    v_h = [act[:, 2 * G_W + h * d:2 * G_W + (h + 1) * d] for h in range(HEADS)]
    z_h = [z_ref[:, h * d:(h + 1) * d] for h in range(HEADS)]
    a = [jnp.exp(bg[b:b + 1, HEADS + h:HEADS + h + 1]) for b, h in pairs]
    k_col = [jnp.broadcast_to(k_t[h][:, b:b + 1], (d, d)) for b, h in pairs]
    k_s = [jnp.sum(k_col[i] * sd_ref[b, h], axis=0, keepdims=True) for i, (b, h) in enumerate(pairs)]
    u = [bg[b:b + 1, h:h + 1] * (v_h[h][b:b + 1, :] - a[i] * k_s[i]) for i, (b, h) in enumerate(pairs)]
    s1 = [a[i] * sd_ref[b, h] + k_col[i] * u[i] for i, (b, h) in enumerate(pairs)]
    for i, (b, h) in enumerate(pairs):
        nd_ref[b, h] = s1[i]
    o = [jnp.sum(jnp.broadcast_to(q_t[h][:, b:b + 1], (d, d)) * s1[i], axis=0, keepdims=True)
         for i, (b, h) in enumerate(pairs)]
    for i, (b, h) in enumerate(pairs):
        yd_ref[b:b + 1, h * d:(h + 1) * d] = _dn_out(o[i], z_h[h][b:b + 1, :], norm_g)


def _sample_mix(proj, bg_raw, state_pool, state_conv, state_delta, new_prev, pool_w, pool_scale, sgu_w0, sgu_b0,
                conv_w, neg_a_row, dt_row, norm_g, *, layer, bt, pos0):
    bs = proj.shape[0]
    qkv_w = 3 * G_W
    col = lambda c: pl.BlockSpec((bt, G_W), lambda i: (i, c))
    whole = lambda a: pl.BlockSpec(a.shape, lambda i: (0,) * a.ndim)
    row = pl.BlockSpec((bt, G_W), lambda i: (i, 0))
    ps = pool_scale.reshape(1, G_W)
    ng = norm_g.reshape(1, HEAD_DIM)
    consts = (pool_w, ps, sgu_w0, sgu_b0, conv_w, neg_a_row, dt_row, ng)
    out_sds = lambda a: jax.ShapeDtypeStruct(a.shape, F32)
    pool_spec = pl.BlockSpec((None, POOL_BUF, bt, G_W), lambda i: (layer, 0, i, 0))
    conv_spec = pl.BlockSpec((None, CONV_W - 1, bt, qkv_w), lambda i: (layer, 0, i, 0))
    delta_spec = pl.BlockSpec((None, bt, HEADS, HEAD_DIM, HEAD_DIM), lambda i: (layer, i, 0, 0, 0))
    y = jax.ShapeDtypeStruct((bs, G_W), F32)
    prev = () if new_prev is None else tuple(new_prev)
    n_in = 9 + len(consts)
    return pl.pallas_call(
        functools.partial(_sample_mix_kernel, bt=bt, pos0=pos0),
        out_shape=(y, y, y, out_sds(state_pool), out_sds(state_conv), out_sds(state_delta)),
        grid=(bs // bt,),
        in_specs=[col(COL_POOL), col(COL_SGU_U), col(COL_SGU_V),
                  pl.BlockSpec((bt, qkv_w), lambda i: (i, COL_DN_QKV * G_W // qkv_w)),
                  col(COL_DN_Z),
                  pl.BlockSpec((bt, BG_LANES), lambda i: (i, 0)),
                  pool_spec, conv_spec, delta_spec] + [whole(a) for a in consts]
                 + [pl.BlockSpec(memory_space=pl.ANY)] * len(prev),
        out_specs=(row, row, row, pool_spec, conv_spec, delta_spec),
        input_output_aliases={n_in + k: 3 + k for k in range(len(prev))},
        compiler_params=_params("parallel"),
        name="sample_mix",
    )(proj, proj, proj, proj, proj, bg_raw, state_pool, state_conv, state_delta, *consts, *prev)


def _lane_row(values, offset):
    return jnp.zeros((1, BG_LANES), F32).at[0, offset:offset + values.shape[0]].set(values)


def kernel(x_prompt, x_sample, cache_k, cache_v, page_table, state_pool, state_conv, state_delta,
           ln_g, ln_b, w_ffn1_in, w_ffn1_out, w_ffn2_in, w_ffn2_out, w_in, w_out,
           pool_w, pool_scale, sgu_w, sgu_b, sb_bias, dn_conv_w, dn_a_log, dn_dt_bias, dn_norm_g):
    depth = ln_g.shape[0]
    alpha = (2.0 * depth) ** 0.25
    bp, seq, d_model = x_prompt.shape
    bs, dec_seq, _ = x_sample.shape
    assert dec_seq == 1
    n_phys, page = cache_k.shape[1], cache_k.shape[2]
    n_pages = page_table.shape[1]
    pos0 = n_pages * page
    ck = cache_k.reshape(depth * n_phys * page * HEADS, HEAD_DIM)
    cv = cache_v.reshape(depth * n_phys * page * HEADS, HEAD_DIM)
    hp = x_prompt.reshape(bp * seq, d_model)
    hs = x_sample.reshape(bs, d_model)
    pool_tm = jnp.swapaxes(state_pool, 1, 2)
    conv_tm = jnp.swapaxes(state_conv, 1, 2)
    tm_p = 512
    tm_s = bs
    tail_w = w_in.shape[2] - N_MAIN_COLS
    w_in_tail = jnp.pad(w_in[:, :, N_MAIN_COLS:].astype(BF16), ((0, 0), (0, 0), (0, BG_LANES - tail_w)))
    w_in_t = jnp.swapaxes(w_in, 1, 2)
    w_out_bf = w_out.astype(BF16)

    outs = {name: [] for name in ("poolp", "convp", "sp", "ks", "vs", "sgus")}
    kv_rows = None
    new_states = None
    for l in range(depth):
        neg_a_row = _lane_row(-jnp.exp(dn_a_log[l]), HEADS)
        dt_row = _lane_row(dn_dt_bias[l], HEADS)
        ffn = functools.partial(_ffn_ln, ln_g=ln_g, ln_b=ln_b, layer=l, alpha=alpha, tf=512)
        out_proj = functools.partial(_out_proj_ln, w_out=w_out_bf, ln_g=ln_g, ln_b=ln_b, layer=l, alpha=alpha)

        g1, *w1 = ffn(hs, (w_ffn1_in, w_ffn1_out), which=0, tm=tm_s)
        sproj, sbg_raw, sk_rows, sv_rows, wi_main = _in_proj_cast(g1, w_in_t, w_in_tail, layer=l)
        sgu_w0 = jnp.repeat(sgu_w[l, :, 0, 0], HEAD_DIM).reshape(1, G_W)
        sgu_b0 = jnp.repeat(sgu_b[l, :, 0], HEAD_DIM).reshape(1, G_W)
        sy_pool, sy_sgu, sy_dn, *new_states = _sample_mix(
            sproj, sbg_raw, pool_tm, conv_tm, state_delta, new_states, pool_w[l], pool_scale[l],
            sgu_w0, sgu_b0, dn_conv_w[l], neg_a_row, dt_row, dn_norm_g[l], layer=l, bt=8, pos0=pos0)
        bias_row = jnp.tile(sb_bias[l], page).reshape(1, page * HEADS)
        phys = (page_table + l * n_phys).reshape(-1)
        h1, sy_sb = _ffn_ln_decode(hp, tuple(w1), ln_g, ln_b, sproj.reshape(bs, 1, N_MAIN_COLS), bias_row, ck, cv, phys,
                                   layer=l, which=0, alpha=alpha, tm=tm_p, tf=512,
                                   n_pages=n_pages, width=page * HEADS)
        g2 = out_proj(g1, (sy_pool, sy_sgu, sy_sb, sy_dn), tm=tm_s)
        hs, *w2 = ffn(g2, (w_ffn2_in, w_ffn2_out), which=2, tm=tm_s)
        outs["ks"].append(sk_rows.reshape(bs, 1, HEADS, HEAD_DIM))
        outs["vs"].append(sv_rows.reshape(bs, 1, HEADS, HEAD_DIM))
        outs["sgus"].append(sproj[:, COL_SGU_V * G_W:(COL_SGU_V + 1) * G_W].reshape(bs, 1, G_W))

        proj, bg_raw, *kv_rows = _in_proj(h1, wi_main, w_in_tail, kv_rows, layer=l, depth=depth, tm=256)
        y_pool, y_sgu = _pool_sgu(proj, pool_w[l], pool_scale[l], sgu_w[l], sgu_b[l], seq=seq)
        y_sb = _sb_attn(proj, sb_bias[l], batch=bp, seq=seq, blk=256)
        dq, dk, dv, bg, bgt = _dn_prep(proj, bg_raw, dn_conv_w[l], neg_a_row, dt_row, seq=seq, rows=256)
        y_dn, s_end = _dn_scan(dq, dk, dv, proj, bg, bgt, dn_norm_g[l], batch=bp, seq=seq, chunks=4)
        h2 = out_proj(h1, (y_pool, y_sgu, y_sb, y_dn), tm=256)
        hp, = ffn(h2, tuple(w2), which=2, tm=tm_p)
        proj3 = proj.reshape(bp, seq, N_MAIN_COLS)
        outs["poolp"].append(proj3[:, seq - POOL_BUF:, :G_W])
        outs["convp"].append(proj3[:, seq - (CONV_W - 1):, COL_DN_QKV * G_W:(COL_DN_QKV + 3) * G_W])
        outs["sp"].append(s_end)

    st = lambda name: jnp.stack(outs[name])
    new_pool, new_conv, new_delta = new_states
    kp, vp = (a.reshape(depth, bp, seq, HEADS, HEAD_DIM) for a in kv_rows)
    return (hp.reshape(bp, seq, d_model), hs.reshape(bs, 1, d_model),
            kp, vp, st("poolp"), st("convp"), st("sp"),
            st("ks"), st("vs"), jnp.swapaxes(new_pool, 1, 2), jnp.swapaxes(new_conv, 1, 2),
            new_delta, st("sgus"))
```

```python
import functools

import jax
import jax.numpy as jnp
import numpy as np
from jax import lax
from jax.experimental import pallas as pl
from jax.experimental.pallas import tpu as pltpu

F32 = jnp.float32
BF16 = jnp.bfloat16
HIGHEST = lax.Precision.HIGHEST

HEADS = 4
HEAD_DIM = 128
G_W = HEADS * HEAD_DIM
POOL_WINDOWS = (2, 4, 8, 16)
POOL_BUF = max(POOL_WINDOWS) - 1
POOL_HALO = 16
SGU_CHUNK = 128
DN_CHUNK = 64
CONV_W = 4
CONV_HALO = 8
LN_EPS = 1e-5
NORM_EPS = 1e-6
VMEM_LIMIT_BYTES = 48 * 1024 * 1024
FUSED_VMEM_LIMIT_BYTES = 58 * 1024 * 1024

COL_POOL, COL_SGU_U, COL_SGU_V, COL_SB_Q, COL_SB_K, COL_SB_V, COL_DN_QKV, COL_DN_Z = 0, 1, 2, 3, 4, 5, 6, 9
N_MAIN_COLS = 10 * G_W
BG_LANES = 128


def _params(*sem):
    return pltpu.CompilerParams(dimension_semantics=sem, vmem_limit_bytes=VMEM_LIMIT_BYTES)


def _hdot(a, b):
    return jnp.dot(a, b, precision=HIGHEST, preferred_element_type=F32)


def _bdot(a, b):
    return jnp.dot(a.astype(BF16), b.astype(BF16), preferred_element_type=F32)


def _bdot_nt(a, b):
    return lax.dot_general(a.astype(BF16), b.astype(BF16), (((1,), (1,)), ((), ())),
                           preferred_element_type=F32)


def _silu(x):
    return x * jax.nn.sigmoid(x)


def _softplus(x):
    return jnp.maximum(x, 0.0) + jnp.log1p(jnp.exp(-jnp.abs(x)))


def _layer_norm(y, g, b):
    mu = jnp.mean(y, -1, keepdims=True)
    d = y - mu
    var = jnp.mean(d * d, -1, keepdims=True)
    return d * lax.rsqrt(var + LN_EPS) * g + b


def _split_bf16(x):
    hi = x.astype(BF16)
    lo = (x - hi.astype(F32)).astype(BF16)
    return hi, lo


_B_NN = (((2,), (1,)), ((0,), (0,)))
_B_NT = (((2,), (2,)), ((0,), (0,)))
_B_TN = (((1,), (1,)), ((0,), (0,)))


def _ffn_ln_kernel(x_ref, *refs, alpha, emit_bf16, tf):
    if emit_bf16:
        wg_ref, wu_ref, wo_ref, g_ref, b_ref, o_ref, wgu_out, wo_out, xb_ref, acc_ref = refs
    else:
        wgu_ref, wo_ref, g_ref, b_ref, o_ref, xb_ref, acc_ref = refs
    f = pl.program_id(1)

    @pl.when(f == 0)
    def _():
        xb_ref[...] = x_ref[...].astype(BF16)
        acc_ref[...] = jnp.zeros_like(acc_ref)

    xb = xb_ref[...]
    if emit_bf16:
        wg, wu, wo = wg_ref[...].astype(BF16), wu_ref[...].astype(BF16), wo_ref[...].astype(BF16)
        wgu_out[:, :tf], wgu_out[:, tf:], wo_out[...] = wg, wu, wo
        gate = jnp.dot(xb, wg, preferred_element_type=F32)
        up = jnp.dot(xb, wu, preferred_element_type=F32)
    else:
        wo = wo_ref[...]
        gate_up = jnp.dot(xb, wgu_ref[...], preferred_element_type=F32)
        gate, up = gate_up[:, :tf], gate_up[:, tf:]
    act = (_silu(gate) * up).astype(BF16)
    acc_ref[...] += jnp.dot(act, wo, preferred_element_type=F32)

    @pl.when(f == pl.num_programs(1) - 1)
    def _():
        o_ref[...] = _layer_norm(alpha * x_ref[...] + 0.5 * acc_ref[...], g_ref[...], b_ref[...])


def _ffn_ln(x, weights, ln_g, ln_b, *, layer, which, alpha, tm, tf):
    n, d = x.shape
    emit_bf16 = weights[0].dtype == F32
    if emit_bf16:
        w_in, w_out = weights
        dff = w_out.shape[1]
        nf = dff // tf
        operands = (w_in, w_in, w_out)
        w_specs = [pl.BlockSpec((None, d, tf), lambda i, f: (layer, 0, f)),
                   pl.BlockSpec((None, d, tf), lambda i, f: (layer, 0, f + nf)),
                   pl.BlockSpec((None, tf, d), lambda i, f: (layer, f, 0))]
    else:
        operands = weights
        dff = weights[1].shape[0]
        nf = dff // tf
        w_specs = [pl.BlockSpec((d, 2 * tf), lambda i, f: (0, f)),
                   pl.BlockSpec((tf, d), lambda i, f: (f, 0))]
    ln_spec = pl.BlockSpec((None, None, 1, d), lambda i, f: (layer, which, 0, 0))
    out_shape = [jax.ShapeDtypeStruct((n, d), F32)]
    out_specs = [pl.BlockSpec((tm, d), lambda i, f: (i, 0))]
    if emit_bf16:
        assert n == tm
        out_shape += [jax.ShapeDtypeStruct((d, 2 * dff), BF16), jax.ShapeDtypeStruct((dff, d), BF16)]
        out_specs += [pl.BlockSpec((d, 2 * tf), lambda i, f: (0, f)), pl.BlockSpec((tf, d), lambda i, f: (f, 0))]
    return pl.pallas_call(
        functools.partial(_ffn_ln_kernel, alpha=alpha, emit_bf16=emit_bf16, tf=tf),
        out_shape=tuple(out_shape),
        grid=(n // tm, nf),
        in_specs=[pl.BlockSpec((tm, d), lambda i, f: (i, 0))] + w_specs + [ln_spec, ln_spec],
        out_specs=tuple(out_specs),
        scratch_shapes=[pltpu.VMEM((tm, d), BF16), pltpu.VMEM((tm, d), F32)],
        compiler_params=_params("parallel", "arbitrary"),
        name="ffn_ln_cast" if emit_bf16 else "ffn_ln",
    )(x, *operands, ln_g.reshape(ln_g.shape[0], ln_g.shape[1], 1, d), ln_b.reshape(ln_b.shape[0], ln_b.shape[1], 1, d))


def _store_head_rows(dst_ref, block, tm):
    for h in range(HEADS):
        dst_ref[pl.ds(h, tm, stride=HEADS), :] = block[:, h * HEAD_DIM:(h + 1) * HEAD_DIM]


def _in_proj_kernel(*refs, tm, aliased):
    h_ref, w_ref, wt_ref = refs[:3]
    o_ref, bg_ref, k_ref, v_ref = refs[3 + aliased:]
    hb = h_ref[...].astype(BF16)
    bg_ref[...] = jnp.dot(hb, wt_ref[...], preferred_element_type=F32)
    o = jnp.dot(hb, w_ref[...], preferred_element_type=F32)
    o_ref[...] = o
    _store_head_rows(k_ref, o[:, COL_SB_K * G_W:(COL_SB_K + 1) * G_W], tm)
    _store_head_rows(v_ref, o[:, COL_SB_V * G_W:(COL_SB_V + 1) * G_W], tm)


def _in_proj(h, w_main, w_tail, kv_prev, *, layer, depth, tm):
    n, d = h.shape
    steps = n // tm
    resident = pl.Buffered(1)
    rows_kv = jax.ShapeDtypeStruct((depth * n * HEADS, HEAD_DIM), F32)
    kv_spec = pl.BlockSpec((tm * HEADS, HEAD_DIM), lambda i: (layer * steps + i, 0))
    aliased = 0 if kv_prev is None else 2
    prev = () if kv_prev is None else tuple(kv_prev)
    return pl.pallas_call(
        functools.partial(_in_proj_kernel, tm=tm, aliased=aliased),
        out_shape=(jax.ShapeDtypeStruct((n, N_MAIN_COLS), F32), jax.ShapeDtypeStruct((n, BG_LANES), F32),
                   rows_kv, rows_kv),
        grid=(steps,),
        in_specs=[
            pl.BlockSpec((tm, d), lambda i: (i, 0)),
            pl.BlockSpec((d, N_MAIN_COLS), lambda i: (0, 0), pipeline_mode=resident),
            pl.BlockSpec((None, d, BG_LANES), lambda i: (layer, 0, 0), pipeline_mode=resident),
        ] + [pl.BlockSpec(memory_space=pl.ANY)] * aliased,
        out_specs=(pl.BlockSpec((tm, N_MAIN_COLS), lambda i: (i, 0)),
                   pl.BlockSpec((tm, BG_LANES), lambda i: (i, 0)), kv_spec, kv_spec),
        input_output_aliases={3: 2, 4: 3} if aliased else {},
        compiler_params=_params("parallel"),
        name="in_proj",
    )(h, w_main, w_tail, *prev)


def _in_proj_cast_kernel(h_ref, w_ref, wt_ref, o_ref, bg_ref, k_ref, v_ref, wb_ref, *, tm):
    j = pl.program_id(0)
    hb = h_ref[...].astype(BF16)
    wb = w_ref[...].T.astype(BF16)
    wb_ref[...] = wb
    o = jnp.dot(hb, wb, preferred_element_type=F32)
    o_ref[...] = o

    @pl.when(j == 0)
    def _():
        bg_ref[...] = jnp.dot(hb, wt_ref[...], preferred_element_type=F32)

    @pl.when(j == COL_SB_K)
    def _():
        _store_head_rows(k_ref, o, tm)

    @pl.when(j == COL_SB_V)
    def _():
        _store_head_rows(v_ref, o, tm)


def _in_proj_cast(h, w_in_t, w_tail, *, layer):
    n, d = h.shape
    rows_kv = jax.ShapeDtypeStruct((n * HEADS, HEAD_DIM), F32)
    kv_spec = pl.BlockSpec((n * HEADS, HEAD_DIM), lambda j: (0, 0))
    return pl.pallas_call(
        functools.partial(_in_proj_cast_kernel, tm=n),
        out_shape=(jax.ShapeDtypeStruct((n, N_MAIN_COLS), F32), jax.ShapeDtypeStruct((n, BG_LANES), F32),
                   rows_kv, rows_kv, jax.ShapeDtypeStruct((d, N_MAIN_COLS), BF16)),
        grid=(N_MAIN_COLS // G_W,),
        in_specs=[
            pl.BlockSpec((n, d), lambda j: (0, 0)),
            pl.BlockSpec((None, G_W, d), lambda j: (layer, j, 0)),
            pl.BlockSpec((None, d, BG_LANES), lambda j: (layer, 0, 0)),
        ],
        out_specs=(pl.BlockSpec((n, G_W), lambda j: (0, j)),
                   pl.BlockSpec((n, BG_LANES), lambda j: (0, 0)), kv_spec, kv_spec,
                   pl.BlockSpec((d, G_W), lambda j: (0, j))),
        compiler_params=_params("arbitrary"),
        name="in_proj_cast",
    )(h, w_in_t, w_tail)


def _out_proj_ln_kernel(h_ref, y0_ref, y1_ref, y2_ref, y3_ref, w_ref, g_ref, b_ref, o_ref, *, alpha):
    mixed = jnp.concatenate([y_ref[...].astype(BF16) for y_ref in (y0_ref, y1_ref, y2_ref, y3_ref)], axis=1)
    acc = jnp.dot(mixed, w_ref[...], preferred_element_type=F32)
    o_ref[...] = _layer_norm(alpha * h_ref[...] + acc, g_ref[...], b_ref[...])


def _out_proj_ln(h, ys, w_out, ln_g, ln_b, *, layer, alpha, tm):
    n, d = h.shape
    y_spec = pl.BlockSpec((tm, G_W), lambda i: (i, 0))
    ln_spec = pl.BlockSpec((None, None, 1, d), lambda i: (layer, 1, 0, 0))
    return pl.pallas_call(
        functools.partial(_out_proj_ln_kernel, alpha=alpha),
        out_shape=jax.ShapeDtypeStruct((n, d), F32),
        grid=(n // tm,),
        in_specs=[pl.BlockSpec((tm, d), lambda i: (i, 0)), y_spec, y_spec, y_spec, y_spec,
                  pl.BlockSpec((None, 4 * G_W, d), lambda i: (layer, 0, 0)), ln_spec, ln_spec],
        out_specs=pl.BlockSpec((tm, d), lambda i: (i, 0)),
        compiler_params=_params("parallel"),
        name="out_proj_ln",
    )(h, *ys, w_out, ln_g.reshape(ln_g.shape[0], ln_g.shape[1], 1, d), ln_b.reshape(ln_b.shape[0], ln_b.shape[1], 1, d))


def _pool_sgu_kernel(cur_ref, halo_ref, u_ref, v_ref, pw_ref, ps_ref, sw_ref, sbc_ref,
                     yp_ref, ys_ref, ext_ref, *, chunks_per_seq):
    c = pl.program_id(0) % chunks_per_seq
    cur = cur_ref[...]
    ext_ref[0:POOL_HALO, :] = jnp.where(c == 0, 0.0, halo_ref[...])
    ext_ref[POOL_HALO:, :] = cur
    rows = SGU_CHUNK
    pos = c * rows + lax.broadcasted_iota(jnp.int32, (rows, 1), 0)
    for gi, w in enumerate(POOL_WINDOWS):
        cols = slice(gi * HEAD_DIM, (gi + 1) * HEAD_DIM)
        s = cur[:, cols]
        for j in range(1, w):
            s = s + ext_ref[POOL_HALO - j:POOL_HALO - j + rows, cols]
        cnt = jnp.minimum(pos + 1, w).astype(F32)
        d = s / cnt - cur[:, cols]
        yp_ref[:, cols] = _bdot(d, pw_ref[gi]) * ps_ref[:, cols]

    r = lax.broadcasted_iota(jnp.int32, (rows, rows), 0)
    s_ = lax.broadcasted_iota(jnp.int32, (rows, rows), 1)
    causal = r >= s_
    for h in range(HEADS):
        cols = slice(h * HEAD_DIM, (h + 1) * HEAD_DIM)
        wm = jnp.where(causal, sw_ref[h], 0.0)
        mixed = _bdot(wm, v_ref[:, cols]) + sbc_ref[:, h:h + 1]
        ys_ref[:, cols] = u_ref[:, cols] * mixed


def _pool_sgu(proj, pool_w, pool_scale, sgu_w, sgu_b, *, seq):
    n = proj.shape[0]
    rows = SGU_CHUNK
    halo_blocks = rows // POOL_HALO
    blk = lambda col: pl.BlockSpec((rows, G_W), lambda g: (g, col))
    whole = lambda a: pl.BlockSpec(a.shape, lambda g: (0,) * a.ndim)
    ps = pool_scale.reshape(1, G_W)
    sbc = sgu_b.T
    return pl.pallas_call(
        functools.partial(_pool_sgu_kernel, chunks_per_seq=seq // rows),
        out_shape=(jax.ShapeDtypeStruct((n, G_W), F32), jax.ShapeDtypeStruct((n, G_W), F32)),
        grid=(n // rows,),
        in_specs=[blk(COL_POOL),
                  pl.BlockSpec((POOL_HALO, G_W), lambda g: (jnp.maximum(g * halo_blocks - 1, 0), COL_POOL)),
                  blk(COL_SGU_U), blk(COL_SGU_V), whole(pool_w), whole(ps), whole(sgu_w), whole(sbc)],
        out_specs=(pl.BlockSpec((rows, G_W), lambda g: (g, 0)), pl.BlockSpec((rows, G_W), lambda g: (g, 0))),
        scratch_shapes=[pltpu.VMEM((POOL_HALO + rows, G_W), F32)],
        compiler_params=_params("parallel"),
        name="pool_sgu",
    )(proj, proj, proj, proj, pool_w, ps, sgu_w, sbc)


def _sb_attn_kernel(bias_ref, q_ref, k_ref, v_ref, o_ref, *, blk, scale):
    qi = pl.program_id(1)
    heads = lambda x: jnp.stack([x[:, h * HEAD_DIM:(h + 1) * HEAD_DIM] for h in range(HEADS)])
    q = heads(q_ref[...]).astype(BF16)
    bias = bias_ref[...]
    row = lax.broadcasted_iota(jnp.int32, (blk, blk), 0)
    col = lax.broadcasted_iota(jnp.int32, (blk, blk), 1)
    after = (row > col).astype(BF16)

    def visit(j, carry, valid):
        o, run = carry
        start = pl.multiple_of(j * blk, blk)
        kb = heads(k_ref[pl.ds(start, blk), :]).astype(BF16)
        vb = heads(v_ref[pl.ds(start, blk), :]).astype(BF16)
        z = lax.dot_general(q, kb, _B_NT, preferred_element_type=F32) * scale + bias
        sp = jnp.maximum(z, 0.0) + jnp.log(1.0 + jnp.exp(-jnp.abs(z)))
        log_fail = -sp if valid is None else jnp.where(valid, -sp, 0.0)
        hi, lo = _split_bf16(log_fail.reshape(HEADS * blk, blk))
        later = (jnp.dot(hi, after, preferred_element_type=F32)
                 + jnp.dot(lo, after, preferred_element_type=F32)).reshape(HEADS, blk, blk) + run
        att = jnp.exp(z - sp + later)
        if valid is not None:
            att = jnp.where(valid, att, 0.0)
        o = o + lax.dot_general(att.astype(BF16), vb, _B_NN, preferred_element_type=F32)
        run = run + jnp.sum(log_fail, axis=-1, keepdims=True)
        return o, run

    init = (jnp.zeros((HEADS, blk, HEAD_DIM), F32), jnp.zeros((HEADS, blk, 1), F32))
    carry = visit(qi, init, (col < row)[None])
    o, _ = lax.fori_loop(0, qi, lambda step, carry: visit(qi - 1 - step, carry, None), carry)
    for h in range(HEADS):
        o_ref[:, h * HEAD_DIM:(h + 1) * HEAD_DIM] = o[h]


def _sb_attn(proj, sb_bias, *, batch, seq, blk):
    n = proj.shape[0]
    nq = seq // blk
    return pl.pallas_call(
        functools.partial(_sb_attn_kernel, blk=blk, scale=HEAD_DIM ** -0.5),
        out_shape=jax.ShapeDtypeStruct((n, G_W), F32),
        grid=(batch, nq),
        in_specs=[pl.BlockSpec((HEADS, 1, 1), lambda b, i: (0, 0, 0)),
                  pl.BlockSpec((blk, G_W), lambda b, i: (b * nq + i, COL_SB_Q)),
                  pl.BlockSpec((seq, G_W), lambda b, i: (b, COL_SB_K)),
                  pl.BlockSpec((seq, G_W), lambda b, i: (b, COL_SB_V))],
        out_specs=pl.BlockSpec((blk, G_W), lambda b, i: (b * nq + i, 0)),
        compiler_params=_params("parallel", "arbitrary"),
        name="sb_attn",
    )(sb_bias.reshape(HEADS, 1, 1), proj, proj, proj)


def _sb_decode_body(q_ref, bias_ref, cum_ref, k_refs, v_refs, o_ref, *, width, scale):
    n_pages = len(k_refs)
    rows = 8
    q = q_ref[0]
    q_rows = jnp.concatenate([q[:, h * HEAD_DIM:(h + 1) * HEAD_DIM] for h in range(HEADS)]
                             + [jnp.zeros((rows - HEADS, HEAD_DIM), F32)], axis=0).astype(BF16)
    rid = lax.broadcasted_iota(jnp.int32, (rows, width), 0)
    cid = lax.broadcasted_iota(jnp.int32, (rows, width), 1)
    own = (cid & (HEADS - 1)) == rid

    z_rows = []
    for p in range(n_pages):
        zf = _bdot_nt(q_rows, k_refs[p][...])
        z_rows.append(jnp.sum(jnp.where(own, zf, 0.0), axis=0, keepdims=True))
    z = jnp.concatenate(z_rows, axis=0) * scale + bias_ref[...]
    sp = _softplus(z)
    hi, lo = _split_bf16(-sp)
    cum = (jnp.dot(hi, cum_ref[...], preferred_element_type=F32)
           + jnp.dot(lo, cum_ref[...], preferred_element_type=F32))
    in_page, page_tot = cum[:, :width], cum[:, width:]
    pr = lax.broadcasted_iota(jnp.int32, (n_pages, n_pages), 0)
    pc = lax.broadcasted_iota(jnp.int32, (n_pages, n_pages), 1)
    later_pages = _hdot((pc > pr).astype(F32), page_tot)
    att = jnp.exp(z - sp + in_page + later_pages)

    acc = jnp.zeros((rows, HEAD_DIM), F32)
    for p in range(n_pages):
        a_p = jnp.where(own, jnp.broadcast_to(att[p:p + 1, :], (rows, width)), 0.0)
        acc = acc + _bdot(a_p, v_refs[p][...])
    for h in range(HEADS):
        o_ref[0, :, h * HEAD_DIM:(h + 1) * HEAD_DIM] = acc[h:h + 1, :]


def _same_head_sums(width):
    src = np.arange(width)[:, None]
    dst = np.arange(width)[None, :]
    same = (src - dst) % HEADS == 0
    return jnp.asarray(np.concatenate([same & (src > dst), same], axis=1), BF16)


def _ffn_ln_decode_kernel(pt_ref, x_ref, wgu_ref, wo_ref, g_ref, b_ref, q_ref, bias_ref, cum_ref, k_hbm, v_hbm,
                          o_ref, y_ref, xb_ref, kbuf, vbuf, sem, *, alpha, tf, n_pages, width, scale, seqs, total):
    i, f = pl.program_id(0), pl.program_id(1)

    @pl.when(f == 0)
    def _():
        xb_ref[...] = x_ref[...].astype(BF16)
        o_ref[...] = jnp.zeros_like(o_ref)

    gate_up = jnp.dot(xb_ref[...], wgu_ref[...], preferred_element_type=F32)
    act = (_silu(gate_up[:, :tf]) * gate_up[:, tf:]).astype(BF16)
    o_ref[...] += jnp.dot(act, wo_ref[...], preferred_element_type=F32)

    def page_copies(t):
        slot = lax.rem(t, 2)
        copies = []
        for p in range(n_pages):
            row0 = pl.multiple_of(pt_ref[t * n_pages + p] * width, width)
            copies.append(pltpu.make_async_copy(k_hbm.at[pl.ds(row0, width), :], kbuf.at[slot, p], sem.at[slot]))
            copies.append(pltpu.make_async_copy(v_hbm.at[pl.ds(row0, width), :], vbuf.at[slot, p], sem.at[slot]))
        return copies

    @pl.when(f < seqs)
    def _():
        t = i * seqs + f

        @pl.when(t == 0)
        def _():
            for c in page_copies(t):
                c.start()

        @pl.when(t + 1 < total)
        def _():
            for c in page_copies(t + 1):
                c.start()

        for c in page_copies(t):
            c.wait()
        slot = lax.rem(t, 2)
        _sb_decode_body(q_ref, bias_ref, cum_ref, [kbuf.at[slot, p] for p in range(n_pages)],
                        [vbuf.at[slot, p] for p in range(n_pages)], y_ref, width=width, scale=scale)

    @pl.when(f == pl.num_programs(1) - 1)
    def _():
        o_ref[...] = _layer_norm(alpha * x_ref[...] + 0.5 * o_ref[...], g_ref[...], b_ref[...])


def _ffn_ln_decode(x, weights, ln_g, ln_b, proj3, bias_row, cache_k, cache_v, phys_pages, *,
                   layer, which, alpha, tm, tf, n_pages, width):
    n, d = x.shape
    bs = proj3.shape[0]
    w_gate_up, w_out = weights
    nf = w_out.shape[0] // tf
    tiles = n // tm
    seqs = bs // tiles
    assert bs == seqs * tiles and seqs <= nf
    seq = lambda i, f: i * seqs + jnp.minimum(f, seqs - 1)
    ln_spec = pl.BlockSpec((None, None, 1, d), lambda i, f, pt: (layer, which, 0, 0))
    grid_spec = pltpu.PrefetchScalarGridSpec(
        num_scalar_prefetch=1,
        grid=(tiles, nf),
        in_specs=[pl.BlockSpec((tm, d), lambda i, f, pt: (i, 0)),
                  pl.BlockSpec((d, 2 * tf), lambda i, f, pt: (0, f)),
                  pl.BlockSpec((tf, d), lambda i, f, pt: (f, 0)),
                  ln_spec, ln_spec,
                  pl.BlockSpec((1, 1, G_W), lambda i, f, pt: (seq(i, f), 0, COL_SB_Q)),
                  pl.BlockSpec((1, width), lambda i, f, pt: (0, 0)),
                  pl.BlockSpec((width, 2 * width), lambda i, f, pt: (0, 0), pipeline_mode=pl.Buffered(1)),
                  pl.BlockSpec(memory_space=pl.ANY), pl.BlockSpec(memory_space=pl.ANY)],
        out_specs=(pl.BlockSpec((tm, d), lambda i, f, pt: (i, 0)),
                   pl.BlockSpec((1, 1, G_W), lambda i, f, pt: (seq(i, f), 0, 0))),
        scratch_shapes=[pltpu.VMEM((tm, d), BF16),
                        pltpu.VMEM((2, n_pages, width, HEAD_DIM), F32),
                        pltpu.VMEM((2, n_pages, width, HEAD_DIM), F32),
                        pltpu.SemaphoreType.DMA((2,))],
    )
    out, y_sb = pl.pallas_call(
        functools.partial(_ffn_ln_decode_kernel, alpha=alpha, tf=tf, n_pages=n_pages, width=width,
                          scale=HEAD_DIM ** -0.5, seqs=seqs, total=bs),
        out_shape=(jax.ShapeDtypeStruct((n, d), F32), jax.ShapeDtypeStruct((bs, 1, G_W), F32)),
        grid_spec=grid_spec,
        compiler_params=pltpu.CompilerParams(dimension_semantics=("arbitrary", "arbitrary"),
                                             vmem_limit_bytes=FUSED_VMEM_LIMIT_BYTES),
        name="ffn_ln_decode",
    )(phys_pages, x, w_gate_up, w_out,
      ln_g.reshape(ln_g.shape[0], ln_g.shape[1], 1, d), ln_b.reshape(ln_b.shape[0], ln_b.shape[1], 1, d),
      proj3, bias_row, _same_head_sums(width), cache_k, cache_v)
    return out, y_sb.reshape(bs, G_W)


def _beta_and_log_decay(raw, neg_a_row, dt_row):
    lane = lax.broadcasted_iota(jnp.int32, raw.shape, 1)
    beta = jax.nn.sigmoid(raw)
    g = neg_a_row * _softplus(raw + dt_row)
    return jnp.where(lane < HEADS, beta, g)


def _l2_normalize(x):
    return x * lax.rsqrt(jnp.sum(x * x, -1, keepdims=True) + NORM_EPS)


def _dn_prep_kernel(raw_ref, halo_ref, bgr_ref, cw_ref, na_ref, dt_ref,
                    q_ref, k_ref, v_ref, bg_ref, bgt_ref, ext_ref, *, tiles_per_seq, rows):
    first = (pl.program_id(0) % tiles_per_seq) == 0
    ext_ref[0:CONV_HALO, :] = jnp.where(first, 0.0, halo_ref[...])
    ext_ref[CONV_HALO:, :] = raw_ref[...]
    conv = None
    for j in range(CONV_W):
        off = CONV_HALO - (CONV_W - 1) + j
        term = ext_ref[off:off + rows, :] * cw_ref[j:j + 1, :]
        conv = term if conv is None else conv + term
    act = _silu(conv)
    for h in range(HEADS):
        cols = slice(h * HEAD_DIM, (h + 1) * HEAD_DIM)
        q_ref[:, cols] = _l2_normalize(act[:, h * HEAD_DIM:(h + 1) * HEAD_DIM]) * (HEAD_DIM ** -0.5)
        k_ref[:, cols] = _l2_normalize(act[:, G_W + h * HEAD_DIM:G_W + (h + 1) * HEAD_DIM])
    v_ref[...] = act[:, 2 * G_W:]
    bg = _beta_and_log_decay(bgr_ref[...], na_ref[...], dt_ref[...])
    r = lax.broadcasted_iota(jnp.int32, (rows, rows), 0)
    s = lax.broadcasted_iota(jnp.int32, (rows, rows), 1)
    shift = DN_CHUNK.bit_length() - 1
    chunk_prefix = ((r >= s) & (lax.shift_right_logical(r, shift) == lax.shift_right_logical(s, shift)))
    lane = lax.broadcasted_iota(jnp.int32, bg.shape, 1)
    bg = jnp.where(lane < HEADS, bg, _hdot(chunk_prefix.astype(F32), bg))
    bg_ref[...] = bg
    per_tile = BG_LANES // DN_CHUNK
    for t in range(rows // BG_LANES):
        tile_t = bg[t * BG_LANES:(t + 1) * BG_LANES, :].T
        for c in range(per_tile):
            bgt_ref[t * per_tile + c] = tile_t[0:2 * HEADS, c * DN_CHUNK:(c + 1) * DN_CHUNK]


def _dn_prep(proj, bg_raw, conv_w, neg_a_row, dt_row, *, seq, rows):
    n = proj.shape[0]
    halo_blocks = rows // CONV_HALO
    qkv_w = 3 * G_W
    col = COL_DN_QKV * G_W // qkv_w
    whole = lambda a: pl.BlockSpec(a.shape, lambda g: (0,) * a.ndim)
    out = jax.ShapeDtypeStruct((n, G_W), F32)
    o_spec = pl.BlockSpec((rows, G_W), lambda g: (g, 0))
    return pl.pallas_call(
        functools.partial(_dn_prep_kernel, tiles_per_seq=seq // rows, rows=rows),
        out_shape=(out, out, out, jax.ShapeDtypeStruct((n, BG_LANES), F32),
                   jax.ShapeDtypeStruct((n // DN_CHUNK, 2 * HEADS, DN_CHUNK), F32)),
        grid=(n // rows,),
        in_specs=[pl.BlockSpec((rows, qkv_w), lambda g: (g, col)),
                  pl.BlockSpec((CONV_HALO, qkv_w), lambda g: (jnp.maximum(g * halo_blocks - 1, 0), col)),
                  pl.BlockSpec((rows, BG_LANES), lambda g: (g, 0)),
                  whole(conv_w), whole(neg_a_row), whole(dt_row)],
        out_specs=(o_spec, o_spec, o_spec, pl.BlockSpec((rows, BG_LANES), lambda g: (g, 0)),
                   pl.BlockSpec((rows // DN_CHUNK, 2 * HEADS, DN_CHUNK), lambda g: (g, 0, 0))),
        scratch_shapes=[pltpu.VMEM((CONV_HALO + rows, qkv_w), F32)],
        compiler_params=_params("parallel"),
        name="dn_prep",
    )(proj, proj, bg_raw, conv_w, neg_a_row, dt_row)


def _cat_lhs(x, axis):
    hi = x.astype(BF16).astype(F32)
    return jnp.concatenate([hi, x - hi, hi], axis=axis).astype(BF16)


def _cat_rhs(x, axis):
    hi = x.astype(BF16)
    lo = (x - hi.astype(F32)).astype(BF16)
    return jnp.concatenate([hi, hi, lo], axis=axis)


def _mm(lhs_cat, rhs_cat, dims=_B_NN):
    return lax.dot_general(lhs_cat, rhs_cat, dims, preferred_element_type=F32)


def _unit_lower_inverse(a_mat, eye):
    n = a_mat.shape[-1]
    inv = eye - a_mat
    power = _mm(_cat_lhs(a_mat, 2), _cat_rhs(a_mat, 1))
    span = 2
    while span < n:
        p_rhs = _cat_rhs(power, 1)
        inv = inv + _mm(_cat_lhs(inv, 2), p_rhs)
        span *= 2
        if span < n:
            power = _mm(_cat_lhs(power, 2), p_rhs)
    return inv


def _dn_out(o, z, norm_g):
    o = o * lax.rsqrt(jnp.mean(o * o, -1, keepdims=True) + NORM_EPS) * norm_g
    return o * _silu(z)


def _dn_scan_kernel(q_ref, k_ref, v_ref, z_ref, bg_ref, bgt_ref, ng_ref, y_ref, s_out_ref, s_ref, *, chunks):
    t = pl.program_id(1)

    @pl.when(t == 0)
    def _():
        s_ref[...] = jnp.zeros_like(s_ref)

    c, d = DN_CHUNK, HEAD_DIM
    units = [(ci, h) for ci in range(chunks) for h in range(HEADS)]
    rows = lambda ci: slice(ci * c, (ci + 1) * c)
    cols = lambda h: slice(h * d, (h + 1) * d)
    gather = lambda ref: jnp.stack([ref[rows(ci), cols(h)] for ci, h in units])
    q, k, v = gather(q_ref), gather(k_ref), gather(v_ref)
    beta = jnp.stack([bg_ref[rows(ci), h:h + 1] for ci, h in units])
    g_col = jnp.stack([bg_ref[rows(ci), HEADS + h:HEADS + h + 1] for ci, h in units])
    g_row = jnp.stack([bgt_ref[ci, HEADS + h:HEADS + h + 1, :] for ci, h in units])

    r = lax.broadcasted_iota(jnp.int32, (c, c), 0)
    s_ = lax.broadcasted_iota(jnp.int32, (c, c), 1)
    incl, strict, eye = (r >= s_)[None], (r > s_)[None], (r == s_).astype(F32)[None]
    decay = jnp.where(incl, jnp.exp(jnp.where(incl, g_col - g_row, 0.0)), 0.0)
    qk_hi, qk_lo = _split_bf16(jnp.concatenate([q, k], axis=1))
    k_hi, k_lo = _split_bf16(k)
    qk_kk = (_mm(jnp.concatenate([qk_hi, qk_lo], axis=2), jnp.concatenate([k_hi, k_hi], axis=2), _B_NT)
             + _mm(qk_hi, k_lo, _B_NT))
    qk = qk_kk[:, :c] * decay
    a_mat = jnp.where(strict, beta * qk_kk[:, c:] * decay, 0.0)
    inv = _unit_lower_inverse(a_mat, eye)
    e_g = jnp.exp(g_col)
    rhs = jnp.concatenate([beta * v, (beta * e_g) * k], axis=2)
    sol = _mm(_cat_lhs(inv, 2), _cat_rhs(rhs, 1))
    u_base, w_mat = sol[:, :, :d], sol[:, :, d:]
    w_hi, w_lo = _split_bf16(w_mat)
    w_cat = jnp.concatenate([w_hi, w_lo], axis=2)
    q_dec = (q * e_g).astype(BF16)
    g_last = g_col[:, c - 1:c, :]
    k_dec = k * jnp.exp(g_last - g_col)
    keep = jnp.exp(g_last)

    norm_g = ng_ref[...]
    s = s_ref[...]
    for ci in range(chunks):
        sl = slice(ci * HEADS, (ci + 1) * HEADS)
        s_hi, s_lo = _split_bf16(s)
        u = u_base[sl] - (_mm(w_cat[sl], jnp.concatenate([s_hi, s_hi], axis=1)) + _mm(w_hi[sl], s_lo))
        o = _mm(q_dec[sl], s_hi) + _mm(qk[sl].astype(BF16), u.astype(BF16))
        s = s * keep[sl] + _mm(_cat_lhs(k_dec[sl], 1), _cat_rhs(u, 1), _B_TN)
        for h in range(HEADS):
            y_ref[rows(ci), cols(h)] = _dn_out(o[h], z_ref[rows(ci), cols(h)], norm_g)
    s_ref[...] = s

    @pl.when(t == pl.num_programs(1) - 1)
    def _():
        s_out_ref[0] = s_ref[...]


def _dn_scan(q, k, v, proj, bg, bgt, norm_g, *, batch, seq, chunks):
    n = q.shape[0]
    rows = chunks * DN_CHUNK
    steps = seq // rows
    spec = pl.BlockSpec((rows, G_W), lambda b, t: (b * steps + t, 0))
    ng = norm_g.reshape(1, HEAD_DIM)
    return pl.pallas_call(
        functools.partial(_dn_scan_kernel, chunks=chunks),
        out_shape=(jax.ShapeDtypeStruct((n, G_W), F32),
                   jax.ShapeDtypeStruct((batch, HEADS, HEAD_DIM, HEAD_DIM), F32)),
        grid=(batch, steps),
        in_specs=[spec, spec, spec,
                  pl.BlockSpec((rows, G_W), lambda b, t: (b * steps + t, COL_DN_Z)),
                  pl.BlockSpec((rows, BG_LANES), lambda b, t: (b * steps + t, 0)),
                  pl.BlockSpec((chunks, 2 * HEADS, DN_CHUNK), lambda b, t: (b * steps + t, 0, 0)),
                  pl.BlockSpec((1, HEAD_DIM), lambda b, t: (0, 0))],
        out_specs=(spec, pl.BlockSpec((1, HEADS, HEAD_DIM, HEAD_DIM), lambda b, t: (b, 0, 0, 0))),
        scratch_shapes=[pltpu.VMEM((HEADS, HEAD_DIM, HEAD_DIM), F32)],
        compiler_params=_params("parallel", "arbitrary"),
        name="dn_scan",
    )(q, k, v, proj, bg, bgt, ng)


def _sample_mix_kernel(pool_ref, u_ref, v_ref, qkv_ref, z_ref, bgr_ref, sp_ref, sc_ref, sd_ref,
                       pw_ref, ps_ref, sw0_ref, sb0_ref, cw_ref, na_ref, dt_ref, ng_ref, *rest, bt, pos0):
    yp_ref, ys_ref, yd_ref, np_ref, nc_ref, nd_ref = rest[-6:]
    new = pool_ref[...]
    for gi, w in enumerate(POOL_WINDOWS):
        cols = slice(gi * HEAD_DIM, (gi + 1) * HEAD_DIM)
        s = new[:, cols]
        for j in range(1, w):
            s = s + sp_ref[POOL_BUF - j, :, cols]
        d = s / float(min(pos0 + 1, w)) - new[:, cols]
        yp_ref[:, cols] = _hdot(d, pw_ref[gi]) * ps_ref[:, cols]
    np_ref[0:POOL_BUF - 1] = sp_ref[1:POOL_BUF]
    np_ref[POOL_BUF - 1] = new

    ys_ref[...] = u_ref[...] * (sw0_ref[...] * v_ref[...] + sb0_ref[...])

    raw = qkv_ref[...]
    conv = raw * cw_ref[CONV_W - 1:CONV_W, :]
    for j in range(CONV_W - 1):
        conv = conv + sc_ref[j] * cw_ref[j:j + 1, :]
    nc_ref[0:CONV_W - 2] = sc_ref[1:CONV_W - 1]
    nc_ref[CONV_W - 2] = raw
    act = _silu(conv)
    bg = _beta_and_log_decay(bgr_ref[...], na_ref[...], dt_ref[...])
    norm_g = ng_ref[...]
    d = HEAD_DIM
    pad = jnp.zeros((d - bt, d), F32)
    k_t, q_t = [], []
    for h in range(HEADS):
        k_t.append(jnp.concatenate([_l2_normalize(act[:, G_W + h * d:G_W + (h + 1) * d]), pad], axis=0).T)
        q_t.append(jnp.concatenate([_l2_normalize(act[:, h * d:(h + 1) * d]) * (d ** -0.5), pad], axis=0).T)
    pairs = [(b, h) for b in range(bt) for h in range(HEADS)]
    v_h = [act[:, 2 * G_W + h * d:2 * G_W + (h + 1) * d] for h in range(HEADS)]
    z_h = [z_ref[:, h * d:(h + 1) * d] for h in range(HEADS)]
    a = [jnp.exp(bg[b:b + 1, HEADS + h:HEADS + h + 1]) for b, h in pairs]
    k_col = [jnp.broadcast_to(k_t[h][:, b:b + 1], (d, d)) for b, h in pairs]
    k_s = [jnp.sum(k_col[i] * sd_ref[b, h], axis=0, keepdims=True) for i, (b, h) in enumerate(pairs)]
    u = [bg[b:b + 1, h:h + 1] * (v_h[h][b:b + 1, :] - a[i] * k_s[i]) for i, (b, h) in enumerate(pairs)]
    s1 = [a[i] * sd_ref[b, h] + k_col[i] * u[i] for i, (b, h) in enumerate(pairs)]
    for i, (b, h) in enumerate(pairs):
        nd_ref[b, h] = s1[i]
    o = [jnp.sum(jnp.broadcast_to(q_t[h][:, b:b + 1], (d, d)) * s1[i], axis=0, keepdims=True)
         for i, (b, h) in enumerate(pairs)]
    for i, (b, h) in enumerate(pairs):
        yd_ref[b:b + 1, h * d:(h + 1) * d] = _dn_out(o[i], z_h[h][b:b + 1, :], norm_g)


def _sample_mix(proj, bg_raw, state_pool, state_conv, state_delta, new_prev, pool_w, pool_scale, sgu_w0, sgu_b0,
                conv_w, neg_a_row, dt_row, norm_g, *, layer, bt, pos0):
    bs = proj.shape[0]
    qkv_w = 3 * G_W
    col = lambda c: pl.BlockSpec((bt, G_W), lambda i: (i, c))
    whole = lambda a: pl.BlockSpec(a.shape, lambda i: (0,) * a.ndim)
    row = pl.BlockSpec((bt, G_W), lambda i: (i, 0))
    ps = pool_scale.reshape(1, G_W)
    ng = norm_g.reshape(1, HEAD_DIM)
    consts = (pool_w, ps, sgu_w0, sgu_b0, conv_w, neg_a_row, dt_row, ng)
    out_sds = lambda a: jax.ShapeDtypeStruct(a.shape, F32)
    pool_spec = pl.BlockSpec((None, POOL_BUF, bt, G_W), lambda i: (layer, 0, i, 0))
    conv_spec = pl.BlockSpec((None, CONV_W - 1, bt, qkv_w), lambda i: (layer, 0, i, 0))
    delta_spec = pl.BlockSpec((None, bt, HEADS, HEAD_DIM, HEAD_DIM), lambda i: (layer, i, 0, 0, 0))
    y = jax.ShapeDtypeStruct((bs, G_W), F32)
    prev = () if new_prev is None else tuple(new_prev)
    n_in = 9 + len(consts)
    return pl.pallas_call(
        functools.partial(_sample_mix_kernel, bt=bt, pos0=pos0),
        out_shape=(y, y, y, out_sds(state_pool), out_sds(state_conv), out_sds(state_delta)),
        grid=(bs // bt,),
        in_specs=[col(COL_POOL), col(COL_SGU_U), col(COL_SGU_V),
                  pl.BlockSpec((bt, qkv_w), lambda i: (i, COL_DN_QKV * G_W // qkv_w)),
                  col(COL_DN_Z),
                  pl.BlockSpec((bt, BG_LANES), lambda i: (i, 0)),
                  pool_spec, conv_spec, delta_spec] + [whole(a) for a in consts]
                 + [pl.BlockSpec(memory_space=pl.ANY)] * len(prev),
        out_specs=(row, row, row, pool_spec, conv_spec, delta_spec),
        input_output_aliases={n_in + k: 3 + k for k in range(len(prev))},
        compiler_params=_params("parallel"),
        name="sample_mix",
    )(proj, proj, proj, proj, proj, bg_raw, state_pool, state_conv, state_delta, *consts, *prev)


def _lane_row(values, offset):
    return jnp.zeros((1, BG_LANES), F32).at[0, offset:offset + values.shape[0]].set(values)


def kernel(x_prompt, x_sample, cache_k, cache_v, page_table, state_pool, state_conv, state_delta,
           ln_g, ln_b, w_ffn1_in, w_ffn1_out, w_ffn2_in, w_ffn2_out, w_in, w_out,
           pool_w, pool_scale, sgu_w, sgu_b, sb_bias, dn_conv_w, dn_a_log, dn_dt_bias, dn_norm_g):
    depth = ln_g.shape[0]
    alpha = (2.0 * depth) ** 0.25
    bp, seq, d_model = x_prompt.shape
    bs, dec_seq, _ = x_sample.shape
    assert dec_seq == 1
    n_phys, page = cache_k.shape[1], cache_k.shape[2]
    n_pages = page_table.shape[1]
    pos0 = n_pages * page
    ck = cache_k.reshape(depth * n_phys * page * HEADS, HEAD_DIM)
    cv = cache_v.reshape(depth * n_phys * page * HEADS, HEAD_DIM)
    hp = x_prompt.reshape(bp * seq, d_model)
    hs = x_sample.reshape(bs, d_model)
    pool_tm = jnp.swapaxes(state_pool, 1, 2)
    conv_tm = jnp.swapaxes(state_conv, 1, 2)
    tm_p = 512
    tm_s = bs
    tail_w = w_in.shape[2] - N_MAIN_COLS
    w_in_tail = jnp.pad(w_in[:, :, N_MAIN_COLS:].astype(BF16), ((0, 0), (0, 0), (0, BG_LANES - tail_w)))
    w_in_t = jnp.swapaxes(w_in, 1, 2)
    w_out_bf = w_out.astype(BF16)

    outs = {name: [] for name in ("poolp", "convp", "sp", "ks", "vs", "sgus")}
    kv_rows = None
    new_states = None
    for l in range(depth):
        neg_a_row = _lane_row(-jnp.exp(dn_a_log[l]), HEADS)
        dt_row = _lane_row(dn_dt_bias[l], HEADS)
        ffn = functools.partial(_ffn_ln, ln_g=ln_g, ln_b=ln_b, layer=l, alpha=alpha, tf=512)
        out_proj = functools.partial(_out_proj_ln, w_out=w_out_bf, ln_g=ln_g, ln_b=ln_b, layer=l, alpha=alpha)

        g1, *w1 = ffn(hs, (w_ffn1_in, w_ffn1_out), which=0, tm=tm_s)
        sproj, sbg_raw, sk_rows, sv_rows, wi_main = _in_proj_cast(g1, w_in_t, w_in_tail, layer=l)
        sgu_w0 = jnp.repeat(sgu_w[l, :, 0, 0], HEAD_DIM).reshape(1, G_W)
        sgu_b0 = jnp.repeat(sgu_b[l, :, 0], HEAD_DIM).reshape(1, G_W)
        sy_pool, sy_sgu, sy_dn, *new_states = _sample_mix(
            sproj, sbg_raw, pool_tm, conv_tm, state_delta, new_states, pool_w[l], pool_scale[l],
            sgu_w0, sgu_b0, dn_conv_w[l], neg_a_row, dt_row, dn_norm_g[l], layer=l, bt=8, pos0=pos0)
        bias_row = jnp.tile(sb_bias[l], page).reshape(1, page * HEADS)
        phys = (page_table + l * n_phys).reshape(-1)
        h1, sy_sb = _ffn_ln_decode(hp, tuple(w1), ln_g, ln_b, sproj.reshape(bs, 1, N_MAIN_COLS), bias_row, ck, cv, phys,
                                   layer=l, which=0, alpha=alpha, tm=tm_p, tf=512,
                                   n_pages=n_pages, width=page * HEADS)
        g2 = out_proj(g1, (sy_pool, sy_sgu, sy_sb, sy_dn), tm=tm_s)
        hs, *w2 = ffn(g2, (w_ffn2_in, w_ffn2_out), which=2, tm=tm_s)
        outs["ks"].append(sk_rows.reshape(bs, 1, HEADS, HEAD_DIM))
        outs["vs"].append(sv_rows.reshape(bs, 1, HEADS, HEAD_DIM))
        outs["sgus"].append(sproj[:, COL_SGU_V * G_W:(COL_SGU_V + 1) * G_W].reshape(bs, 1, G_W))

        proj, bg_raw, *kv_rows = _in_proj(h1, wi_main, w_in_tail, kv_rows, layer=l, depth=depth, tm=256)
        y_pool, y_sgu = _pool_sgu(proj, pool_w[l], pool_scale[l], sgu_w[l], sgu_b[l], seq=seq)
        y_sb = _sb_attn(proj, sb_bias[l], batch=bp, seq=seq, blk=256)
        dq, dk, dv, bg, bgt = _dn_prep(proj, bg_raw, dn_conv_w[l], neg_a_row, dt_row, seq=seq, rows=256)
        y_dn, s_end = _dn_scan(dq, dk, dv, proj, bg, bgt, dn_norm_g[l], batch=bp, seq=seq, chunks=4)
        h2 = out_proj(h1, (y_pool, y_sgu, y_sb, y_dn), tm=256)
        hp, = ffn(h2, tuple(w2), which=2, tm=tm_p)
        proj3 = proj.reshape(bp, seq, N_MAIN_COLS)
        outs["poolp"].append(proj3[:, seq - POOL_BUF:, :G_W])
        outs["convp"].append(proj3[:, seq - (CONV_W - 1):, COL_DN_QKV * G_W:(COL_DN_QKV + 3) * G_W])
        outs["sp"].append(s_end)

    st = lambda name: jnp.stack(outs[name])
    new_pool, new_conv, new_delta = new_states
    kp, vp = (a.reshape(depth, bp, seq, HEADS, HEAD_DIM) for a in kv_rows)
    return (hp.reshape(bp, seq, d_model), hs.reshape(bs, 1, d_model),
            kp, vp, st("poolp"), st("convp"), st("sp"),
            st("ks"), st("vs"), jnp.swapaxes(new_pool, 1, 2), jnp.swapaxes(new_conv, 1, 2),
            new_delta, st("sgus"))
```

```python
import functools

import jax
import jax.numpy as jnp
import numpy as np
from jax import lax
from jax.experimental import pallas as pl
from jax.experimental.pallas import tpu as pltpu

F32 = jnp.float32
BF16 = jnp.bfloat16
HIGHEST = lax.Precision.HIGHEST

HEADS = 4
HEAD_DIM = 128
G_W = HEADS * HEAD_DIM
POOL_WINDOWS = (2, 4, 8, 16)
POOL_BUF = max(POOL_WINDOWS) - 1
POOL_HALO = 16
SGU_CHUNK = 128
DN_CHUNK = 64
CONV_W = 4
CONV_HALO = 8
LN_EPS = 1e-5
NORM_EPS = 1e-6
VMEM_LIMIT_BYTES = 48 * 1024 * 1024
FUSED_VMEM_LIMIT_BYTES = 58 * 1024 * 1024

COL_POOL, COL_SGU_U, COL_SGU_V, COL_SB_Q, COL_SB_K, COL_SB_V, COL_DN_QKV, COL_DN_Z = 0, 1, 2, 3, 4, 5, 6, 9
N_MAIN_COLS = 10 * G_W
BG_LANES = 128


def _params(*sem):
    return pltpu.CompilerParams(dimension_semantics=sem, vmem_limit_bytes=VMEM_LIMIT_BYTES)


def _hdot(a, b):
    return jnp.dot(a, b, precision=HIGHEST, preferred_element_type=F32)


def _bdot(a, b):
    return jnp.dot(a.astype(BF16), b.astype(BF16), preferred_element_type=F32)


def _bdot_nt(a, b):
    return lax.dot_general(a.astype(BF16), b.astype(BF16), (((1,), (1,)), ((), ())),
                           preferred_element_type=F32)


def _silu(x):
    return x * jax.nn.sigmoid(x)


def _softplus(x):
    return jnp.maximum(x, 0.0) + jnp.log1p(jnp.exp(-jnp.abs(x)))


def _layer_norm(y, g, b):
    mu = jnp.mean(y, -1, keepdims=True)
    d = y - mu
    var = jnp.mean(d * d, -1, keepdims=True)
    return d * lax.rsqrt(var + LN_EPS) * g + b


def _split_bf16(x):
    hi = x.astype(BF16)
    lo = (x - hi.astype(F32)).astype(BF16)
    return hi, lo


_B_NN = (((2,), (1,)), ((0,), (0,)))
_B_NT = (((2,), (2,)), ((0,), (0,)))
_B_TN = (((1,), (1,)), ((0,), (0,)))


def _ffn_ln_kernel(x_ref, *refs, alpha, emit_bf16, tf):
    if emit_bf16:
        wg_ref, wu_ref, wo_ref, g_ref, b_ref, o_ref, wgu_out, wo_out, xb_ref, acc_ref = refs
    else:
        wgu_ref, wo_ref, g_ref, b_ref, o_ref, xb_ref, acc_ref = refs
    f = pl.program_id(1)

    @pl.when(f == 0)
    def _():
        xb_ref[...] = x_ref[...].astype(BF16)
        acc_ref[...] = jnp.zeros_like(acc_ref)

    xb = xb_ref[...]
    if emit_bf16:
        wg, wu, wo = wg_ref[...].astype(BF16), wu_ref[...].astype(BF16), wo_ref[...].astype(BF16)
        wgu_out[:, :tf], wgu_out[:, tf:], wo_out[...] = wg, wu, wo
        gate = jnp.dot(xb, wg, preferred_element_type=F32)
        up = jnp.dot(xb, wu, preferred_element_type=F32)
    else:
        wo = wo_ref[...]
        gate_up = jnp.dot(xb, wgu_ref[...], preferred_element_type=F32)
        gate, up = gate_up[:, :tf], gate_up[:, tf:]
    act = (_silu(gate) * up).astype(BF16)
    acc_ref[...] += jnp.dot(act, wo, preferred_element_type=F32)

    @pl.when(f == pl.num_programs(1) - 1)
    def _():
        o_ref[...] = _layer_norm(alpha * x_ref[...] + 0.5 * acc_ref[...], g_ref[...], b_ref[...])


def _ffn_ln(x, weights, ln_g, ln_b, *, layer, which, alpha, tm, tf):
    n, d = x.shape
    emit_bf16 = weights[0].dtype == F32
    if emit_bf16:
        w_in, w_out = weights
        dff = w_out.shape[1]
        nf = dff // tf
        operands = (w_in, w_in, w_out)
        w_specs = [pl.BlockSpec((None, d, tf), lambda i, f: (layer, 0, f)),
                   pl.BlockSpec((None, d, tf), lambda i, f: (layer, 0, f + nf)),
                   pl.BlockSpec((None, tf, d), lambda i, f: (layer, f, 0))]
    else:
        operands = weights
        dff = weights[1].shape[0]
        nf = dff // tf
        w_specs = [pl.BlockSpec((d, 2 * tf), lambda i, f: (0, f)),
                   pl.BlockSpec((tf, d), lambda i, f: (f, 0))]
    ln_spec = pl.BlockSpec((None, None, 1, d), lambda i, f: (layer, which, 0, 0))
    out_shape = [jax.ShapeDtypeStruct((n, d), F32)]
    out_specs = [pl.BlockSpec((tm, d), lambda i, f: (i, 0))]
    if emit_bf16:
        assert n == tm
        out_shape += [jax.ShapeDtypeStruct((d, 2 * dff), BF16), jax.ShapeDtypeStruct((dff, d), BF16)]
        out_specs += [pl.BlockSpec((d, 2 * tf), lambda i, f: (0, f)), pl.BlockSpec((tf, d), lambda i, f: (f, 0))]
    return pl.pallas_call(
        functools.partial(_ffn_ln_kernel, alpha=alpha, emit_bf16=emit_bf16, tf=tf),
        out_shape=tuple(out_shape),
        grid=(n // tm, nf),
        in_specs=[pl.BlockSpec((tm, d), lambda i, f: (i, 0))] + w_specs + [ln_spec, ln_spec],
        out_specs=tuple(out_specs),
        scratch_shapes=[pltpu.VMEM((tm, d), BF16), pltpu.VMEM((tm, d), F32)],
        compiler_params=_params("parallel", "arbitrary"),
        name="ffn_ln_cast" if emit_bf16 else "ffn_ln",
    )(x, *operands, ln_g.reshape(ln_g.shape[0], ln_g.shape[1], 1, d), ln_b.reshape(ln_b.shape[0], ln_b.shape[1], 1, d))


def _store_head_rows(dst_ref, block, tm):
    for h in range(HEADS):
        dst_ref[pl.ds(h, tm, stride=HEADS), :] = block[:, h * HEAD_DIM:(h + 1) * HEAD_DIM]


def _in_proj_kernel(*refs, tm, aliased):
    h_ref, w_ref, wt_ref = refs[:3]
    o_ref, bg_ref, k_ref, v_ref = refs[3 + aliased:]
    hb = h_ref[...].astype(BF16)
    bg_ref[...] = jnp.dot(hb, wt_ref[...], preferred_element_type=F32)
    o = jnp.dot(hb, w_ref[...], preferred_element_type=F32)
    o_ref[...] = o
    _store_head_rows(k_ref, o[:, COL_SB_K * G_W:(COL_SB_K + 1) * G_W], tm)
    _store_head_rows(v_ref, o[:, COL_SB_V * G_W:(COL_SB_V + 1) * G_W], tm)


def _in_proj(h, w_main, w_tail, kv_prev, *, layer, depth, tm):
    n, d = h.shape
    steps = n // tm
    resident = pl.Buffered(1)
    rows_kv = jax.ShapeDtypeStruct((depth * n * HEADS, HEAD_DIM), F32)
    kv_spec = pl.BlockSpec((tm * HEADS, HEAD_DIM), lambda i: (layer * steps + i, 0))
    aliased = 0 if kv_prev is None else 2
    prev = () if kv_prev is None else tuple(kv_prev)
    return pl.pallas_call(
        functools.partial(_in_proj_kernel, tm=tm, aliased=aliased),
        out_shape=(jax.ShapeDtypeStruct((n, N_MAIN_COLS), F32), jax.ShapeDtypeStruct((n, BG_LANES), F32),
                   rows_kv, rows_kv),
        grid=(steps,),
        in_specs=[
            pl.BlockSpec((tm, d), lambda i: (i, 0)),
            pl.BlockSpec((d, N_MAIN_COLS), lambda i: (0, 0), pipeline_mode=resident),
            pl.BlockSpec((None, d, BG_LANES), lambda i: (layer, 0, 0), pipeline_mode=resident),
        ] + [pl.BlockSpec(memory_space=pl.ANY)] * aliased,
        out_specs=(pl.BlockSpec((tm, N_MAIN_COLS), lambda i: (i, 0)),
                   pl.BlockSpec((tm, BG_LANES), lambda i: (i, 0)), kv_spec, kv_spec),
        input_output_aliases={3: 2, 4: 3} if aliased else {},
        compiler_params=_params("parallel"),
        name="in_proj",
    )(h, w_main, w_tail, *prev)


def _in_proj_cast_kernel(h_ref, w_ref, wt_ref, o_ref, bg_ref, k_ref, v_ref, wb_ref, *, tm):
    j = pl.program_id(0)
    hb = h_ref[...].astype(BF16)
    wb = w_ref[...].T.astype(BF16)
    wb_ref[...] = wb
    o = jnp.dot(hb, wb, preferred_element_type=F32)
    o_ref[...] = o

    @pl.when(j == 0)
    def _():
        bg_ref[...] = jnp.dot(hb, wt_ref[...], preferred_element_type=F32)

    @pl.when(j == COL_SB_K)
    def _():
        _store_head_rows(k_ref, o, tm)

    @pl.when(j == COL_SB_V)
    def _():
        _store_head_rows(v_ref, o, tm)


def _in_proj_cast(h, w_in_t, w_tail, *, layer):
    n, d = h.shape
    rows_kv = jax.ShapeDtypeStruct((n * HEADS, HEAD_DIM), F32)
    kv_spec = pl.BlockSpec((n * HEADS, HEAD_DIM), lambda j: (0, 0))
    return pl.pallas_call(
        functools.partial(_in_proj_cast_kernel, tm=n),
        out_shape=(jax.ShapeDtypeStruct((n, N_MAIN_COLS), F32), jax.ShapeDtypeStruct((n, BG_LANES), F32),
                   rows_kv, rows_kv, jax.ShapeDtypeStruct((d, N_MAIN_COLS), BF16)),
        grid=(N_MAIN_COLS // G_W,),
        in_specs=[
            pl.BlockSpec((n, d), lambda j: (0, 0)),
            pl.BlockSpec((None, G_W, d), lambda j: (layer, j, 0)),
            pl.BlockSpec((None, d, BG_LANES), lambda j: (layer, 0, 0)),
        ],
        out_specs=(pl.BlockSpec((n, G_W), lambda j: (0, j)),
                   pl.BlockSpec((n, BG_LANES), lambda j: (0, 0)), kv_spec, kv_spec,
                   pl.BlockSpec((d, G_W), lambda j: (0, j))),
        compiler_params=_params("arbitrary"),
        name="in_proj_cast",
    )(h, w_in_t, w_tail)


def _out_proj_ln_kernel(h_ref, y0_ref, y1_ref, y2_ref, y3_ref, w_ref, g_ref, b_ref, o_ref, *, alpha):
    mixed = jnp.concatenate([y_ref[...].astype(BF16) for y_ref in (y0_ref, y1_ref, y2_ref, y3_ref)], axis=1)
    acc = jnp.dot(mixed, w_ref[...], preferred_element_type=F32)
    o_ref[...] = _layer_norm(alpha * h_ref[...] + acc, g_ref[...], b_ref[...])


def _out_proj_ln(h, ys, w_out, ln_g, ln_b, *, layer, alpha, tm):
    n, d = h.shape
    y_spec = pl.BlockSpec((tm, G_W), lambda i: (i, 0))
    ln_spec = pl.BlockSpec((None, None, 1, d), lambda i: (layer, 1, 0, 0))
    return pl.pallas_call(
        functools.partial(_out_proj_ln_kernel, alpha=alpha),
        out_shape=jax.ShapeDtypeStruct((n, d), F32),
        grid=(n // tm,),
        in_specs=[pl.BlockSpec((tm, d), lambda i: (i, 0)), y_spec, y_spec, y_spec, y_spec,
                  pl.BlockSpec((None, 4 * G_W, d), lambda i: (layer, 0, 0)), ln_spec, ln_spec],
        out_specs=pl.BlockSpec((tm, d), lambda i: (i, 0)),
        compiler_params=_params("parallel"),
        name="out_proj_ln",
    )(h, *ys, w_out, ln_g.reshape(ln_g.shape[0], ln_g.shape[1], 1, d), ln_b.reshape(ln_b.shape[0], ln_b.shape[1], 1, d))


def _pool_sgu_kernel(cur_ref, halo_ref, u_ref, v_ref, pw_ref, ps_ref, sw_ref, sbc_ref,
                     yp_ref, ys_ref, ext_ref, *, chunks_per_seq):
    c = pl.program_id(0) % chunks_per_seq
    cur = cur_ref[...]
    ext_ref[0:POOL_HALO, :] = jnp.where(c == 0, 0.0, halo_ref[...])
    ext_ref[POOL_HALO:, :] = cur
    rows = SGU_CHUNK
    pos = c * rows + lax.broadcasted_iota(jnp.int32, (rows, 1), 0)
    for gi, w in enumerate(POOL_WINDOWS):
        cols = slice(gi * HEAD_DIM, (gi + 1) * HEAD_DIM)
        s = cur[:, cols]
        for j in range(1, w):
            s = s + ext_ref[POOL_HALO - j:POOL_HALO - j + rows, cols]
        cnt = jnp.minimum(pos + 1, w).astype(F32)
        d = s / cnt - cur[:, cols]
        yp_ref[:, cols] = _bdot(d, pw_ref[gi]) * ps_ref[:, cols]

    r = lax.broadcasted_iota(jnp.int32, (rows, rows), 0)
    s_ = lax.broadcasted_iota(jnp.int32, (rows, rows), 1)
    causal = r >= s_
    for h in range(HEADS):
        cols = slice(h * HEAD_DIM, (h + 1) * HEAD_DIM)
        wm = jnp.where(causal, sw_ref[h], 0.0)
        mixed = _bdot(wm, v_ref[:, cols]) + sbc_ref[:, h:h + 1]
        ys_ref[:, cols] = u_ref[:, cols] * mixed


def _pool_sgu(proj, pool_w, pool_scale, sgu_w, sgu_b, *, seq):
    n = proj.shape[0]
    rows = SGU_CHUNK
    halo_blocks = rows // POOL_HALO
    blk = lambda col: pl.BlockSpec((rows, G_W), lambda g: (g, col))
    whole = lambda a: pl.BlockSpec(a.shape, lambda g: (0,) * a.ndim)
    ps = pool_scale.reshape(1, G_W)
    sbc = sgu_b.T
    return pl.pallas_call(
        functools.partial(_pool_sgu_kernel, chunks_per_seq=seq // rows),
        out_shape=(jax.ShapeDtypeStruct((n, G_W), F32), jax.ShapeDtypeStruct((n, G_W), F32)),
        grid=(n // rows,),
        in_specs=[blk(COL_POOL),
                  pl.BlockSpec((POOL_HALO, G_W), lambda g: (jnp.maximum(g * halo_blocks - 1, 0), COL_POOL)),
                  blk(COL_SGU_U), blk(COL_SGU_V), whole(pool_w), whole(ps), whole(sgu_w), whole(sbc)],
        out_specs=(pl.BlockSpec((rows, G_W), lambda g: (g, 0)), pl.BlockSpec((rows, G_W), lambda g: (g, 0))),
        scratch_shapes=[pltpu.VMEM((POOL_HALO + rows, G_W), F32)],
        compiler_params=_params("parallel"),
        name="pool_sgu",
    )(proj, proj, proj, proj, pool_w, ps, sgu_w, sbc)


def _sb_attn_kernel(bias_ref, q_ref, k_ref, v_ref, o_ref, *, blk, scale):
    qi = pl.program_id(1)
    heads = lambda x: jnp.stack([x[:, h * HEAD_DIM:(h + 1) * HEAD_DIM] for h in range(HEADS)])
    q = heads(q_ref[...]).astype(BF16)
    bias = bias_ref[...]
    row = lax.broadcasted_iota(jnp.int32, (blk, blk), 0)
    col = lax.broadcasted_iota(jnp.int32, (blk, blk), 1)
    after = (row > col).astype(BF16)

    def visit(j, carry, valid):
        o, run = carry
        start = pl.multiple_of(j * blk, blk)
        kb = heads(k_ref[pl.ds(start, blk), :]).astype(BF16)
        vb = heads(v_ref[pl.ds(start, blk), :]).astype(BF16)
        z = lax.dot_general(q, kb, _B_NT, preferred_element_type=F32) * scale + bias
        sp = jnp.maximum(z, 0.0) + jnp.log(1.0 + jnp.exp(-jnp.abs(z)))
        log_fail = -sp if valid is None else jnp.where(valid, -sp, 0.0)
        hi = log_fail.reshape(HEADS * blk, blk).astype(BF16)
        later = jnp.dot(hi, after, preferred_element_type=F32).reshape(HEADS, blk, blk) + run
        att = jnp.exp(z - sp + later)
        if valid is not None:
            att = jnp.where(valid, att, 0.0)
        o = o + lax.dot_general(att.astype(BF16), vb, _B_NN, preferred_element_type=F32)
        run = run + jnp.sum(log_fail, axis=-1, keepdims=True)
        return o, run

    init = (jnp.zeros((HEADS, blk, HEAD_DIM), F32), jnp.zeros((HEADS, blk, 1), F32))
    carry = visit(qi, init, (col < row)[None])
    o, _ = lax.fori_loop(0, qi, lambda step, carry: visit(qi - 1 - step, carry, None), carry)
    for h in range(HEADS):
        o_ref[:, h * HEAD_DIM:(h + 1) * HEAD_DIM] = o[h]


def _sb_attn(proj, sb_bias, *, batch, seq, blk):
    n = proj.shape[0]
    nq = seq // blk
    return pl.pallas_call(
        functools.partial(_sb_attn_kernel, blk=blk, scale=HEAD_DIM ** -0.5),
        out_shape=jax.ShapeDtypeStruct((n, G_W), F32),
        grid=(batch, nq),
        in_specs=[pl.BlockSpec((HEADS, 1, 1), lambda b, i: (0, 0, 0)),
                  pl.BlockSpec((blk, G_W), lambda b, i: (b * nq + i, COL_SB_Q)),
                  pl.BlockSpec((seq, G_W), lambda b, i: (b, COL_SB_K)),
                  pl.BlockSpec((seq, G_W), lambda b, i: (b, COL_SB_V))],
        out_specs=pl.BlockSpec((blk, G_W), lambda b, i: (b * nq + i, 0)),
        compiler_params=_params("parallel", "arbitrary"),
        name="sb_attn",
    )(sb_bias.reshape(HEADS, 1, 1), proj, proj, proj)


_DECODE_ROWS = 8


def _own_lanes(width):
    rid = lax.broadcasted_iota(jnp.int32, (_DECODE_ROWS, width), 0)
    cid = lax.broadcasted_iota(jnp.int32, (_DECODE_ROWS, width), 1)
    return (cid & (HEADS - 1)) == rid


def _decode_logits(q_ref, k_refs, width):
    q = q_ref[0]
    q_rows = jnp.concatenate([q[:, h * HEAD_DIM:(h + 1) * HEAD_DIM] for h in range(HEADS)]
                             + [jnp.zeros((_DECODE_ROWS - HEADS, HEAD_DIM), F32)], axis=0).astype(BF16)
    own = _own_lanes(width)
    z_full = [_bdot_nt(q_rows, k_ref[...]) for k_ref in k_refs]
    return jnp.concatenate([jnp.sum(jnp.where(own, zf, 0.0), axis=0, keepdims=True) for zf in z_full], axis=0)


def _decode_weights(qk, bias_ref, cum_ref, width, scale):
    n_pages = qk.shape[0]
    z = qk * scale + bias_ref[...]
    sp = _softplus(z)
    hi, lo = _split_bf16(-sp)
    cum = (jnp.dot(hi, cum_ref[...], preferred_element_type=F32)
           + jnp.dot(lo, cum_ref[...], preferred_element_type=F32))
    in_page, page_tot = cum[:, :width], cum[:, width:]
    later_pages = [None] * n_pages
    run = jnp.zeros((1, width), F32)
    for p in range(n_pages - 1, -1, -1):
        later_pages[p] = run
        run = run + page_tot[p:p + 1, :]
    return jnp.exp(z - sp + in_page + jnp.concatenate(later_pages, axis=0))


def _decode_output(att, v_refs, o_ref, width):
    own = _own_lanes(width)
    spread = [jnp.where(own, jnp.broadcast_to(att[p:p + 1, :], (_DECODE_ROWS, width)), 0.0).astype(BF16)
              for p in range(len(v_refs))]
    parts = [jnp.dot(a_p, v_ref[...].astype(BF16), preferred_element_type=F32) for a_p, v_ref in zip(spread, v_refs)]
    while len(parts) > 1:
        parts = [a + b for a, b in zip(parts[::2], parts[1::2])]
    acc = parts[0]
    for h in range(HEADS):
        o_ref[0, :, h * HEAD_DIM:(h + 1) * HEAD_DIM] = acc[h:h + 1, :]


def _same_head_sums(width):
    src = np.arange(width)[:, None]
    dst = np.arange(width)[None, :]
    same = (src - dst) % HEADS == 0
    return jnp.asarray(np.concatenate([same & (src > dst), same], axis=1), BF16)


def _ffn_ln_decode_kernel(pt_ref, x_ref, wgu_ref, wo_ref, g_ref, b_ref, q_ref, bias_ref, cum_ref, k_hbm, v_hbm,
                          o_ref, y_ref, xb_ref, kbuf, vbuf, sem, *, alpha, tf, n_pages, width, scale, seqs, total):
    i, f = pl.program_id(0), pl.program_id(1)

    @pl.when(f == 0)
    def _():
        xb_ref[...] = x_ref[...].astype(BF16)
        o_ref[...] = jnp.zeros_like(o_ref)

    def page_copies(t):
        slot = lax.rem(t, 2)
        copies = []
        for p in range(n_pages):
            row0 = pl.multiple_of(pt_ref[t * n_pages + p] * width, width)
            copies.append(pltpu.make_async_copy(k_hbm.at[pl.ds(row0, width), :], kbuf.at[slot, p], sem.at[slot]))
            copies.append(pltpu.make_async_copy(v_hbm.at[pl.ds(row0, width), :], vbuf.at[slot, p], sem.at[slot]))
        return copies

    @pl.when(f < seqs)
    def _():
        t = i * seqs + f

        @pl.when(t == 0)
        def _():
            for c in page_copies(t):
                c.start()

        @pl.when(t + 1 < total)
        def _():
            for c in page_copies(t + 1):
                c.start()

        for c in page_copies(t):
            c.wait()
        slot = lax.rem(t, 2)
        qk = _decode_logits(q_ref, [kbuf.at[slot, p] for p in range(n_pages)], width)
        att = _decode_weights(qk, bias_ref, cum_ref, width, scale)
        _decode_output(att, [vbuf.at[slot, p] for p in range(n_pages)], y_ref, width)

    gate_up = jnp.dot(xb_ref[...], wgu_ref[...], preferred_element_type=F32)
    act = (_silu(gate_up[:, :tf]) * gate_up[:, tf:]).astype(BF16)
    o_ref[...] += jnp.dot(act, wo_ref[...], preferred_element_type=F32)

    @pl.when(f == pl.num_programs(1) - 1)
    def _():
        o_ref[...] = _layer_norm(alpha * x_ref[...] + 0.5 * o_ref[...], g_ref[...], b_ref[...])


def _ffn_ln_decode(x, weights, ln_g, ln_b, proj3, bias_row, cache_k, cache_v, phys_pages, *,
                   layer, which, alpha, tm, tf, n_pages, width):
    n, d = x.shape
    bs = proj3.shape[0]
    w_gate_up, w_out = weights
    nf = w_out.shape[0] // tf
    tiles = n // tm
    seqs = bs // tiles
    assert bs == seqs * tiles and seqs <= nf
    seq = lambda i, f: i * seqs + jnp.minimum(f, seqs - 1)
    ln_spec = pl.BlockSpec((None, None, 1, d), lambda i, f, pt: (layer, which, 0, 0))
    grid_spec = pltpu.PrefetchScalarGridSpec(
        num_scalar_prefetch=1,
        grid=(tiles, nf),
        in_specs=[pl.BlockSpec((tm, d), lambda i, f, pt: (i, 0)),
                  pl.BlockSpec((d, 2 * tf), lambda i, f, pt: (0, f)),
                  pl.BlockSpec((tf, d), lambda i, f, pt: (f, 0)),
                  ln_spec, ln_spec,
                  pl.BlockSpec((1, 1, G_W), lambda i, f, pt: (seq(i, f), 0, COL_SB_Q)),
                  pl.BlockSpec((1, width), lambda i, f, pt: (0, 0)),
                  pl.BlockSpec((width, 2 * width), lambda i, f, pt: (0, 0), pipeline_mode=pl.Buffered(1)),
                  pl.BlockSpec(memory_space=pl.ANY), pl.BlockSpec(memory_space=pl.ANY)],
        out_specs=(pl.BlockSpec((tm, d), lambda i, f, pt: (i, 0)),
                   pl.BlockSpec((1, 1, G_W), lambda i, f, pt: (seq(i, f), 0, 0))),
        scratch_shapes=[pltpu.VMEM((tm, d), BF16),
                        pltpu.VMEM((2, n_pages, width, HEAD_DIM), F32),
                        pltpu.VMEM((2, n_pages, width, HEAD_DIM), F32),
                        pltpu.SemaphoreType.DMA((2,))],
    )
    out, y_sb = pl.pallas_call(
        functools.partial(_ffn_ln_decode_kernel, alpha=alpha, tf=tf, n_pages=n_pages, width=width,
                          scale=HEAD_DIM ** -0.5, seqs=seqs, total=bs),
        out_shape=(jax.ShapeDtypeStruct((n, d), F32), jax.ShapeDtypeStruct((bs, 1, G_W), F32)),
        grid_spec=grid_spec,
        compiler_params=pltpu.CompilerParams(dimension_semantics=("arbitrary", "arbitrary"),
                                             vmem_limit_bytes=FUSED_VMEM_LIMIT_BYTES),
        name="ffn_ln_decode",
    )(phys_pages, x, w_gate_up, w_out,
      ln_g.reshape(ln_g.shape[0], ln_g.shape[1], 1, d), ln_b.reshape(ln_b.shape[0], ln_b.shape[1], 1, d),
      proj3, bias_row, _same_head_sums(width), cache_k, cache_v)
    return out, y_sb.reshape(bs, G_W)


def _beta_and_log_decay(raw, neg_a_row, dt_row):
    lane = lax.broadcasted_iota(jnp.int32, raw.shape, 1)
    beta = jax.nn.sigmoid(raw)
    g = neg_a_row * _softplus(raw + dt_row)
    return jnp.where(lane < HEADS, beta, g)


def _l2_normalize(x):
    return x * lax.rsqrt(jnp.sum(x * x, -1, keepdims=True) + NORM_EPS)


def _dn_prep_kernel(raw_ref, halo_ref, bgr_ref, cw_ref, na_ref, dt_ref,
                    q_ref, k_ref, v_ref, bg_ref, bgt_ref, ext_ref, *, tiles_per_seq, rows):
    first = (pl.program_id(0) % tiles_per_seq) == 0
    ext_ref[0:CONV_HALO, :] = jnp.where(first, 0.0, halo_ref[...])
    ext_ref[CONV_HALO:, :] = raw_ref[...]
    conv = None
    for j in range(CONV_W):
        off = CONV_HALO - (CONV_W - 1) + j
        term = ext_ref[off:off + rows, :] * cw_ref[j:j + 1, :]
        conv = term if conv is None else conv + term
    act = _silu(conv)
    for h in range(HEADS):
        cols = slice(h * HEAD_DIM, (h + 1) * HEAD_DIM)
        q_ref[:, cols] = _l2_normalize(act[:, h * HEAD_DIM:(h + 1) * HEAD_DIM]) * (HEAD_DIM ** -0.5)
        k_ref[:, cols] = _l2_normalize(act[:, G_W + h * HEAD_DIM:G_W + (h + 1) * HEAD_DIM])
    v_ref[...] = act[:, 2 * G_W:]
    bg = _beta_and_log_decay(bgr_ref[...], na_ref[...], dt_ref[...])
    r = lax.broadcasted_iota(jnp.int32, (rows, rows), 0)
    s = lax.broadcasted_iota(jnp.int32, (rows, rows), 1)
    shift = DN_CHUNK.bit_length() - 1
    chunk_prefix = ((r >= s) & (lax.shift_right_logical(r, shift) == lax.shift_right_logical(s, shift)))
    lane = lax.broadcasted_iota(jnp.int32, bg.shape, 1)
    bg = jnp.where(lane < HEADS, bg, _hdot(chunk_prefix.astype(F32), bg))
    bg_ref[...] = bg
    per_tile = BG_LANES // DN_CHUNK
    for t in range(rows // BG_LANES):
        tile_t = bg[t * BG_LANES:(t + 1) * BG_LANES, :].T
        for c in range(per_tile):
            bgt_ref[t * per_tile + c] = tile_t[0:2 * HEADS, c * DN_CHUNK:(c + 1) * DN_CHUNK]


def _dn_prep(proj, bg_raw, conv_w, neg_a_row, dt_row, *, seq, rows):
    n = proj.shape[0]
    halo_blocks = rows // CONV_HALO
    qkv_w = 3 * G_W
    col = COL_DN_QKV * G_W // qkv_w
    whole = lambda a: pl.BlockSpec(a.shape, lambda g: (0,) * a.ndim)
    out = jax.ShapeDtypeStruct((n, G_W), F32)
    o_spec = pl.BlockSpec((rows, G_W), lambda g: (g, 0))
    return pl.pallas_call(
        functools.partial(_dn_prep_kernel, tiles_per_seq=seq // rows, rows=rows),
        out_shape=(out, out, out, jax.ShapeDtypeStruct((n, BG_LANES), F32),
                   jax.ShapeDtypeStruct((n // DN_CHUNK, 2 * HEADS, DN_CHUNK), F32)),
        grid=(n // rows,),
        in_specs=[pl.BlockSpec((rows, qkv_w), lambda g: (g, col)),
                  pl.BlockSpec((CONV_HALO, qkv_w), lambda g: (jnp.maximum(g * halo_blocks - 1, 0), col)),
                  pl.BlockSpec((rows, BG_LANES), lambda g: (g, 0)),
                  whole(conv_w), whole(neg_a_row), whole(dt_row)],
        out_specs=(o_spec, o_spec, o_spec, pl.BlockSpec((rows, BG_LANES), lambda g: (g, 0)),
                   pl.BlockSpec((rows // DN_CHUNK, 2 * HEADS, DN_CHUNK), lambda g: (g, 0, 0))),
        scratch_shapes=[pltpu.VMEM((CONV_HALO + rows, qkv_w), F32)],
        compiler_params=_params("parallel"),
        name="dn_prep",
    )(proj, proj, bg_raw, conv_w, neg_a_row, dt_row)


def _cat_lhs(x, axis):
    hi = x.astype(BF16).astype(F32)
    return jnp.concatenate([hi, x - hi, hi], axis=axis).astype(BF16)


def _cat_rhs(x, axis):
    hi = x.astype(BF16)
    lo = (x - hi.astype(F32)).astype(BF16)
    return jnp.concatenate([hi, hi, lo], axis=axis)


def _mm(lhs_cat, rhs_cat, dims=_B_NN):
    return lax.dot_general(lhs_cat, rhs_cat, dims, preferred_element_type=F32)


def _unit_lower_inverse(a_mat, eye):
    n = a_mat.shape[-1]
    inv = eye - a_mat
    power = _mm(_cat_lhs(a_mat, 2), _cat_rhs(a_mat, 1))
    span = 2
    while span < n:
        p_rhs = _cat_rhs(power, 1)
        inv = inv + _mm(_cat_lhs(inv, 2), p_rhs)
        span *= 2
        if span < n:
            power = _mm(_cat_lhs(power, 2), p_rhs)
    return inv


def _dn_out(o, z, norm_g):
    o = o * lax.rsqrt(jnp.mean(o * o, -1, keepdims=True) + NORM_EPS) * norm_g
    return o * _silu(z)


def _dn_scan_kernel(q_ref, k_ref, v_ref, z_ref, bg_ref, bgt_ref, ng_ref, y_ref, s_out_ref, s_ref, *, chunks):
    t = pl.program_id(1)

    @pl.when(t == 0)
    def _():
        s_ref[...] = jnp.zeros_like(s_ref)

    c, d = DN_CHUNK, HEAD_DIM
    units = [(ci, h) for ci in range(chunks) for h in range(HEADS)]
    rows = lambda ci: slice(ci * c, (ci + 1) * c)
    cols = lambda h: slice(h * d, (h + 1) * d)
    gather = lambda ref: jnp.stack([ref[rows(ci), cols(h)] for ci, h in units])
    q, k, v = gather(q_ref), gather(k_ref), gather(v_ref)
    beta = jnp.stack([bg_ref[rows(ci), h:h + 1] for ci, h in units])
    g_col = jnp.stack([bg_ref[rows(ci), HEADS + h:HEADS + h + 1] for ci, h in units])
    g_row = jnp.stack([bgt_ref[ci, HEADS + h:HEADS + h + 1, :] for ci, h in units])

    r = lax.broadcasted_iota(jnp.int32, (c, c), 0)
    s_ = lax.broadcasted_iota(jnp.int32, (c, c), 1)
    incl, strict, eye = (r >= s_)[None], (r > s_)[None], (r == s_).astype(F32)[None]
    decay = jnp.where(incl, jnp.exp(jnp.where(incl, g_col - g_row, 0.0)), 0.0)
    qk_hi, qk_lo = _split_bf16(jnp.concatenate([q, k], axis=1))
    k_hi, k_lo = _split_bf16(k)
    qk_kk = (_mm(jnp.concatenate([qk_hi, qk_lo], axis=2), jnp.concatenate([k_hi, k_hi], axis=2), _B_NT)
             + _mm(qk_hi, k_lo, _B_NT))
    qk = qk_kk[:, :c] * decay
    a_mat = jnp.where(strict, beta * qk_kk[:, c:] * decay, 0.0)
    inv = _unit_lower_inverse(a_mat, eye)
    e_g = jnp.exp(g_col)
    rhs = jnp.concatenate([beta * v, (beta * e_g) * k], axis=2)
    sol = _mm(_cat_lhs(inv, 2), _cat_rhs(rhs, 1))
    u_base, w_mat = sol[:, :, :d], sol[:, :, d:]
    w_hi, w_lo = _split_bf16(w_mat)
    w_cat = jnp.concatenate([w_hi, w_lo], axis=2)
    q_dec = (q * e_g).astype(BF16)
    g_last = g_col[:, c - 1:c, :]
    k_dec = k * jnp.exp(g_last - g_col)
    keep = jnp.exp(g_last)

    norm_g = ng_ref[...]
    s = s_ref[...]
    for ci in range(chunks):
        sl = slice(ci * HEADS, (ci + 1) * HEADS)
        s_hi, s_lo = _split_bf16(s)
        u = u_base[sl] - (_mm(w_cat[sl], jnp.concatenate([s_hi, s_hi], axis=1)) + _mm(w_hi[sl], s_lo))
        o = _mm(q_dec[sl], s_hi) + _mm(qk[sl].astype(BF16), u.astype(BF16))
        s = s * keep[sl] + _mm(_cat_lhs(k_dec[sl], 1), _cat_rhs(u, 1), _B_TN)
        for h in range(HEADS):
            y_ref[rows(ci), cols(h)] = _dn_out(o[h], z_ref[rows(ci), cols(h)], norm_g)
    s_ref[...] = s

    @pl.when(t == pl.num_programs(1) - 1)
    def _():
        s_out_ref[0] = s_ref[...]


def _dn_scan(q, k, v, proj, bg, bgt, norm_g, *, batch, seq, chunks):
    n = q.shape[0]
    rows = chunks * DN_CHUNK
    steps = seq // rows
    spec = pl.BlockSpec((rows, G_W), lambda b, t: (b * steps + t, 0))
    ng = norm_g.reshape(1, HEAD_DIM)
    return pl.pallas_call(
        functools.partial(_dn_scan_kernel, chunks=chunks),
        out_shape=(jax.ShapeDtypeStruct((n, G_W), F32),
                   jax.ShapeDtypeStruct((batch, HEADS, HEAD_DIM, HEAD_DIM), F32)),
        grid=(batch, steps),
        in_specs=[spec, spec, spec,
                  pl.BlockSpec((rows, G_W), lambda b, t: (b * steps + t, COL_DN_Z)),
                  pl.BlockSpec((rows, BG_LANES), lambda b, t: (b * steps + t, 0)),
                  pl.BlockSpec((chunks, 2 * HEADS, DN_CHUNK), lambda b, t: (b * steps + t, 0, 0)),
                  pl.BlockSpec((1, HEAD_DIM), lambda b, t: (0, 0))],
        out_specs=(spec, pl.BlockSpec((1, HEADS, HEAD_DIM, HEAD_DIM), lambda b, t: (b, 0, 0, 0))),
        scratch_shapes=[pltpu.VMEM((HEADS, HEAD_DIM, HEAD_DIM), F32)],
        compiler_params=_params("parallel", "arbitrary"),
        name="dn_scan",
    )(q, k, v, proj, bg, bgt, ng)


def _sample_mix_kernel(pool_ref, u_ref, v_ref, qkv_ref, z_ref, bgr_ref, sp_ref, sc_ref, sd_ref,
                       pw_ref, ps_ref, sw0_ref, sb0_ref, cw_ref, na_ref, dt_ref, ng_ref, *rest, bt, pos0):
    yp_ref, ys_ref, yd_ref, np_ref, nc_ref, nd_ref = rest[-6:]
    new = pool_ref[...]
    for gi, w in enumerate(POOL_WINDOWS):
        cols = slice(gi * HEAD_DIM, (gi + 1) * HEAD_DIM)
        s = new[:, cols]
        for j in range(1, w):
            s = s + sp_ref[POOL_BUF - j, :, cols]
        d = s / float(min(pos0 + 1, w)) - new[:, cols]
        yp_ref[:, cols] = _hdot(d, pw_ref[gi]) * ps_ref[:, cols]
    np_ref[0:POOL_BUF - 1] = sp_ref[1:POOL_BUF]
    np_ref[POOL_BUF - 1] = new

    ys_ref[...] = u_ref[...] * (sw0_ref[...] * v_ref[...] + sb0_ref[...])

    raw = qkv_ref[...]
    conv = raw * cw_ref[CONV_W - 1:CONV_W, :]
    for j in range(CONV_W - 1):
        conv = conv + sc_ref[j] * cw_ref[j:j + 1, :]
    nc_ref[0:CONV_W - 2] = sc_ref[1:CONV_W - 1]
    nc_ref[CONV_W - 2] = raw
    act = _silu(conv)
    bg = _beta_and_log_decay(bgr_ref[...], na_ref[...], dt_ref[...])
    norm_g = ng_ref[...]
    d = HEAD_DIM
    pad = jnp.zeros((d - bt, d), F32)
    k_t, q_t = [], []
    for h in range(HEADS):
        k_t.append(jnp.concatenate([_l2_normalize(act[:, G_W + h * d:G_W + (h + 1) * d]), pad], axis=0).T)
        q_t.append(jnp.concatenate([_l2_normalize(act[:, h * d:(h + 1) * d]) * (d ** -0.5), pad], axis=0).T)
    pairs = [(b, h) for b in range(bt) for h in range(HEADS)]
    v_h = [act[:, 2 * G_W + h * d:2 * G_W + (h + 1) * d] for h in range(HEADS)]
    z_h = [z_ref[:, h * d:(h + 1) * d] for h in range(HEADS)]
    a = [jnp.exp(bg[b:b + 1, HEADS + h:HEADS + h + 1]) for b, h in pairs]
    k_col = [jnp.broadcast_to(k_t[h][:, b:b + 1], (d, d)) for b, h in pairs]
    k_s = [jnp.sum(k_col[i] * sd_ref[b, h], axis=0, keepdims=True) for i, (b, h) in enumerate(pairs)]
    u = [bg[b:b + 1, h:h + 1] * (v_h[h][b:b + 1, :] - a[i] * k_s[i]) for i, (b, h) in enumerate(pairs)]
    s1 = [a[i] * sd_ref[b, h] + k_col[i] * u[i] for i, (b, h) in enumerate(pairs)]
    for i, (b, h) in enumerate(pairs):
        nd_ref[b, h] = s1[i]
    o = [jnp.sum(jnp.broadcast_to(q_t[h][:, b:b + 1], (d, d)) * s1[i], axis=0, keepdims=True)
         for i, (b, h) in enumerate(pairs)]
    for i, (b, h) in enumerate(pairs):
        yd_ref[b:b + 1, h * d:(h + 1) * d] = _dn_out(o[i], z_h[h][b:b + 1, :], norm_g)


def _sample_mix(proj, bg_raw, state_pool, state_conv, state_delta, new_prev, pool_w, pool_scale, sgu_w0, sgu_b0,
                conv_w, neg_a_row, dt_row, norm_g, *, layer, bt, pos0):
    bs = proj.shape[0]
    qkv_w = 3 * G_W
    col = lambda c: pl.BlockSpec((bt, G_W), lambda i: (i, c))
    whole = lambda a: pl.BlockSpec(a.shape, lambda i: (0,) * a.ndim)
    row = pl.BlockSpec((bt, G_W), lambda i: (i, 0))
    ps = pool_scale.reshape(1, G_W)
    ng = norm_g.reshape(1, HEAD_DIM)
    consts = (pool_w, ps, sgu_w0, sgu_b0, conv_w, neg_a_row, dt_row, ng)
    out_sds = lambda a: jax.ShapeDtypeStruct(a.shape, F32)
    pool_spec = pl.BlockSpec((None, POOL_BUF, bt, G_W), lambda i: (layer, 0, i, 0))
    conv_spec = pl.BlockSpec((None, CONV_W - 1, bt, qkv_w), lambda i: (layer, 0, i, 0))
    delta_spec = pl.BlockSpec((None, bt, HEADS, HEAD_DIM, HEAD_DIM), lambda i: (layer, i, 0, 0, 0))
    y = jax.ShapeDtypeStruct((bs, G_W), F32)
    prev = () if new_prev is None else tuple(new_prev)
    n_in = 9 + len(consts)
    return pl.pallas_call(
        functools.partial(_sample_mix_kernel, bt=bt, pos0=pos0),
        out_shape=(y, y, y, out_sds(state_pool), out_sds(state_conv), out_sds(state_delta)),
        grid=(bs // bt,),
        in_specs=[col(COL_POOL), col(COL_SGU_U), col(COL_SGU_V),
                  pl.BlockSpec((bt, qkv_w), lambda i: (i, COL_DN_QKV * G_W // qkv_w)),
                  col(COL_DN_Z),
                  pl.BlockSpec((bt, BG_LANES), lambda i: (i, 0)),
                  pool_spec, conv_spec, delta_spec] + [whole(a) for a in consts]
                 + [pl.BlockSpec(memory_space=pl.ANY)] * len(prev),
        out_specs=(row, row, row, pool_spec, conv_spec, delta_spec),
        input_output_aliases={n_in + k: 3 + k for k in range(len(prev))},
        compiler_params=_params("parallel"),
        name="sample_mix",
    )(proj, proj, proj, proj, proj, bg_raw, state_pool, state_conv, state_delta, *consts, *prev)


def _lane_row(values, offset):
    return jnp.zeros((1, BG_LANES), F32).at[0, offset:offset + values.shape[0]].set(values)


def kernel(x_prompt, x_sample, cache_k, cache_v, page_table, state_pool, state_conv, state_delta,
           ln_g, ln_b, w_ffn1_in, w_ffn1_out, w_ffn2_in, w_ffn2_out, w_in, w_out,
           pool_w, pool_scale, sgu_w, sgu_b, sb_bias, dn_conv_w, dn_a_log, dn_dt_bias, dn_norm_g):
    depth = ln_g.shape[0]
    alpha = (2.0 * depth) ** 0.25
    bp, seq, d_model = x_prompt.shape
    bs, dec_seq, _ = x_sample.shape
    assert dec_seq == 1
    n_phys, page = cache_k.shape[1], cache_k.shape[2]
    n_pages = page_table.shape[1]
    pos0 = n_pages * page
    ck = cache_k.reshape(depth * n_phys * page * HEADS, HEAD_DIM)
    cv = cache_v.reshape(depth * n_phys * page * HEADS, HEAD_DIM)
    hp = x_prompt.reshape(bp * seq, d_model)
    hs = x_sample.reshape(bs, d_model)
    pool_tm = jnp.swapaxes(state_pool, 1, 2)
    conv_tm = jnp.swapaxes(state_conv, 1, 2)
    tm_p = 512
    tm_s = bs
    tail_w = w_in.shape[2] - N_MAIN_COLS
    w_in_tail = jnp.pad(w_in[:, :, N_MAIN_COLS:].astype(BF16), ((0, 0), (0, 0), (0, BG_LANES - tail_w)))
    w_in_t = jnp.swapaxes(w_in, 1, 2)
    w_out_bf = w_out.astype(BF16)

    outs = {name: [] for name in ("poolp", "convp", "sp", "ks", "vs", "sgus")}
    kv_rows = None
    new_states = None
    for l in range(depth):
        neg_a_row = _lane_row(-jnp.exp(dn_a_log[l]), HEADS)
        dt_row = _lane_row(dn_dt_bias[l], HEADS)
        ffn = functools.partial(_ffn_ln, ln_g=ln_g, ln_b=ln_b, layer=l, alpha=alpha, tf=512)
        out_proj = functools.partial(_out_proj_ln, w_out=w_out_bf, ln_g=ln_g, ln_b=ln_b, layer=l, alpha=alpha)

        g1, *w1 = ffn(hs, (w_ffn1_in, w_ffn1_out), which=0, tm=tm_s)
        sproj, sbg_raw, sk_rows, sv_rows, wi_main = _in_proj_cast(g1, w_in_t, w_in_tail, layer=l)
        sgu_w0 = jnp.repeat(sgu_w[l, :, 0, 0], HEAD_DIM).reshape(1, G_W)
        sgu_b0 = jnp.repeat(sgu_b[l, :, 0], HEAD_DIM).reshape(1, G_W)
        sy_pool, sy_sgu, sy_dn, *new_states = _sample_mix(
            sproj, sbg_raw, pool_tm, conv_tm, state_delta, new_states, pool_w[l], pool_scale[l],
            sgu_w0, sgu_b0, dn_conv_w[l], neg_a_row, dt_row, dn_norm_g[l], layer=l, bt=8, pos0=pos0)
        bias_row = jnp.tile(sb_bias[l], page).reshape(1, page * HEADS)
        phys = (page_table + l * n_phys).reshape(-1)
        h1, sy_sb = _ffn_ln_decode(hp, tuple(w1), ln_g, ln_b, sproj.reshape(bs, 1, N_MAIN_COLS), bias_row, ck, cv, phys,
                                   layer=l, which=0, alpha=alpha, tm=tm_p, tf=512,
                                   n_pages=n_pages, width=page * HEADS)
        g2 = out_proj(g1, (sy_pool, sy_sgu, sy_sb, sy_dn), tm=tm_s)
        hs, *w2 = ffn(g2, (w_ffn2_in, w_ffn2_out), which=2, tm=tm_s)
        outs["ks"].append(sk_rows.reshape(bs, 1, HEADS, HEAD_DIM))
        outs["vs"].append(sv_rows.reshape(bs, 1, HEADS, HEAD_DIM))
        outs["sgus"].append(sproj[:, COL_SGU_V * G_W:(COL_SGU_V + 1) * G_W].reshape(bs, 1, G_W))

        proj, bg_raw, *kv_rows = _in_proj(h1, wi_main, w_in_tail, kv_rows, layer=l, depth=depth, tm=256)
        y_pool, y_sgu = _pool_sgu(proj, pool_w[l], pool_scale[l], sgu_w[l], sgu_b[l], seq=seq)
        y_sb = _sb_attn(proj, sb_bias[l], batch=bp, seq=seq, blk=256)
        dq, dk, dv, bg, bgt = _dn_prep(proj, bg_raw, dn_conv_w[l], neg_a_row, dt_row, seq=seq, rows=256)
        y_dn, s_end = _dn_scan(dq, dk, dv, proj, bg, bgt, dn_norm_g[l], batch=bp, seq=seq, chunks=4)
        h2 = out_proj(h1, (y_pool, y_sgu, y_sb, y_dn), tm=256)
        hp, = ffn(h2, tuple(w2), which=2, tm=tm_p)
        proj3 = proj.reshape(bp, seq, N_MAIN_COLS)
        outs["poolp"].append(proj3[:, seq - POOL_BUF:, :G_W])
        outs["convp"].append(proj3[:, seq - (CONV_W - 1):, COL_DN_QKV * G_W:(COL_DN_QKV + 3) * G_W])
        outs["sp"].append(s_end)

    st = lambda name: jnp.stack(outs[name])
    new_pool, new_conv, new_delta = new_states
    kp, vp = (a.reshape(depth, bp, seq, HEADS, HEAD_DIM) for a in kv_rows)
    return (hp.reshape(bp, seq, d_model), hs.reshape(bs, 1, d_model),
            kp, vp, st("poolp"), st("convp"), st("sp"),
            st("ks"), st("vs"), jnp.swapaxes(new_pool, 1, 2), jnp.swapaxes(new_conv, 1, 2),
            new_delta, st("sgus"))
```

```python
import functools

import jax
import jax.numpy as jnp
import numpy as np
from jax import lax
from jax.experimental import pallas as pl
from jax.experimental.pallas import tpu as pltpu

F32 = jnp.float32
BF16 = jnp.bfloat16
HIGHEST = lax.Precision.HIGHEST

HEADS = 4
HEAD_DIM = 128
G_W = HEADS * HEAD_DIM
POOL_WINDOWS = (2, 4, 8, 16)
POOL_BUF = max(POOL_WINDOWS) - 1
POOL_HALO = 16
SGU_CHUNK = 128
DN_CHUNK = 64
CONV_W = 4
CONV_HALO = 8
LN_EPS = 1e-5
NORM_EPS = 1e-6
VMEM_LIMIT_BYTES = 48 * 1024 * 1024
FUSED_VMEM_LIMIT_BYTES = 58 * 1024 * 1024

COL_POOL, COL_SGU_U, COL_SGU_V, COL_SB_Q, COL_SB_K, COL_SB_V, COL_DN_QKV, COL_DN_Z = 0, 1, 2, 3, 4, 5, 6, 9
N_MAIN_COLS = 10 * G_W
BG_LANES = 128


def _params(*sem):
    return pltpu.CompilerParams(dimension_semantics=sem, vmem_limit_bytes=VMEM_LIMIT_BYTES)


def _hdot(a, b):
    return jnp.dot(a, b, precision=HIGHEST, preferred_element_type=F32)


def _bdot(a, b):
    return jnp.dot(a.astype(BF16), b.astype(BF16), preferred_element_type=F32)


def _bdot_nt(a, b):
    return lax.dot_general(a.astype(BF16), b.astype(BF16), (((1,), (1,)), ((), ())),
                           preferred_element_type=F32)


def _silu(x):
    return x * jax.nn.sigmoid(x)


def _softplus(x):
    return jnp.maximum(x, 0.0) + jnp.log1p(jnp.exp(-jnp.abs(x)))


def _layer_norm(y, g, b):
    mu = jnp.mean(y, -1, keepdims=True)
    d = y - mu
    var = jnp.mean(d * d, -1, keepdims=True)
    return d * lax.rsqrt(var + LN_EPS) * g + b


def _split_bf16(x):
    hi = x.astype(BF16)
    lo = (x - hi.astype(F32)).astype(BF16)
    return hi, lo


_B_NN = (((2,), (1,)), ((0,), (0,)))
_B_NT = (((2,), (2,)), ((0,), (0,)))
_B_TN = (((1,), (1,)), ((0,), (0,)))


def _ffn_ln_kernel(x_ref, *refs, alpha, emit_bf16, tf):
    if emit_bf16:
        wg_ref, wu_ref, wo_ref, g_ref, b_ref, o_ref, wgu_out, wo_out, xb_ref, acc_ref = refs
    else:
        wgu_ref, wo_ref, g_ref, b_ref, o_ref, xb_ref, acc_ref = refs
    f = pl.program_id(1)

    @pl.when(f == 0)
    def _():
        xb_ref[...] = x_ref[...].astype(BF16)
        acc_ref[...] = jnp.zeros_like(acc_ref)

    xb = xb_ref[...]
    if emit_bf16:
        wg, wu, wo = wg_ref[...].astype(BF16), wu_ref[...].astype(BF16), wo_ref[...].astype(BF16)
        wgu_out[:, :tf], wgu_out[:, tf:], wo_out[...] = wg, wu, wo
        gate = jnp.dot(xb, wg, preferred_element_type=F32)
        up = jnp.dot(xb, wu, preferred_element_type=F32)
    else:
        wo = wo_ref[...]
        gate_up = jnp.dot(xb, wgu_ref[...], preferred_element_type=F32)
        gate, up = gate_up[:, :tf], gate_up[:, tf:]
    act = (_silu(gate) * up).astype(BF16)
    acc_ref[...] += jnp.dot(act, wo, preferred_element_type=F32)

    @pl.when(f == pl.num_programs(1) - 1)
    def _():
        o_ref[...] = _layer_norm(alpha * x_ref[...] + 0.5 * acc_ref[...], g_ref[...], b_ref[...])


def _ffn_ln(x, weights, ln_g, ln_b, *, layer, which, alpha, tm, tf):
    n, d = x.shape
    emit_bf16 = weights[0].dtype == F32
    if emit_bf16:
        w_in, w_out = weights
        dff = w_out.shape[1]
        nf = dff // tf
        operands = (w_in, w_in, w_out)
        w_specs = [pl.BlockSpec((None, d, tf), lambda i, f: (layer, 0, f)),
                   pl.BlockSpec((None, d, tf), lambda i, f: (layer, 0, f + nf)),
                   pl.BlockSpec((None, tf, d), lambda i, f: (layer, f, 0))]
    else:
        operands = weights
        dff = weights[1].shape[0]
        nf = dff // tf
        w_specs = [pl.BlockSpec((d, 2 * tf), lambda i, f: (0, f)),
                   pl.BlockSpec((tf, d), lambda i, f: (f, 0))]
    ln_spec = pl.BlockSpec((None, None, 1, d), lambda i, f: (layer, which, 0, 0))
    out_shape = [jax.ShapeDtypeStruct((n, d), F32)]
    out_specs = [pl.BlockSpec((tm, d), lambda i, f: (i, 0))]
    if emit_bf16:
        assert n == tm
        out_shape += [jax.ShapeDtypeStruct((d, 2 * dff), BF16), jax.ShapeDtypeStruct((dff, d), BF16)]
        out_specs += [pl.BlockSpec((d, 2 * tf), lambda i, f: (0, f)), pl.BlockSpec((tf, d), lambda i, f: (f, 0))]
    return pl.pallas_call(
        functools.partial(_ffn_ln_kernel, alpha=alpha, emit_bf16=emit_bf16, tf=tf),
        out_shape=tuple(out_shape),
        grid=(n // tm, nf),
        in_specs=[pl.BlockSpec((tm, d), lambda i, f: (i, 0))] + w_specs + [ln_spec, ln_spec],
        out_specs=tuple(out_specs),
        scratch_shapes=[pltpu.VMEM((tm, d), BF16), pltpu.VMEM((tm, d), F32)],
        compiler_params=_params("parallel", "arbitrary"),
        name="ffn_ln_cast" if emit_bf16 else "ffn_ln",
    )(x, *operands, ln_g.reshape(ln_g.shape[0], ln_g.shape[1], 1, d), ln_b.reshape(ln_b.shape[0], ln_b.shape[1], 1, d))


def _store_head_rows(dst_ref, block, tm):
    for h in range(HEADS):
        dst_ref[pl.ds(h, tm, stride=HEADS), :] = block[:, h * HEAD_DIM:(h + 1) * HEAD_DIM]


def _in_proj_kernel(*refs, tm, aliased):
    h_ref, w_ref, wt_ref = refs[:3]
    o_ref, bg_ref, k_ref, v_ref = refs[3 + aliased:]
    hb = h_ref[...].astype(BF16)
    bg_ref[...] = jnp.dot(hb, wt_ref[...], preferred_element_type=F32)
    o = jnp.dot(hb, w_ref[...], preferred_element_type=F32)
    o_ref[...] = o
    _store_head_rows(k_ref, o[:, COL_SB_K * G_W:(COL_SB_K + 1) * G_W], tm)
    _store_head_rows(v_ref, o[:, COL_SB_V * G_W:(COL_SB_V + 1) * G_W], tm)


def _in_proj(h, w_main, w_tail, kv_prev, *, layer, depth, tm):
    n, d = h.shape
    steps = n // tm
    resident = pl.Buffered(1)
    rows_kv = jax.ShapeDtypeStruct((depth * n * HEADS, HEAD_DIM), F32)
    kv_spec = pl.BlockSpec((tm * HEADS, HEAD_DIM), lambda i: (layer * steps + i, 0))
    aliased = 0 if kv_prev is None else 2
    prev = () if kv_prev is None else tuple(kv_prev)
    return pl.pallas_call(
        functools.partial(_in_proj_kernel, tm=tm, aliased=aliased),
        out_shape=(jax.ShapeDtypeStruct((n, N_MAIN_COLS), F32), jax.ShapeDtypeStruct((n, BG_LANES), F32),
                   rows_kv, rows_kv),
        grid=(steps,),
        in_specs=[
            pl.BlockSpec((tm, d), lambda i: (i, 0)),
            pl.BlockSpec((d, N_MAIN_COLS), lambda i: (0, 0), pipeline_mode=resident),
            pl.BlockSpec((None, d, BG_LANES), lambda i: (layer, 0, 0), pipeline_mode=resident),
        ] + [pl.BlockSpec(memory_space=pl.ANY)] * aliased,
        out_specs=(pl.BlockSpec((tm, N_MAIN_COLS), lambda i: (i, 0)),
                   pl.BlockSpec((tm, BG_LANES), lambda i: (i, 0)), kv_spec, kv_spec),
        input_output_aliases={3: 2, 4: 3} if aliased else {},
        compiler_params=_params("parallel"),
        name="in_proj",
    )(h, w_main, w_tail, *prev)


def _in_proj_cast_kernel(h_ref, w_ref, wt_ref, o_ref, bg_ref, k_ref, v_ref, wb_ref, *, tm):
    j = pl.program_id(0)
    hb = h_ref[...].astype(BF16)
    wb = w_ref[...].T.astype(BF16)
    wb_ref[...] = wb
    o = jnp.dot(hb, wb, preferred_element_type=F32)
    o_ref[...] = o

    @pl.when(j == 0)
    def _():
        bg_ref[...] = jnp.dot(hb, wt_ref[...], preferred_element_type=F32)

    @pl.when(j == COL_SB_K)
    def _():
        _store_head_rows(k_ref, o, tm)

    @pl.when(j == COL_SB_V)
    def _():
        _store_head_rows(v_ref, o, tm)


def _in_proj_cast(h, w_in_t, w_tail, *, layer):
    n, d = h.shape
    rows_kv = jax.ShapeDtypeStruct((n * HEADS, HEAD_DIM), F32)
    kv_spec = pl.BlockSpec((n * HEADS, HEAD_DIM), lambda j: (0, 0))
    return pl.pallas_call(
        functools.partial(_in_proj_cast_kernel, tm=n),
        out_shape=(jax.ShapeDtypeStruct((n, N_MAIN_COLS), F32), jax.ShapeDtypeStruct((n, BG_LANES), F32),
                   rows_kv, rows_kv, jax.ShapeDtypeStruct((d, N_MAIN_COLS), BF16)),
        grid=(N_MAIN_COLS // G_W,),
        in_specs=[
            pl.BlockSpec((n, d), lambda j: (0, 0)),
            pl.BlockSpec((None, G_W, d), lambda j: (layer, j, 0)),
            pl.BlockSpec((None, d, BG_LANES), lambda j: (layer, 0, 0)),
        ],
        out_specs=(pl.BlockSpec((n, G_W), lambda j: (0, j)),
                   pl.BlockSpec((n, BG_LANES), lambda j: (0, 0)), kv_spec, kv_spec,
                   pl.BlockSpec((d, G_W), lambda j: (0, j))),
        compiler_params=_params("arbitrary"),
        name="in_proj_cast",
    )(h, w_in_t, w_tail)


def _out_proj_ln_kernel(h_ref, y0_ref, y1_ref, y2_ref, y3_ref, w_ref, g_ref, b_ref, o_ref, *, alpha):
    mixed = jnp.concatenate([y_ref[...].astype(BF16) for y_ref in (y0_ref, y1_ref, y2_ref, y3_ref)], axis=1)
    acc = jnp.dot(mixed, w_ref[...], preferred_element_type=F32)
    o_ref[...] = _layer_norm(alpha * h_ref[...] + acc, g_ref[...], b_ref[...])


def _out_proj_ln(h, ys, w_out, ln_g, ln_b, *, layer, alpha, tm):
    n, d = h.shape
    y_spec = pl.BlockSpec((tm, G_W), lambda i: (i, 0))
    ln_spec = pl.BlockSpec((None, None, 1, d), lambda i: (layer, 1, 0, 0))
    return pl.pallas_call(
        functools.partial(_out_proj_ln_kernel, alpha=alpha),
        out_shape=jax.ShapeDtypeStruct((n, d), F32),
        grid=(n // tm,),
        in_specs=[pl.BlockSpec((tm, d), lambda i: (i, 0)), y_spec, y_spec, y_spec, y_spec,
                  pl.BlockSpec((None, 4 * G_W, d), lambda i: (layer, 0, 0)), ln_spec, ln_spec],
        out_specs=pl.BlockSpec((tm, d), lambda i: (i, 0)),
        compiler_params=_params("parallel"),
        name="out_proj_ln",
    )(h, *ys, w_out, ln_g.reshape(ln_g.shape[0], ln_g.shape[1], 1, d), ln_b.reshape(ln_b.shape[0], ln_b.shape[1], 1, d))


def _pool_sgu_kernel(cur_ref, halo_ref, u_ref, v_ref, pw_ref, ps_ref, sw_ref, sbc_ref,
                     yp_ref, ys_ref, ext_ref, *, chunks_per_seq):
    c = pl.program_id(0) % chunks_per_seq
    cur = cur_ref[...]
    ext_ref[0:POOL_HALO, :] = jnp.where(c == 0, 0.0, halo_ref[...])
    ext_ref[POOL_HALO:, :] = cur
    rows = SGU_CHUNK
    pos = c * rows + lax.broadcasted_iota(jnp.int32, (rows, 1), 0)
    for gi, w in enumerate(POOL_WINDOWS):
        cols = slice(gi * HEAD_DIM, (gi + 1) * HEAD_DIM)
        s = cur[:, cols]
        for j in range(1, w):
            s = s + ext_ref[POOL_HALO - j:POOL_HALO - j + rows, cols]
        cnt = jnp.minimum(pos + 1, w).astype(F32)
        d = s / cnt - cur[:, cols]
        yp_ref[:, cols] = _bdot(d, pw_ref[gi]) * ps_ref[:, cols]

    r = lax.broadcasted_iota(jnp.int32, (rows, rows), 0)
    s_ = lax.broadcasted_iota(jnp.int32, (rows, rows), 1)
    causal = r >= s_
    for h in range(HEADS):
        cols = slice(h * HEAD_DIM, (h + 1) * HEAD_DIM)
        wm = jnp.where(causal, sw_ref[h], 0.0)
        mixed = _bdot(wm, v_ref[:, cols]) + sbc_ref[:, h:h + 1]
        ys_ref[:, cols] = u_ref[:, cols] * mixed


def _pool_sgu(proj, pool_w, pool_scale, sgu_w, sgu_b, *, seq):
    n = proj.shape[0]
    rows = SGU_CHUNK
    halo_blocks = rows // POOL_HALO
    blk = lambda col: pl.BlockSpec((rows, G_W), lambda g: (g, col))
    whole = lambda a: pl.BlockSpec(a.shape, lambda g: (0,) * a.ndim)
    ps = pool_scale.reshape(1, G_W)
    sbc = sgu_b.T
    return pl.pallas_call(
        functools.partial(_pool_sgu_kernel, chunks_per_seq=seq // rows),
        out_shape=(jax.ShapeDtypeStruct((n, G_W), F32), jax.ShapeDtypeStruct((n, G_W), F32)),
        grid=(n // rows,),
        in_specs=[blk(COL_POOL),
                  pl.BlockSpec((POOL_HALO, G_W), lambda g: (jnp.maximum(g * halo_blocks - 1, 0), COL_POOL)),
                  blk(COL_SGU_U), blk(COL_SGU_V), whole(pool_w), whole(ps), whole(sgu_w), whole(sbc)],
        out_specs=(pl.BlockSpec((rows, G_W), lambda g: (g, 0)), pl.BlockSpec((rows, G_W), lambda g: (g, 0))),
        scratch_shapes=[pltpu.VMEM((POOL_HALO + rows, G_W), F32)],
        compiler_params=_params("parallel"),
        name="pool_sgu",
    )(proj, proj, proj, proj, pool_w, ps, sgu_w, sbc)


def _sb_attn_kernel(bias_ref, q_ref, k_ref, v_ref, o_ref, *, blk, scale):
    qi = pl.program_id(1)
    heads = lambda x: jnp.stack([x[:, h * HEAD_DIM:(h + 1) * HEAD_DIM] for h in range(HEADS)])
    q = heads(q_ref[...]).astype(BF16)
    bias = bias_ref[...]
    row = lax.broadcasted_iota(jnp.int32, (blk, blk), 0)
    col = lax.broadcasted_iota(jnp.int32, (blk, blk), 1)
    after = (row > col).astype(BF16)

    def visit(j, carry, valid):
        o, run = carry
        start = pl.multiple_of(j * blk, blk)
        kb = heads(k_ref[pl.ds(start, blk), :]).astype(BF16)
        vb = heads(v_ref[pl.ds(start, blk), :]).astype(BF16)
        z = lax.dot_general(q, kb, _B_NT, preferred_element_type=F32) * scale + bias
        sp = jnp.maximum(z, 0.0) + jnp.log(1.0 + jnp.exp(-jnp.abs(z)))
        log_fail = -sp if valid is None else jnp.where(valid, -sp, 0.0)
        hi = log_fail.reshape(HEADS * blk, blk).astype(BF16)
        later = jnp.dot(hi, after, preferred_element_type=F32).reshape(HEADS, blk, blk) + run
        att = jnp.exp(z - sp + later)
        if valid is not None:
            att = jnp.where(valid, att, 0.0)
        o = o + lax.dot_general(att.astype(BF16), vb, _B_NN, preferred_element_type=F32)
        run = run + jnp.sum(log_fail, axis=-1, keepdims=True)
        return o, run

    init = (jnp.zeros((HEADS, blk, HEAD_DIM), F32), jnp.zeros((HEADS, blk, 1), F32))
    carry = visit(qi, init, (col < row)[None])
    o, _ = lax.fori_loop(0, qi, lambda step, carry: visit(qi - 1 - step, carry, None), carry)
    for h in range(HEADS):
        o_ref[:, h * HEAD_DIM:(h + 1) * HEAD_DIM] = o[h]


def _sb_attn(proj, sb_bias, *, batch, seq, blk):
    n = proj.shape[0]
    nq = seq // blk
    return pl.pallas_call(
        functools.partial(_sb_attn_kernel, blk=blk, scale=HEAD_DIM ** -0.5),
        out_shape=jax.ShapeDtypeStruct((n, G_W), F32),
        grid=(batch, nq),
        in_specs=[pl.BlockSpec((HEADS, 1, 1), lambda b, i: (0, 0, 0)),
                  pl.BlockSpec((blk, G_W), lambda b, i: (b * nq + i, COL_SB_Q)),
                  pl.BlockSpec((seq, G_W), lambda b, i: (b, COL_SB_K)),
                  pl.BlockSpec((seq, G_W), lambda b, i: (b, COL_SB_V))],
        out_specs=pl.BlockSpec((blk, G_W), lambda b, i: (b * nq + i, 0)),
        compiler_params=_params("parallel", "arbitrary"),
        name="sb_attn",
    )(sb_bias.reshape(HEADS, 1, 1), proj, proj, proj)


def _sb_decode_body(q_ref, bias_ref, cum_ref, k_refs, v_refs, o_ref, *, width, scale):
    n_pages = len(k_refs)
    rows = 8
    q = q_ref[0]
    q_rows = jnp.concatenate([q[:, h * HEAD_DIM:(h + 1) * HEAD_DIM] for h in range(HEADS)]
                             + [jnp.zeros((rows - HEADS, HEAD_DIM), F32)], axis=0).astype(BF16)
    rid = lax.broadcasted_iota(jnp.int32, (rows, width), 0)
    cid = lax.broadcasted_iota(jnp.int32, (rows, width), 1)
    own = (cid & (HEADS - 1)) == rid

    z_rows = []
    for p in range(n_pages):
        zf = _bdot_nt(q_rows, k_refs[p][...])
        z_rows.append(jnp.sum(jnp.where(own, zf, 0.0), axis=0, keepdims=True))
    z = jnp.concatenate(z_rows, axis=0) * scale + bias_ref[...]
    sp = _softplus(z)
    hi, lo = _split_bf16(-sp)
    cum = (jnp.dot(hi, cum_ref[...], preferred_element_type=F32)
           + jnp.dot(lo, cum_ref[...], preferred_element_type=F32))
    in_page, page_tot = cum[:, :width], cum[:, width:]
    pr = lax.broadcasted_iota(jnp.int32, (n_pages, n_pages), 0)
    pc = lax.broadcasted_iota(jnp.int32, (n_pages, n_pages), 1)
    later_pages = _hdot((pc > pr).astype(F32), page_tot)
    att = jnp.exp(z - sp + in_page + later_pages)

    acc = jnp.zeros((rows, HEAD_DIM), F32)
    for p in range(n_pages):
        a_p = jnp.where(own, jnp.broadcast_to(att[p:p + 1, :], (rows, width)), 0.0)
        acc = acc + _bdot(a_p, v_refs[p][...])
    for h in range(HEADS):
        o_ref[0, :, h * HEAD_DIM:(h + 1) * HEAD_DIM] = acc[h:h + 1, :]


def _same_head_sums(width):
    src = np.arange(width)[:, None]
    dst = np.arange(width)[None, :]
    same = (src - dst) % HEADS == 0
    return jnp.asarray(np.concatenate([same & (src > dst), same], axis=1), BF16)


def _ffn_ln_decode_kernel(pt_ref, x_ref, wgu_ref, wo_ref, g_ref, b_ref, q_ref, bias_ref, cum_ref, k_hbm, v_hbm,
                          o_ref, y_ref, xb_ref, kbuf, vbuf, sem, *, alpha, tf, n_pages, width, scale, seqs, total):
    i, f = pl.program_id(0), pl.program_id(1)

    @pl.when(f == 0)
    def _():
        xb_ref[...] = x_ref[...].astype(BF16)
        o_ref[...] = jnp.zeros_like(o_ref)

    gate_up = jnp.dot(xb_ref[...], wgu_ref[...], preferred_element_type=F32)
    act = (_silu(gate_up[:, :tf]) * gate_up[:, tf:]).astype(BF16)
    o_ref[...] += jnp.dot(act, wo_ref[...], preferred_element_type=F32)

    def page_copies(t):
        slot = lax.rem(t, 2)
        copies = []
        for p in range(n_pages):
            row0 = pl.multiple_of(pt_ref[t * n_pages + p] * width, width)
            copies.append(pltpu.make_async_copy(k_hbm.at[pl.ds(row0, width), :], kbuf.at[slot, p], sem.at[slot]))
            copies.append(pltpu.make_async_copy(v_hbm.at[pl.ds(row0, width), :], vbuf.at[slot, p], sem.at[slot]))
        return copies

    @pl.when(f < seqs)
    def _():
        t = i * seqs + f

        @pl.when(t == 0)
        def _():
            for c in page_copies(t):
                c.start()

        @pl.when(t + 1 < total)
        def _():
            for c in page_copies(t + 1):
                c.start()

        for c in page_copies(t):
            c.wait()
        slot = lax.rem(t, 2)
        _sb_decode_body(q_ref, bias_ref, cum_ref, [kbuf.at[slot, p] for p in range(n_pages)],
                        [vbuf.at[slot, p] for p in range(n_pages)], y_ref, width=width, scale=scale)

    @pl.when(f == pl.num_programs(1) - 1)
    def _():
        o_ref[...] = _layer_norm(alpha * x_ref[...] + 0.5 * o_ref[...], g_ref[...], b_ref[...])


def _ffn_ln_decode(x, weights, ln_g, ln_b, proj3, bias_row, cache_k, cache_v, phys_pages, *,
                   layer, which, alpha, tm, tf, n_pages, width):
    n, d = x.shape
    bs = proj3.shape[0]
    w_gate_up, w_out = weights
    nf = w_out.shape[0] // tf
    tiles = n // tm
    seqs = bs // tiles
    assert bs == seqs * tiles and seqs <= nf
    seq = lambda i, f: i * seqs + jnp.minimum(f, seqs - 1)
    ln_spec = pl.BlockSpec((None, None, 1, d), lambda i, f, pt: (layer, which, 0, 0))
    grid_spec = pltpu.PrefetchScalarGridSpec(
        num_scalar_prefetch=1,
        grid=(tiles, nf),
        in_specs=[pl.BlockSpec((tm, d), lambda i, f, pt: (i, 0)),
                  pl.BlockSpec((d, 2 * tf), lambda i, f, pt: (0, f)),
                  pl.BlockSpec((tf, d), lambda i, f, pt: (f, 0)),
                  ln_spec, ln_spec,
                  pl.BlockSpec((1, 1, G_W), lambda i, f, pt: (seq(i, f), 0, COL_SB_Q)),
                  pl.BlockSpec((1, width), lambda i, f, pt: (0, 0)),
                  pl.BlockSpec((width, 2 * width), lambda i, f, pt: (0, 0), pipeline_mode=pl.Buffered(1)),
                  pl.BlockSpec(memory_space=pl.ANY), pl.BlockSpec(memory_space=pl.ANY)],
        out_specs=(pl.BlockSpec((tm, d), lambda i, f, pt: (i, 0)),
                   pl.BlockSpec((1, 1, G_W), lambda i, f, pt: (seq(i, f), 0, 0))),
        scratch_shapes=[pltpu.VMEM((tm, d), BF16),
                        pltpu.VMEM((2, n_pages, width, HEAD_DIM), F32),
                        pltpu.VMEM((2, n_pages, width, HEAD_DIM), F32),
                        pltpu.SemaphoreType.DMA((2,))],
    )
    out, y_sb = pl.pallas_call(
        functools.partial(_ffn_ln_decode_kernel, alpha=alpha, tf=tf, n_pages=n_pages, width=width,
                          scale=HEAD_DIM ** -0.5, seqs=seqs, total=bs),
        out_shape=(jax.ShapeDtypeStruct((n, d), F32), jax.ShapeDtypeStruct((bs, 1, G_W), F32)),
        grid_spec=grid_spec,
        compiler_params=pltpu.CompilerParams(dimension_semantics=("arbitrary", "arbitrary"),
                                             vmem_limit_bytes=FUSED_VMEM_LIMIT_BYTES),
        name="ffn_ln_decode",
    )(phys_pages, x, w_gate_up, w_out,
      ln_g.reshape(ln_g.shape[0], ln_g.shape[1], 1, d), ln_b.reshape(ln_b.shape[0], ln_b.shape[1], 1, d),
      proj3, bias_row, _same_head_sums(width), cache_k, cache_v)
    return out, y_sb.reshape(bs, G_W)


def _beta_and_log_decay(raw, neg_a_row, dt_row):
    lane = lax.broadcasted_iota(jnp.int32, raw.shape, 1)
    beta = jax.nn.sigmoid(raw)
    g = neg_a_row * _softplus(raw + dt_row)
    return jnp.where(lane < HEADS, beta, g)


def _l2_normalize(x):
    return x * lax.rsqrt(jnp.sum(x * x, -1, keepdims=True) + NORM_EPS)


def _dn_prep_kernel(raw_ref, halo_ref, bgr_ref, cw_ref, na_ref, dt_ref,
                    q_ref, k_ref, v_ref, bg_ref, bgt_ref, ext_ref, *, tiles_per_seq, rows):
    first = (pl.program_id(0) % tiles_per_seq) == 0
    ext_ref[0:CONV_HALO, :] = jnp.where(first, 0.0, halo_ref[...])
    ext_ref[CONV_HALO:, :] = raw_ref[...]
    conv = None
    for j in range(CONV_W):
        off = CONV_HALO - (CONV_W - 1) + j
        term = ext_ref[off:off + rows, :] * cw_ref[j:j + 1, :]
        conv = term if conv is None else conv + term
    act = _silu(conv)
    for h in range(HEADS):
        cols = slice(h * HEAD_DIM, (h + 1) * HEAD_DIM)
        q_ref[:, cols] = _l2_normalize(act[:, h * HEAD_DIM:(h + 1) * HEAD_DIM]) * (HEAD_DIM ** -0.5)
        k_ref[:, cols] = _l2_normalize(act[:, G_W + h * HEAD_DIM:G_W + (h + 1) * HEAD_DIM])
    v_ref[...] = act[:, 2 * G_W:]
    bg = _beta_and_log_decay(bgr_ref[...], na_ref[...], dt_ref[...])
    r = lax.broadcasted_iota(jnp.int32, (rows, rows), 0)
    s = lax.broadcasted_iota(jnp.int32, (rows, rows), 1)
    shift = DN_CHUNK.bit_length() - 1
    chunk_prefix = ((r >= s) & (lax.shift_right_logical(r, shift) == lax.shift_right_logical(s, shift)))
    lane = lax.broadcasted_iota(jnp.int32, bg.shape, 1)
    bg = jnp.where(lane < HEADS, bg, _hdot(chunk_prefix.astype(F32), bg))
    bg_ref[...] = bg
    per_tile = BG_LANES // DN_CHUNK
    for t in range(rows // BG_LANES):
        tile_t = bg[t * BG_LANES:(t + 1) * BG_LANES, :].T
        for c in range(per_tile):
            bgt_ref[t * per_tile + c] = tile_t[0:2 * HEADS, c * DN_CHUNK:(c + 1) * DN_CHUNK]


def _dn_prep(proj, bg_raw, conv_w, neg_a_row, dt_row, *, seq, rows):
    n = proj.shape[0]
    halo_blocks = rows // CONV_HALO
    qkv_w = 3 * G_W
    col = COL_DN_QKV * G_W // qkv_w
    whole = lambda a: pl.BlockSpec(a.shape, lambda g: (0,) * a.ndim)
    out = jax.ShapeDtypeStruct((n, G_W), F32)
    o_spec = pl.BlockSpec((rows, G_W), lambda g: (g, 0))
    return pl.pallas_call(
        functools.partial(_dn_prep_kernel, tiles_per_seq=seq // rows, rows=rows),
        out_shape=(out, out, out, jax.ShapeDtypeStruct((n, BG_LANES), F32),
                   jax.ShapeDtypeStruct((n // DN_CHUNK, 2 * HEADS, DN_CHUNK), F32)),
        grid=(n // rows,),
        in_specs=[pl.BlockSpec((rows, qkv_w), lambda g: (g, col)),
                  pl.BlockSpec((CONV_HALO, qkv_w), lambda g: (jnp.maximum(g * halo_blocks - 1, 0), col)),
                  pl.BlockSpec((rows, BG_LANES), lambda g: (g, 0)),
                  whole(conv_w), whole(neg_a_row), whole(dt_row)],
        out_specs=(o_spec, o_spec, o_spec, pl.BlockSpec((rows, BG_LANES), lambda g: (g, 0)),
                   pl.BlockSpec((rows // DN_CHUNK, 2 * HEADS, DN_CHUNK), lambda g: (g, 0, 0))),
        scratch_shapes=[pltpu.VMEM((CONV_HALO + rows, qkv_w), F32)],
        compiler_params=_params("parallel"),
        name="dn_prep",
    )(proj, proj, bg_raw, conv_w, neg_a_row, dt_row)


def _cat_lhs(x, axis):
    hi = x.astype(BF16).astype(F32)
    return jnp.concatenate([hi, x - hi, hi], axis=axis).astype(BF16)


def _cat_rhs(x, axis):
    hi = x.astype(BF16)
    lo = (x - hi.astype(F32)).astype(BF16)
    return jnp.concatenate([hi, hi, lo], axis=axis)


def _mm(lhs_cat, rhs_cat, dims=_B_NN):
    return lax.dot_general(lhs_cat, rhs_cat, dims, preferred_element_type=F32)


def _unit_lower_inverse(a_mat, eye):
    n = a_mat.shape[-1]
    inv = eye - a_mat
    power = _mm(_cat_lhs(a_mat, 2), _cat_rhs(a_mat, 1))
    span = 2
    while span < n:
        p_rhs = _cat_rhs(power, 1)
        inv = inv + _mm(_cat_lhs(inv, 2), p_rhs)
        span *= 2
        if span < n:
            power = _mm(_cat_lhs(power, 2), p_rhs)
    return inv


def _dn_out(o, z, norm_g):
    o = o * lax.rsqrt(jnp.mean(o * o, -1, keepdims=True) + NORM_EPS) * norm_g
    return o * _silu(z)


def _dn_scan_kernel(q_ref, k_ref, v_ref, z_ref, bg_ref, bgt_ref, ng_ref, y_ref, s_out_ref, s_ref, *, chunks):
    t = pl.program_id(1)

    @pl.when(t == 0)
    def _():
        s_ref[...] = jnp.zeros_like(s_ref)

    c, d = DN_CHUNK, HEAD_DIM
    units = [(ci, h) for ci in range(chunks) for h in range(HEADS)]
    rows = lambda ci: slice(ci * c, (ci + 1) * c)
    cols = lambda h: slice(h * d, (h + 1) * d)
    gather = lambda ref: jnp.stack([ref[rows(ci), cols(h)] for ci, h in units])
    q, k, v = gather(q_ref), gather(k_ref), gather(v_ref)
    beta = jnp.stack([bg_ref[rows(ci), h:h + 1] for ci, h in units])
    g_col = jnp.stack([bg_ref[rows(ci), HEADS + h:HEADS + h + 1] for ci, h in units])
    g_row = jnp.stack([bgt_ref[ci, HEADS + h:HEADS + h + 1, :] for ci, h in units])

    r = lax.broadcasted_iota(jnp.int32, (c, c), 0)
    s_ = lax.broadcasted_iota(jnp.int32, (c, c), 1)
    incl, strict, eye = (r >= s_)[None], (r > s_)[None], (r == s_).astype(F32)[None]
    decay = jnp.where(incl, jnp.exp(jnp.where(incl, g_col - g_row, 0.0)), 0.0)
    qk_hi, qk_lo = _split_bf16(jnp.concatenate([q, k], axis=1))
    k_hi, k_lo = _split_bf16(k)
    qk_kk = (_mm(jnp.concatenate([qk_hi, qk_lo], axis=2), jnp.concatenate([k_hi, k_hi], axis=2), _B_NT)
             + _mm(qk_hi, k_lo, _B_NT))
    qk = qk_kk[:, :c] * decay
    a_mat = jnp.where(strict, beta * qk_kk[:, c:] * decay, 0.0)
    inv = _unit_lower_inverse(a_mat, eye)
    e_g = jnp.exp(g_col)
    rhs = jnp.concatenate([beta * v, (beta * e_g) * k], axis=2)
    sol = _mm(_cat_lhs(inv, 2), _cat_rhs(rhs, 1))
    u_base, w_mat = sol[:, :, :d], sol[:, :, d:]
    w_hi, w_lo = _split_bf16(w_mat)
    w_cat = jnp.concatenate([w_hi, w_lo], axis=2)
    q_dec = (q * e_g).astype(BF16)
    g_last = g_col[:, c - 1:c, :]
    k_dec = k * jnp.exp(g_last - g_col)
    keep = jnp.exp(g_last)

    norm_g = ng_ref[...]
    s = s_ref[...]
    for ci in range(chunks):
        sl = slice(ci * HEADS, (ci + 1) * HEADS)
        s_hi, s_lo = _split_bf16(s)
        u = u_base[sl] - (_mm(w_cat[sl], jnp.concatenate([s_hi, s_hi], axis=1)) + _mm(w_hi[sl], s_lo))
        o = _mm(q_dec[sl], s_hi) + _mm(qk[sl].astype(BF16), u.astype(BF16))
        s = s * keep[sl] + _mm(_cat_lhs(k_dec[sl], 1), _cat_rhs(u, 1), _B_TN)
        for h in range(HEADS):
            y_ref[rows(ci), cols(h)] = _dn_out(o[h], z_ref[rows(ci), cols(h)], norm_g)
    s_ref[...] = s

    @pl.when(t == pl.num_programs(1) - 1)
    def _():
        s_out_ref[0] = s_ref[...]


def _dn_scan(q, k, v, proj, bg, bgt, norm_g, *, batch, seq, chunks):
    n = q.shape[0]
    rows = chunks * DN_CHUNK
    steps = seq // rows
    spec = pl.BlockSpec((rows, G_W), lambda b, t: (b * steps + t, 0))
    ng = norm_g.reshape(1, HEAD_DIM)
    return pl.pallas_call(
        functools.partial(_dn_scan_kernel, chunks=chunks),
        out_shape=(jax.ShapeDtypeStruct((n, G_W), F32),
                   jax.ShapeDtypeStruct((batch, HEADS, HEAD_DIM, HEAD_DIM), F32)),
        grid=(batch, steps),
        in_specs=[spec, spec, spec,
                  pl.BlockSpec((rows, G_W), lambda b, t: (b * steps + t, COL_DN_Z)),
                  pl.BlockSpec((rows, BG_LANES), lambda b, t: (b * steps + t, 0)),
                  pl.BlockSpec((chunks, 2 * HEADS, DN_CHUNK), lambda b, t: (b * steps + t, 0, 0)),
                  pl.BlockSpec((1, HEAD_DIM), lambda b, t: (0, 0))],
        out_specs=(spec, pl.BlockSpec((1, HEADS, HEAD_DIM, HEAD_DIM), lambda b, t: (b, 0, 0, 0))),
        scratch_shapes=[pltpu.VMEM((HEADS, HEAD_DIM, HEAD_DIM), F32)],
        compiler_params=_params("parallel", "arbitrary"),
        name="dn_scan",
    )(q, k, v, proj, bg, bgt, ng)


def _sample_mix_kernel(pool_ref, u_ref, v_ref, qkv_ref, z_ref, bgr_ref, sp_ref, sc_ref, sd_ref,
                       pw_ref, ps_ref, sw0_ref, sb0_ref, cw_ref, na_ref, dt_ref, ng_ref, *rest, bt, pos0):
    yp_ref, ys_ref, yd_ref, np_ref, nc_ref, nd_ref = rest[-6:]
    new = pool_ref[...]
    for gi, w in enumerate(POOL_WINDOWS):
        cols = slice(gi * HEAD_DIM, (gi + 1) * HEAD_DIM)
        s = new[:, cols]
        for j in range(1, w):
            s = s + sp_ref[POOL_BUF - j, :, cols]
        d = s / float(min(pos0 + 1, w)) - new[:, cols]
        yp_ref[:, cols] = _hdot(d, pw_ref[gi]) * ps_ref[:, cols]
    np_ref[0:POOL_BUF - 1] = sp_ref[1:POOL_BUF]
    np_ref[POOL_BUF - 1] = new

    ys_ref[...] = u_ref[...] * (sw0_ref[...] * v_ref[...] + sb0_ref[...])

    raw = qkv_ref[...]
    conv = raw * cw_ref[CONV_W - 1:CONV_W, :]
    for j in range(CONV_W - 1):
        conv = conv + sc_ref[j] * cw_ref[j:j + 1, :]
    nc_ref[0:CONV_W - 2] = sc_ref[1:CONV_W - 1]
    nc_ref[CONV_W - 2] = raw
    act = _silu(conv)
    bg = _beta_and_log_decay(bgr_ref[...], na_ref[...], dt_ref[...])
    norm_g = ng_ref[...]
    d = HEAD_DIM
    pad = jnp.zeros((d - bt, d), F32)
    k_t, q_t = [], []
    for h in range(HEADS):
        k_t.append(jnp.concatenate([_l2_normalize(act[:, G_W + h * d:G_W + (h + 1) * d]), pad], axis=0).T)
        q_t.append(jnp.concatenate([_l2_normalize(act[:, h * d:(h + 1) * d]) * (d ** -0.5), pad], axis=0).T)
    pairs = [(b, h) for b in range(bt) for h in range(HEADS)]
    v_h = [act[:, 2 * G_W + h * d:2 * G_W + (h + 1) * d] for h in range(HEADS)]
    z_h = [z_ref[:, h * d:(h + 1) * d] for h in range(HEADS)]
    a = [jnp.exp(bg[b:b + 1, HEADS + h:HEADS + h + 1]) for b, h in pairs]
    k_col = [jnp.broadcast_to(k_t[h][:, b:b + 1], (d, d)) for b, h in pairs]
    k_s = [jnp.sum(k_col[i] * sd_ref[b, h], axis=0, keepdims=True) for i, (b, h) in enumerate(pairs)]
    u = [bg[b:b + 1, h:h + 1] * (v_h[h][b:b + 1, :] - a[i] * k_s[i]) for i, (b, h) in enumerate(pairs)]
    s1 = [a[i] * sd_ref[b, h] + k_col[i] * u[i] for i, (b, h) in enumerate(pairs)]
    for i, (b, h) in enumerate(pairs):
        nd_ref[b, h] = s1[i]
    o = [jnp.sum(jnp.broadcast_to(q_t[h][:, b:b + 1], (d, d)) * s1[i], axis=0, keepdims=True)
         for i, (b, h) in enumerate(pairs)]
    for i, (b, h) in enumerate(pairs):
        yd_ref[b:b + 1, h * d:(h + 1) * d] = _dn_out(o[i], z_h[h][b:b + 1, :], norm_g)


def _sample_mix(proj, bg_raw, state_pool, state_conv, state_delta, new_prev, pool_w, pool_scale, sgu_w0, sgu_b0,
                conv_w, neg_a_row, dt_row, norm_g, *, layer, bt, pos0):
    bs = proj.shape[0]
    qkv_w = 3 * G_W
    col = lambda c: pl.BlockSpec((bt, G_W), lambda i: (i, c))
    whole = lambda a: pl.BlockSpec(a.shape, lambda i: (0,) * a.ndim)
    row = pl.BlockSpec((bt, G_W), lambda i: (i, 0))
    ps = pool_scale.reshape(1, G_W)
    ng = norm_g.reshape(1, HEAD_DIM)
    consts = (pool_w, ps, sgu_w0, sgu_b0, conv_w, neg_a_row, dt_row, ng)
    out_sds = lambda a: jax.ShapeDtypeStruct(a.shape, F32)
    pool_spec = pl.BlockSpec((None, POOL_BUF, bt, G_W), lambda i: (layer, 0, i, 0))
    conv_spec = pl.BlockSpec((None, CONV_W - 1, bt, qkv_w), lambda i: (layer, 0, i, 0))
    delta_spec = pl.BlockSpec((None, bt, HEADS, HEAD_DIM, HEAD_DIM), lambda i: (layer, i, 0, 0, 0))
    y = jax.ShapeDtypeStruct((bs, G_W), F32)
    prev = () if new_prev is None else tuple(new_prev)
    n_in = 9 + len(consts)
    return pl.pallas_call(
        functools.partial(_sample_mix_kernel, bt=bt, pos0=pos0),
        out_shape=(y, y, y, out_sds(state_pool), out_sds(state_conv), out_sds(state_delta)),
        grid=(bs // bt,),
        in_specs=[col(COL_POOL), col(COL_SGU_U), col(COL_SGU_V),
                  pl.BlockSpec((bt, qkv_w), lambda i: (i, COL_DN_QKV * G_W // qkv_w)),
                  col(COL_DN_Z),
                  pl.BlockSpec((bt, BG_LANES), lambda i: (i, 0)),
                  pool_spec, conv_spec, delta_spec] + [whole(a) for a in consts]
                 + [pl.BlockSpec(memory_space=pl.ANY)] * len(prev),
        out_specs=(row, row, row, pool_spec, conv_spec, delta_spec),
        input_output_aliases={n_in + k: 3 + k for k in range(len(prev))},
        compiler_params=_params("parallel"),
        name="sample_mix",
    )(proj, proj, proj, proj, proj, bg_raw, state_pool, state_conv, state_delta, *consts, *prev)


def _lane_row(values, offset):
    return jnp.zeros((1, BG_LANES), F32).at[0, offset:offset + values.shape[0]].set(values)


def kernel(x_prompt, x_sample, cache_k, cache_v, page_table, state_pool, state_conv, state_delta,
           ln_g, ln_b, w_ffn1_in, w_ffn1_out, w_ffn2_in, w_ffn2_out, w_in, w_out,
           pool_w, pool_scale, sgu_w, sgu_b, sb_bias, dn_conv_w, dn_a_log, dn_dt_bias, dn_norm_g):
    depth = ln_g.shape[0]
    alpha = (2.0 * depth) ** 0.25
    bp, seq, d_model = x_prompt.shape
    bs, dec_seq, _ = x_sample.shape
    assert dec_seq == 1
    n_phys, page = cache_k.shape[1], cache_k.shape[2]
    n_pages = page_table.shape[1]
    pos0 = n_pages * page
    ck = cache_k.reshape(depth * n_phys * page * HEADS, HEAD_DIM)
    cv = cache_v.reshape(depth * n_phys * page * HEADS, HEAD_DIM)
    hp = x_prompt.reshape(bp * seq, d_model)
    hs = x_sample.reshape(bs, d_model)
    pool_tm = jnp.swapaxes(state_pool, 1, 2)
    conv_tm = jnp.swapaxes(state_conv, 1, 2)
    tm_p = 512
    tm_s = bs
    tail_w = w_in.shape[2] - N_MAIN_COLS
    w_in_tail = jnp.pad(w_in[:, :, N_MAIN_COLS:].astype(BF16), ((0, 0), (0, 0), (0, BG_LANES - tail_w)))
    w_in_t = jnp.swapaxes(w_in, 1, 2)
    w_out_bf = w_out.astype(BF16)

    outs = {name: [] for name in ("poolp", "convp", "sp", "ks", "vs", "sgus")}
    kv_rows = None
    new_states = None
    for l in range(depth):
        neg_a_row = _lane_row(-jnp.exp(dn_a_log[l]), HEADS)
        dt_row = _lane_row(dn_dt_bias[l], HEADS)
        ffn = functools.partial(_ffn_ln, ln_g=ln_g, ln_b=ln_b, layer=l, alpha=alpha, tf=512)
        out_proj = functools.partial(_out_proj_ln, w_out=w_out_bf, ln_g=ln_g, ln_b=ln_b, layer=l, alpha=alpha)

        g1, *w1 = ffn(hs, (w_ffn1_in, w_ffn1_out), which=0, tm=tm_s)
        sproj, sbg_raw, sk_rows, sv_rows, wi_main = _in_proj_cast(g1, w_in_t, w_in_tail, layer=l)
        sgu_w0 = jnp.repeat(sgu_w[l, :, 0, 0], HEAD_DIM).reshape(1, G_W)
        sgu_b0 = jnp.repeat(sgu_b[l, :, 0], HEAD_DIM).reshape(1, G_W)
        sy_pool, sy_sgu, sy_dn, *new_states = _sample_mix(
            sproj, sbg_raw, pool_tm, conv_tm, state_delta, new_states, pool_w[l], pool_scale[l],
            sgu_w0, sgu_b0, dn_conv_w[l], neg_a_row, dt_row, dn_norm_g[l], layer=l, bt=8, pos0=pos0)
        bias_row = jnp.tile(sb_bias[l], page).reshape(1, page * HEADS)
        phys = (page_table + l * n_phys).reshape(-1)
        h1, sy_sb = _ffn_ln_decode(hp, tuple(w1), ln_g, ln_b, sproj.reshape(bs, 1, N_MAIN_COLS), bias_row, ck, cv, phys,
                                   layer=l, which=0, alpha=alpha, tm=tm_p, tf=512,
                                   n_pages=n_pages, width=page * HEADS)
        g2 = out_proj(g1, (sy_pool, sy_sgu, sy_sb, sy_dn), tm=tm_s)
        hs, *w2 = ffn(g2, (w_ffn2_in, w_ffn2_out), which=2, tm=tm_s)
        outs["ks"].append(sk_rows.reshape(bs, 1, HEADS, HEAD_DIM))
        outs["vs"].append(sv_rows.reshape(bs, 1, HEADS, HEAD_DIM))
        outs["sgus"].append(sproj[:, COL_SGU_V * G_W:(COL_SGU_V + 1) * G_W].reshape(bs, 1, G_W))

        proj, bg_raw, *kv_rows = _in_proj(h1, wi_main, w_in_tail, kv_rows, layer=l, depth=depth, tm=256)
        y_pool, y_sgu = _pool_sgu(proj, pool_w[l], pool_scale[l], sgu_w[l], sgu_b[l], seq=seq)
        y_sb = _sb_attn(proj, sb_bias[l], batch=bp, seq=seq, blk=256)
        dq, dk, dv, bg, bgt = _dn_prep(proj, bg_raw, dn_conv_w[l], neg_a_row, dt_row, seq=seq, rows=256)
        y_dn, s_end = _dn_scan(dq, dk, dv, proj, bg, bgt, dn_norm_g[l], batch=bp, seq=seq, chunks=4)
        h2 = out_proj(h1, (y_pool, y_sgu, y_sb, y_dn), tm=256)
        hp, = ffn(h2, tuple(w2), which=2, tm=tm_p)
        proj3 = proj.reshape(bp, seq, N_MAIN_COLS)
        outs["poolp"].append(proj3[:, seq - POOL_BUF:, :G_W])
        outs["convp"].append(proj3[:, seq - (CONV_W - 1):, COL_DN_QKV * G_W:(COL_DN_QKV + 3) * G_W])
        outs["sp"].append(s_end)

    st = lambda name: jnp.stack(outs[name])
    new_pool, new_conv, new_delta = new_states
    kp, vp = (a.reshape(depth, bp, seq, HEADS, HEAD_DIM) for a in kv_rows)
    return (hp.reshape(bp, seq, d_model), hs.reshape(bs, 1, d_model),
            kp, vp, st("poolp"), st("convp"), st("sp"),
            st("ks"), st("vs"), jnp.swapaxes(new_pool, 1, 2), jnp.swapaxes(new_conv, 1, 2),
            new_delta, st("sgus"))
```

```python
import functools

import jax
import jax.numpy as jnp
import numpy as np
from jax import lax
from jax.experimental import pallas as pl
from jax.experimental.pallas import tpu as pltpu

F32 = jnp.float32
BF16 = jnp.bfloat16
HIGHEST = lax.Precision.HIGHEST

HEADS = 4
HEAD_DIM = 128
G_W = HEADS * HEAD_DIM
POOL_WINDOWS = (2, 4, 8, 16)
POOL_BUF = max(POOL_WINDOWS) - 1
POOL_HALO = 16
SGU_CHUNK = 128
DN_CHUNK = 64
CONV_W = 4
CONV_HALO = 8
LN_EPS = 1e-5
NORM_EPS = 1e-6
VMEM_LIMIT_BYTES = 48 * 1024 * 1024
FUSED_VMEM_LIMIT_BYTES = 58 * 1024 * 1024

COL_POOL, COL_SGU_U, COL_SGU_V, COL_SB_Q, COL_SB_K, COL_SB_V, COL_DN_QKV, COL_DN_Z = 0, 1, 2, 3, 4, 5, 6, 9
N_MAIN_COLS = 10 * G_W
BG_LANES = 128


def _params(*sem):
    return pltpu.CompilerParams(dimension_semantics=sem, vmem_limit_bytes=VMEM_LIMIT_BYTES)


def _hdot(a, b):
    return jnp.dot(a, b, precision=HIGHEST, preferred_element_type=F32)


def _bdot(a, b):
    return jnp.dot(a.astype(BF16), b.astype(BF16), preferred_element_type=F32)


def _bdot_nt(a, b):
    return lax.dot_general(a.astype(BF16), b.astype(BF16), (((1,), (1,)), ((), ())),
                           preferred_element_type=F32)


def _silu(x):
    return x * jax.nn.sigmoid(x)


def _softplus(x):
    return jnp.maximum(x, 0.0) + jnp.log1p(jnp.exp(-jnp.abs(x)))


def _layer_norm(y, g, b):
    mu = jnp.mean(y, -1, keepdims=True)
    d = y - mu
    var = jnp.mean(d * d, -1, keepdims=True)
    return d * lax.rsqrt(var + LN_EPS) * g + b


def _split_bf16(x):
    hi = x.astype(BF16)
    lo = (x - hi.astype(F32)).astype(BF16)
    return hi, lo


_B_NN = (((2,), (1,)), ((0,), (0,)))
_B_NT = (((2,), (2,)), ((0,), (0,)))
_B_TN = (((1,), (1,)), ((0,), (0,)))


def _ffn_ln_kernel(x_ref, *refs, alpha, emit_bf16, tf):
    if emit_bf16:
        wg_ref, wu_ref, wo_ref, g_ref, b_ref, o_ref, wgu_out, wo_out, xb_ref, acc_ref = refs
    else:
        wgu_ref, wo_ref, g_ref, b_ref, o_ref, xb_ref, acc_ref = refs
    f = pl.program_id(1)

    @pl.when(f == 0)
    def _():
        xb_ref[...] = x_ref[...].astype(BF16)
        acc_ref[...] = jnp.zeros_like(acc_ref)

    xb = xb_ref[...]
    if emit_bf16:
        wg, wu, wo = wg_ref[...].astype(BF16), wu_ref[...].astype(BF16), wo_ref[...].astype(BF16)
        wgu_out[:, :tf], wgu_out[:, tf:], wo_out[...] = wg, wu, wo
        gate = jnp.dot(xb, wg, preferred_element_type=F32)
        up = jnp.dot(xb, wu, preferred_element_type=F32)
    else:
        wo = wo_ref[...]
        gate_up = jnp.dot(xb, wgu_ref[...], preferred_element_type=F32)
        gate, up = gate_up[:, :tf], gate_up[:, tf:]
    act = (_silu(gate) * up).astype(BF16)
    acc_ref[...] += jnp.dot(act, wo, preferred_element_type=F32)

    @pl.when(f == pl.num_programs(1) - 1)
    def _():
        o_ref[...] = _layer_norm(alpha * x_ref[...] + 0.5 * acc_ref[...], g_ref[...], b_ref[...])


def _ffn_ln(x, weights, ln_g, ln_b, *, layer, which, alpha, tm, tf):
    n, d = x.shape
    emit_bf16 = weights[0].dtype == F32
    if emit_bf16:
        w_in, w_out = weights
        dff = w_out.shape[1]
        nf = dff // tf
        operands = (w_in, w_in, w_out)
        w_specs = [pl.BlockSpec((None, d, tf), lambda i, f: (layer, 0, f)),
                   pl.BlockSpec((None, d, tf), lambda i, f: (layer, 0, f + nf)),
                   pl.BlockSpec((None, tf, d), lambda i, f: (layer, f, 0))]
    else:
        operands = weights
        dff = weights[1].shape[0]
        nf = dff // tf
        w_specs = [pl.BlockSpec((d, 2 * tf), lambda i, f: (0, f)),
                   pl.BlockSpec((tf, d), lambda i, f: (f, 0))]
    ln_spec = pl.BlockSpec((None, None, 1, d), lambda i, f: (layer, which, 0, 0))
    out_shape = [jax.ShapeDtypeStruct((n, d), F32)]
    out_specs = [pl.BlockSpec((tm, d), lambda i, f: (i, 0))]
    if emit_bf16:
        assert n == tm
        out_shape += [jax.ShapeDtypeStruct((d, 2 * dff), BF16), jax.ShapeDtypeStruct((dff, d), BF16)]
        out_specs += [pl.BlockSpec((d, 2 * tf), lambda i, f: (0, f)), pl.BlockSpec((tf, d), lambda i, f: (f, 0))]
    return pl.pallas_call(
        functools.partial(_ffn_ln_kernel, alpha=alpha, emit_bf16=emit_bf16, tf=tf),
        out_shape=tuple(out_shape),
        grid=(n // tm, nf),
        in_specs=[pl.BlockSpec((tm, d), lambda i, f: (i, 0))] + w_specs + [ln_spec, ln_spec],
        out_specs=tuple(out_specs),
        scratch_shapes=[pltpu.VMEM((tm, d), BF16), pltpu.VMEM((tm, d), F32)],
        compiler_params=_params("parallel", "arbitrary"),
        name="ffn_ln_cast" if emit_bf16 else "ffn_ln",
    )(x, *operands, ln_g.reshape(ln_g.shape[0], ln_g.shape[1], 1, d), ln_b.reshape(ln_b.shape[0], ln_b.shape[1], 1, d))


def _store_head_rows(dst_ref, block, tm):
    for h in range(HEADS):
        dst_ref[pl.ds(h, tm, stride=HEADS), :] = block[:, h * HEAD_DIM:(h + 1) * HEAD_DIM]


def _in_proj_kernel(*refs, tm, aliased):
    h_ref, w_ref, wt_ref = refs[:3]
    o_ref, bg_ref, k_ref, v_ref = refs[3 + aliased:]
    hb = h_ref[...].astype(BF16)
    bg_ref[...] = jnp.dot(hb, wt_ref[...], preferred_element_type=F32)
    o = jnp.dot(hb, w_ref[...], preferred_element_type=F32)
    o_ref[...] = o
    _store_head_rows(k_ref, o[:, COL_SB_K * G_W:(COL_SB_K + 1) * G_W], tm)
    _store_head_rows(v_ref, o[:, COL_SB_V * G_W:(COL_SB_V + 1) * G_W], tm)


def _in_proj(h, w_main, w_tail, kv_prev, *, layer, depth, tm):
    n, d = h.shape
    steps = n // tm
    resident = pl.Buffered(1)
    rows_kv = jax.ShapeDtypeStruct((depth * n * HEADS, HEAD_DIM), F32)
    kv_spec = pl.BlockSpec((tm * HEADS, HEAD_DIM), lambda i: (layer * steps + i, 0))
    aliased = 0 if kv_prev is None else 2
    prev = () if kv_prev is None else tuple(kv_prev)
    return pl.pallas_call(
        functools.partial(_in_proj_kernel, tm=tm, aliased=aliased),
        out_shape=(jax.ShapeDtypeStruct((n, N_MAIN_COLS), F32), jax.ShapeDtypeStruct((n, BG_LANES), F32),
                   rows_kv, rows_kv),
        grid=(steps,),
        in_specs=[
            pl.BlockSpec((tm, d), lambda i: (i, 0)),
            pl.BlockSpec((d, N_MAIN_COLS), lambda i: (0, 0), pipeline_mode=resident),
            pl.BlockSpec((None, d, BG_LANES), lambda i: (layer, 0, 0), pipeline_mode=resident),
        ] + [pl.BlockSpec(memory_space=pl.ANY)] * aliased,
        out_specs=(pl.BlockSpec((tm, N_MAIN_COLS), lambda i: (i, 0)),
                   pl.BlockSpec((tm, BG_LANES), lambda i: (i, 0)), kv_spec, kv_spec),
        input_output_aliases={3: 2, 4: 3} if aliased else {},
        compiler_params=_params("parallel"),
        name="in_proj",
    )(h, w_main, w_tail, *prev)


def _in_proj_cast_kernel(h_ref, w_ref, wt_ref, o_ref, bg_ref, k_ref, v_ref, wb_ref, *, tm):
    j = pl.program_id(0)
    hb = h_ref[...].astype(BF16)
    wb = w_ref[...].T.astype(BF16)
    wb_ref[...] = wb
    o = jnp.dot(hb, wb, preferred_element_type=F32)
    o_ref[...] = o

    @pl.when(j == 0)
    def _():
        bg_ref[...] = jnp.dot(hb, wt_ref[...], preferred_element_type=F32)

    @pl.when(j == COL_SB_K)
    def _():
        _store_head_rows(k_ref, o, tm)

    @pl.when(j == COL_SB_V)
    def _():
        _store_head_rows(v_ref, o, tm)


def _in_proj_cast(h, w_in_t, w_tail, *, layer):
    n, d = h.shape
    rows_kv = jax.ShapeDtypeStruct((n * HEADS, HEAD_DIM), F32)
    kv_spec = pl.BlockSpec((n * HEADS, HEAD_DIM), lambda j: (0, 0))
    return pl.pallas_call(
        functools.partial(_in_proj_cast_kernel, tm=n),
        out_shape=(jax.ShapeDtypeStruct((n, N_MAIN_COLS), F32), jax.ShapeDtypeStruct((n, BG_LANES), F32),
                   rows_kv, rows_kv, jax.ShapeDtypeStruct((d, N_MAIN_COLS), BF16)),
        grid=(N_MAIN_COLS // G_W,),
        in_specs=[
            pl.BlockSpec((n, d), lambda j: (0, 0)),
            pl.BlockSpec((None, G_W, d), lambda j: (layer, j, 0)),
            pl.BlockSpec((None, d, BG_LANES), lambda j: (layer, 0, 0)),
        ],
        out_specs=(pl.BlockSpec((n, G_W), lambda j: (0, j)),
                   pl.BlockSpec((n, BG_LANES), lambda j: (0, 0)), kv_spec, kv_spec,
                   pl.BlockSpec((d, G_W), lambda j: (0, j))),
        compiler_params=_params("arbitrary"),
        name="in_proj_cast",
    )(h, w_in_t, w_tail)


def _out_proj_ln_kernel(h_ref, y0_ref, y1_ref, y2_ref, y3_ref, w_ref, g_ref, b_ref, o_ref, *, alpha):
    mixed = jnp.concatenate([y_ref[...].astype(BF16) for y_ref in (y0_ref, y1_ref, y2_ref, y3_ref)], axis=1)
    acc = jnp.dot(mixed, w_ref[...], preferred_element_type=F32)
    o_ref[...] = _layer_norm(alpha * h_ref[...] + acc, g_ref[...], b_ref[...])


def _out_proj_ln(h, ys, w_out, ln_g, ln_b, *, layer, alpha, tm):
    n, d = h.shape
    y_spec = pl.BlockSpec((tm, G_W), lambda i: (i, 0))
    ln_spec = pl.BlockSpec((None, None, 1, d), lambda i: (layer, 1, 0, 0))
    return pl.pallas_call(
        functools.partial(_out_proj_ln_kernel, alpha=alpha),
        out_shape=jax.ShapeDtypeStruct((n, d), F32),
        grid=(n // tm,),
        in_specs=[pl.BlockSpec((tm, d), lambda i: (i, 0)), y_spec, y_spec, y_spec, y_spec,
                  pl.BlockSpec((None, 4 * G_W, d), lambda i: (layer, 0, 0)), ln_spec, ln_spec],
        out_specs=pl.BlockSpec((tm, d), lambda i: (i, 0)),
        compiler_params=_params("parallel"),
        name="out_proj_ln",
    )(h, *ys, w_out, ln_g.reshape(ln_g.shape[0], ln_g.shape[1], 1, d), ln_b.reshape(ln_b.shape[0], ln_b.shape[1], 1, d))


def _pool_sgu_kernel(cur_ref, halo_ref, u_ref, v_ref, pw_ref, ps_ref, sw_ref, sbc_ref,
                     yp_ref, ys_ref, ext_ref, *, tiles_per_seq, rows):
    c = pl.program_id(0) % tiles_per_seq
    cur = cur_ref[...]
    ext_ref[0:POOL_HALO, :] = jnp.where(c == 0, 0.0, halo_ref[...])
    ext_ref[POOL_HALO:, :] = cur
    pos = c * rows + lax.broadcasted_iota(jnp.int32, (rows, 1), 0)
    for gi, w in enumerate(POOL_WINDOWS):
        cols = slice(gi * HEAD_DIM, (gi + 1) * HEAD_DIM)
        s = cur[:, cols]
        for j in range(1, w):
            s = s + ext_ref[POOL_HALO - j:POOL_HALO - j + rows, cols]
        cnt = jnp.minimum(pos + 1, w).astype(F32)
        d = s / cnt - cur[:, cols]
        yp_ref[:, cols] = _bdot(d, pw_ref[gi]) * ps_ref[:, cols]

    r = lax.broadcasted_iota(jnp.int32, (SGU_CHUNK, SGU_CHUNK), 0)
    s_ = lax.broadcasted_iota(jnp.int32, (SGU_CHUNK, SGU_CHUNK), 1)
    causal = r >= s_
    for h in range(HEADS):
        cols = slice(h * HEAD_DIM, (h + 1) * HEAD_DIM)
        wm = jnp.where(causal, sw_ref[h], 0.0).astype(BF16)
        for ci in range(rows // SGU_CHUNK):
            chunk = slice(ci * SGU_CHUNK, (ci + 1) * SGU_CHUNK)
            mixed = _bdot(wm, v_ref[chunk, cols]) + sbc_ref[:, h:h + 1]
            ys_ref[chunk, cols] = u_ref[chunk, cols] * mixed


def _pool_sgu(proj, pool_w, pool_scale, sgu_w, sgu_b, *, seq, rows):
    n = proj.shape[0]
    assert rows % SGU_CHUNK == 0 and seq % rows == 0
    halo_blocks = rows // POOL_HALO
    blk = lambda col: pl.BlockSpec((rows, G_W), lambda g: (g, col))
    whole = lambda a: pl.BlockSpec(a.shape, lambda g: (0,) * a.ndim)
    ps = pool_scale.reshape(1, G_W)
    sbc = sgu_b.T
    return pl.pallas_call(
        functools.partial(_pool_sgu_kernel, tiles_per_seq=seq // rows, rows=rows),
        out_shape=(jax.ShapeDtypeStruct((n, G_W), F32), jax.ShapeDtypeStruct((n, G_W), F32)),
        grid=(n // rows,),
        in_specs=[blk(COL_POOL),
                  pl.BlockSpec((POOL_HALO, G_W), lambda g: (jnp.maximum(g * halo_blocks - 1, 0), COL_POOL)),
                  blk(COL_SGU_U), blk(COL_SGU_V), whole(pool_w), whole(ps), whole(sgu_w), whole(sbc)],
        out_specs=(pl.BlockSpec((rows, G_W), lambda g: (g, 0)), pl.BlockSpec((rows, G_W), lambda g: (g, 0))),
        scratch_shapes=[pltpu.VMEM((POOL_HALO + rows, G_W), F32)],
        compiler_params=_params("parallel"),
        name="pool_sgu",
    )(proj, proj, proj, proj, pool_w, ps, sgu_w, sbc)


def _sb_attn_kernel(bias_ref, q_ref, k_ref, v_ref, o_ref, *, blk, scale):
    qi = pl.program_id(1)
    heads = lambda x: jnp.stack([x[:, h * HEAD_DIM:(h + 1) * HEAD_DIM] for h in range(HEADS)])
    q = heads(q_ref[...]).astype(BF16)
    bias = bias_ref[...]
    row = lax.broadcasted_iota(jnp.int32, (blk, blk), 0)
    col = lax.broadcasted_iota(jnp.int32, (blk, blk), 1)
    after = (row > col).astype(BF16)

    def visit(j, carry, valid):
        o, run = carry
        start = pl.multiple_of(j * blk, blk)
        kb = heads(k_ref[pl.ds(start, blk), :]).astype(BF16)
        vb = heads(v_ref[pl.ds(start, blk), :]).astype(BF16)
        z = lax.dot_general(q, kb, _B_NT, preferred_element_type=F32) * scale + bias
        sp = jnp.maximum(z, 0.0) + jnp.log(1.0 + jnp.exp(-jnp.abs(z)))
        log_fail = -sp if valid is None else jnp.where(valid, -sp, 0.0)
        hi = log_fail.reshape(HEADS * blk, blk).astype(BF16)
        later = jnp.dot(hi, after, preferred_element_type=F32).reshape(HEADS, blk, blk) + run
        att = jnp.exp(z - sp + later)
        if valid is not None:
            att = jnp.where(valid, att, 0.0)
        o = o + lax.dot_general(att.astype(BF16), vb, _B_NN, preferred_element_type=F32)
        run = run + jnp.sum(log_fail, axis=-1, keepdims=True)
        return o, run

    init = (jnp.zeros((HEADS, blk, HEAD_DIM), F32), jnp.zeros((HEADS, blk, 1), F32))
    carry = visit(qi, init, (col < row)[None])
    o, _ = lax.fori_loop(0, qi, lambda step, carry: visit(qi - 1 - step, carry, None), carry)
    for h in range(HEADS):
        o_ref[:, h * HEAD_DIM:(h + 1) * HEAD_DIM] = o[h]


def _sb_attn(proj, sb_bias, *, batch, seq, blk):
    n = proj.shape[0]
    nq = seq // blk
    return pl.pallas_call(
        functools.partial(_sb_attn_kernel, blk=blk, scale=HEAD_DIM ** -0.5),
        out_shape=jax.ShapeDtypeStruct((n, G_W), F32),
        grid=(batch, nq),
        in_specs=[pl.BlockSpec((HEADS, 1, 1), lambda b, i: (0, 0, 0)),
                  pl.BlockSpec((blk, G_W), lambda b, i: (b * nq + i, COL_SB_Q)),
                  pl.BlockSpec((seq, G_W), lambda b, i: (b, COL_SB_K)),
                  pl.BlockSpec((seq, G_W), lambda b, i: (b, COL_SB_V))],
        out_specs=pl.BlockSpec((blk, G_W), lambda b, i: (b * nq + i, 0)),
        compiler_params=_params("parallel", "arbitrary"),
        name="sb_attn",
    )(sb_bias.reshape(HEADS, 1, 1), proj, proj, proj)


def _sb_decode_body(q_ref, bias_ref, cum_ref, k_refs, v_refs, o_ref, *, width, scale):
    n_pages = len(k_refs)
    rows = 8
    q = q_ref[0]
    q_rows = jnp.concatenate([q[:, h * HEAD_DIM:(h + 1) * HEAD_DIM] for h in range(HEADS)]
                             + [jnp.zeros((rows - HEADS, HEAD_DIM), F32)], axis=0).astype(BF16)
    rid = lax.broadcasted_iota(jnp.int32, (rows, width), 0)
    cid = lax.broadcasted_iota(jnp.int32, (rows, width), 1)
    own = (cid & (HEADS - 1)) == rid

    z_rows = []
    for p in range(n_pages):
        zf = _bdot_nt(q_rows, k_refs[p][...])
        z_rows.append(jnp.sum(jnp.where(own, zf, 0.0), axis=0, keepdims=True))
    z = jnp.concatenate(z_rows, axis=0) * scale + bias_ref[...]
    sp = _softplus(z)
    hi, lo = _split_bf16(-sp)
    cum = (jnp.dot(hi, cum_ref[...], preferred_element_type=F32)
           + jnp.dot(lo, cum_ref[...], preferred_element_type=F32))
    in_page, page_tot = cum[:, :width], cum[:, width:]
    pr = lax.broadcasted_iota(jnp.int32, (n_pages, n_pages), 0)
    pc = lax.broadcasted_iota(jnp.int32, (n_pages, n_pages), 1)
    later_pages = _hdot((pc > pr).astype(F32), page_tot)
    att = jnp.exp(z - sp + in_page + later_pages)

    acc = jnp.zeros((rows, HEAD_DIM), F32)
    for p in range(n_pages):
        a_p = jnp.where(own, jnp.broadcast_to(att[p:p + 1, :], (rows, width)), 0.0)
        acc = acc + _bdot(a_p, v_refs[p][...])
    for h in range(HEADS):
        o_ref[0, :, h * HEAD_DIM:(h + 1) * HEAD_DIM] = acc[h:h + 1, :]


def _same_head_sums(width):
    src = np.arange(width)[:, None]
    dst = np.arange(width)[None, :]
    same = (src - dst) % HEADS == 0
    return jnp.asarray(np.concatenate([same & (src > dst), same], axis=1), BF16)


def _ffn_ln_decode_kernel(pt_ref, x_ref, wgu_ref, wo_ref, g_ref, b_ref, q_ref, bias_ref, cum_ref, k_hbm, v_hbm,
                          o_ref, y_ref, xb_ref, kbuf, vbuf, sem, *, alpha, tf, n_pages, width, scale, seqs, total):
    i, f = pl.program_id(0), pl.program_id(1)

    @pl.when(f == 0)
    def _():
        xb_ref[...] = x_ref[...].astype(BF16)
        o_ref[...] = jnp.zeros_like(o_ref)

    gate_up = jnp.dot(xb_ref[...], wgu_ref[...], preferred_element_type=F32)
    act = (_silu(gate_up[:, :tf]) * gate_up[:, tf:]).astype(BF16)
    o_ref[...] += jnp.dot(act, wo_ref[...], preferred_element_type=F32)

    def page_copies(t):
        slot = lax.rem(t, 2)
        copies = []
        for p in range(n_pages):
            row0 = pl.multiple_of(pt_ref[t * n_pages + p] * width, width)
            copies.append(pltpu.make_async_copy(k_hbm.at[pl.ds(row0, width), :], kbuf.at[slot, p], sem.at[slot]))
            copies.append(pltpu.make_async_copy(v_hbm.at[pl.ds(row0, width), :], vbuf.at[slot, p], sem.at[slot]))
        return copies

    @pl.when(f < seqs)
    def _():
        t = i * seqs + f

        @pl.when(t == 0)
        def _():
            for c in page_copies(t):
                c.start()

        @pl.when(t + 1 < total)
        def _():
            for c in page_copies(t + 1):
                c.start()

        for c in page_copies(t):
            c.wait()
        slot = lax.rem(t, 2)
        _sb_decode_body(q_ref, bias_ref, cum_ref, [kbuf.at[slot, p] for p in range(n_pages)],
                        [vbuf.at[slot, p] for p in range(n_pages)], y_ref, width=width, scale=scale)

    @pl.when(f == pl.num_programs(1) - 1)
    def _():
        o_ref[...] = _layer_norm(alpha * x_ref[...] + 0.5 * o_ref[...], g_ref[...], b_ref[...])


def _ffn_ln_decode(x, weights, ln_g, ln_b, proj3, bias_row, cache_k, cache_v, phys_pages, *,
                   layer, which, alpha, tm, tf, n_pages, width):
    n, d = x.shape
    bs = proj3.shape[0]
    w_gate_up, w_out = weights
    nf = w_out.shape[0] // tf
    tiles = n // tm
    seqs = bs // tiles
    assert bs == seqs * tiles and seqs <= nf
    seq = lambda i, f: i * seqs + jnp.minimum(f, seqs - 1)
    ln_spec = pl.BlockSpec((None, None, 1, d), lambda i, f, pt: (layer, which, 0, 0))
    grid_spec = pltpu.PrefetchScalarGridSpec(
        num_scalar_prefetch=1,
        grid=(tiles, nf),
        in_specs=[pl.BlockSpec((tm, d), lambda i, f, pt: (i, 0)),
                  pl.BlockSpec((d, 2 * tf), lambda i, f, pt: (0, f)),
                  pl.BlockSpec((tf, d), lambda i, f, pt: (f, 0)),
                  ln_spec, ln_spec,
                  pl.BlockSpec((1, 1, G_W), lambda i, f, pt: (seq(i, f), 0, COL_SB_Q)),
                  pl.BlockSpec((1, width), lambda i, f, pt: (0, 0)),
                  pl.BlockSpec((width, 2 * width), lambda i, f, pt: (0, 0), pipeline_mode=pl.Buffered(1)),
                  pl.BlockSpec(memory_space=pl.ANY), pl.BlockSpec(memory_space=pl.ANY)],
        out_specs=(pl.BlockSpec((tm, d), lambda i, f, pt: (i, 0)),
                   pl.BlockSpec((1, 1, G_W), lambda i, f, pt: (seq(i, f), 0, 0))),
        scratch_shapes=[pltpu.VMEM((tm, d), BF16),
                        pltpu.VMEM((2, n_pages, width, HEAD_DIM), F32),
                        pltpu.VMEM((2, n_pages, width, HEAD_DIM), F32),
                        pltpu.SemaphoreType.DMA((2,))],
    )
    out, y_sb = pl.pallas_call(
        functools.partial(_ffn_ln_decode_kernel, alpha=alpha, tf=tf, n_pages=n_pages, width=width,
                          scale=HEAD_DIM ** -0.5, seqs=seqs, total=bs),
        out_shape=(jax.ShapeDtypeStruct((n, d), F32), jax.ShapeDtypeStruct((bs, 1, G_W), F32)),
        grid_spec=grid_spec,
        compiler_params=pltpu.CompilerParams(dimension_semantics=("arbitrary", "arbitrary"),
                                             vmem_limit_bytes=FUSED_VMEM_LIMIT_BYTES),
        name="ffn_ln_decode",
    )(phys_pages, x, w_gate_up, w_out,
      ln_g.reshape(ln_g.shape[0], ln_g.shape[1], 1, d), ln_b.reshape(ln_b.shape[0], ln_b.shape[1], 1, d),
      proj3, bias_row, _same_head_sums(width), cache_k, cache_v)
    return out, y_sb.reshape(bs, G_W)


def _beta_and_log_decay(raw, neg_a_row, dt_row):
    lane = lax.broadcasted_iota(jnp.int32, raw.shape, 1)
    beta = jax.nn.sigmoid(raw)
    g = neg_a_row * _softplus(raw + dt_row)
    return jnp.where(lane < HEADS, beta, g)


def _l2_normalize(x):
    return x * lax.rsqrt(jnp.sum(x * x, -1, keepdims=True) + NORM_EPS)


def _dn_prep_kernel(raw_ref, halo_ref, bgr_ref, cw_ref, na_ref, dt_ref,
                    q_ref, k_ref, v_ref, bg_ref, bgt_ref, ext_ref, *, tiles_per_seq, rows):
    first = (pl.program_id(0) % tiles_per_seq) == 0
    ext_ref[0:CONV_HALO, :] = jnp.where(first, 0.0, halo_ref[...])
    ext_ref[CONV_HALO:, :] = raw_ref[...]
    conv = None
    for j in range(CONV_W):
        off = CONV_HALO - (CONV_W - 1) + j
        term = ext_ref[off:off + rows, :] * cw_ref[j:j + 1, :]
        conv = term if conv is None else conv + term
    act = _silu(conv)
    for h in range(HEADS):
        cols = slice(h * HEAD_DIM, (h + 1) * HEAD_DIM)
        q_ref[:, cols] = _l2_normalize(act[:, h * HEAD_DIM:(h + 1) * HEAD_DIM]) * (HEAD_DIM ** -0.5)
        k_ref[:, cols] = _l2_normalize(act[:, G_W + h * HEAD_DIM:G_W + (h + 1) * HEAD_DIM])
    v_ref[...] = act[:, 2 * G_W:]
    bg = _beta_and_log_decay(bgr_ref[...], na_ref[...], dt_ref[...])
    r = lax.broadcasted_iota(jnp.int32, (rows, rows), 0)
    s = lax.broadcasted_iota(jnp.int32, (rows, rows), 1)
    shift = DN_CHUNK.bit_length() - 1
    chunk_prefix = ((r >= s) & (lax.shift_right_logical(r, shift) == lax.shift_right_logical(s, shift)))
    lane = lax.broadcasted_iota(jnp.int32, bg.shape, 1)
    bg = jnp.where(lane < HEADS, bg, _hdot(chunk_prefix.astype(F32), bg))
    bg_ref[...] = bg
    per_tile = BG_LANES // DN_CHUNK
    for t in range(rows // BG_LANES):
        tile_t = bg[t * BG_LANES:(t + 1) * BG_LANES, :].T
        for c in range(per_tile):
            bgt_ref[t * per_tile + c] = tile_t[0:2 * HEADS, c * DN_CHUNK:(c + 1) * DN_CHUNK]


def _dn_prep(proj, bg_raw, conv_w, neg_a_row, dt_row, *, seq, rows):
    n = proj.shape[0]
    halo_blocks = rows // CONV_HALO
    qkv_w = 3 * G_W
    col = COL_DN_QKV * G_W // qkv_w
    whole = lambda a: pl.BlockSpec(a.shape, lambda g: (0,) * a.ndim)
    out = jax.ShapeDtypeStruct((n, G_W), F32)
    o_spec = pl.BlockSpec((rows, G_W), lambda g: (g, 0))
    return pl.pallas_call(
        functools.partial(_dn_prep_kernel, tiles_per_seq=seq // rows, rows=rows),
        out_shape=(out, out, out, jax.ShapeDtypeStruct((n, BG_LANES), F32),
                   jax.ShapeDtypeStruct((n // DN_CHUNK, 2 * HEADS, DN_CHUNK), F32)),
        grid=(n // rows,),
        in_specs=[pl.BlockSpec((rows, qkv_w), lambda g: (g, col)),
                  pl.BlockSpec((CONV_HALO, qkv_w), lambda g: (jnp.maximum(g * halo_blocks - 1, 0), col)),
                  pl.BlockSpec((rows, BG_LANES), lambda g: (g, 0)),
                  whole(conv_w), whole(neg_a_row), whole(dt_row)],
        out_specs=(o_spec, o_spec, o_spec, pl.BlockSpec((rows, BG_LANES), lambda g: (g, 0)),
                   pl.BlockSpec((rows // DN_CHUNK, 2 * HEADS, DN_CHUNK), lambda g: (g, 0, 0))),
        scratch_shapes=[pltpu.VMEM((CONV_HALO + rows, qkv_w), F32)],
        compiler_params=_params("parallel"),
        name="dn_prep",
    )(proj, proj, bg_raw, conv_w, neg_a_row, dt_row)


def _cat_lhs(x, axis):
    hi = x.astype(BF16).astype(F32)
    return jnp.concatenate([hi, x - hi, hi], axis=axis).astype(BF16)


def _cat_rhs(x, axis):
    hi = x.astype(BF16)
    lo = (x - hi.astype(F32)).astype(BF16)
    return jnp.concatenate([hi, hi, lo], axis=axis)


def _mm(lhs_cat, rhs_cat, dims=_B_NN):
    return lax.dot_general(lhs_cat, rhs_cat, dims, preferred_element_type=F32)


def _unit_lower_inverse(a_mat, eye):
    n = a_mat.shape[-1]
    inv = eye - a_mat
    power = _mm(_cat_lhs(a_mat, 2), _cat_rhs(a_mat, 1))
    span = 2
    while span < n:
        p_rhs = _cat_rhs(power, 1)
        inv = inv + _mm(_cat_lhs(inv, 2), p_rhs)
        span *= 2
        if span < n:
            power = _mm(_cat_lhs(power, 2), p_rhs)
    return inv


def _dn_out(o, z, norm_g):
    o = o * lax.rsqrt(jnp.mean(o * o, -1, keepdims=True) + NORM_EPS) * norm_g
    return o * _silu(z)


def _dn_scan_kernel(q_ref, k_ref, v_ref, z_ref, bg_ref, bgt_ref, ng_ref, y_ref, s_out_ref, s_ref, *, chunks):
    t = pl.program_id(1)

    @pl.when(t == 0)
    def _():
        s_ref[...] = jnp.zeros_like(s_ref)

    c, d = DN_CHUNK, HEAD_DIM
    units = [(ci, h) for ci in range(chunks) for h in range(HEADS)]
    rows = lambda ci: slice(ci * c, (ci + 1) * c)
    cols = lambda h: slice(h * d, (h + 1) * d)
    gather = lambda ref: jnp.stack([ref[rows(ci), cols(h)] for ci, h in units])
    q, k, v = gather(q_ref), gather(k_ref), gather(v_ref)
    beta = jnp.stack([bg_ref[rows(ci), h:h + 1] for ci, h in units])
    g_col = jnp.stack([bg_ref[rows(ci), HEADS + h:HEADS + h + 1] for ci, h in units])
    g_row = jnp.stack([bgt_ref[ci, HEADS + h:HEADS + h + 1, :] for ci, h in units])

    r = lax.broadcasted_iota(jnp.int32, (c, c), 0)
    s_ = lax.broadcasted_iota(jnp.int32, (c, c), 1)
    incl, strict, eye = (r >= s_)[None], (r > s_)[None], (r == s_).astype(F32)[None]
    decay = jnp.where(incl, jnp.exp(jnp.where(incl, g_col - g_row, 0.0)), 0.0)
    qk_hi, qk_lo = _split_bf16(jnp.concatenate([q, k], axis=1))
    k_hi, k_lo = _split_bf16(k)
    qk_kk = (_mm(jnp.concatenate([qk_hi, qk_lo], axis=2), jnp.concatenate([k_hi, k_hi], axis=2), _B_NT)
             + _mm(qk_hi, k_lo, _B_NT))
    qk = qk_kk[:, :c] * decay
    a_mat = jnp.where(strict, beta * qk_kk[:, c:] * decay, 0.0)
    inv = _unit_lower_inverse(a_mat, eye)
    e_g = jnp.exp(g_col)
    rhs = jnp.concatenate([beta * v, (beta * e_g) * k], axis=2)
    sol = _mm(_cat_lhs(inv, 2), _cat_rhs(rhs, 1))
    u_base, w_mat = sol[:, :, :d], sol[:, :, d:]
    w_hi, w_lo = _split_bf16(w_mat)
    w_cat = jnp.concatenate([w_hi, w_lo], axis=2)
    q_dec = (q * e_g).astype(BF16)
    g_last = g_col[:, c - 1:c, :]
    k_dec = k * jnp.exp(g_last - g_col)
    keep = jnp.exp(g_last)

    norm_g = ng_ref[...]
    s = s_ref[...]
    for ci in range(chunks):
        sl = slice(ci * HEADS, (ci + 1) * HEADS)
        s_hi, s_lo = _split_bf16(s)
        u = u_base[sl] - (_mm(w_cat[sl], jnp.concatenate([s_hi, s_hi], axis=1)) + _mm(w_hi[sl], s_lo))
        o = _mm(q_dec[sl], s_hi) + _mm(qk[sl].astype(BF16), u.astype(BF16))
        s = s * keep[sl] + _mm(_cat_lhs(k_dec[sl], 1), _cat_rhs(u, 1), _B_TN)
        for h in range(HEADS):
            y_ref[rows(ci), cols(h)] = _dn_out(o[h], z_ref[rows(ci), cols(h)], norm_g)
    s_ref[...] = s

    @pl.when(t == pl.num_programs(1) - 1)
    def _():
        s_out_ref[0] = s_ref[...]


def _dn_scan(q, k, v, proj, bg, bgt, norm_g, *, batch, seq, chunks):
    n = q.shape[0]
    rows = chunks * DN_CHUNK
    steps = seq // rows
    spec = pl.BlockSpec((rows, G_W), lambda b, t: (b * steps + t, 0))
    ng = norm_g.reshape(1, HEAD_DIM)
    return pl.pallas_call(
        functools.partial(_dn_scan_kernel, chunks=chunks),
        out_shape=(jax.ShapeDtypeStruct((n, G_W), F32),
                   jax.ShapeDtypeStruct((batch, HEADS, HEAD_DIM, HEAD_DIM), F32)),
        grid=(batch, steps),
        in_specs=[spec, spec, spec,
                  pl.BlockSpec((rows, G_W), lambda b, t: (b * steps + t, COL_DN_Z)),
                  pl.BlockSpec((rows, BG_LANES), lambda b, t: (b * steps + t, 0)),
                  pl.BlockSpec((chunks, 2 * HEADS, DN_CHUNK), lambda b, t: (b * steps + t, 0, 0)),
                  pl.BlockSpec((1, HEAD_DIM), lambda b, t: (0, 0))],
        out_specs=(spec, pl.BlockSpec((1, HEADS, HEAD_DIM, HEAD_DIM), lambda b, t: (b, 0, 0, 0))),
        scratch_shapes=[pltpu.VMEM((HEADS, HEAD_DIM, HEAD_DIM), F32)],
        compiler_params=_params("parallel", "arbitrary"),
        name="dn_scan",
    )(q, k, v, proj, bg, bgt, ng)


def _sample_mix_kernel(pool_ref, u_ref, v_ref, qkv_ref, z_ref, bgr_ref, sp_ref, sc_ref, sd_ref,
                       pw_ref, ps_ref, sw0_ref, sb0_ref, cw_ref, na_ref, dt_ref, ng_ref, *rest, bt, pos0):
    yp_ref, ys_ref, yd_ref, np_ref, nc_ref, nd_ref = rest[-6:]
    new = pool_ref[...]
    for gi, w in enumerate(POOL_WINDOWS):
        cols = slice(gi * HEAD_DIM, (gi + 1) * HEAD_DIM)
        s = new[:, cols]
        for j in range(1, w):
            s = s + sp_ref[POOL_BUF - j, :, cols]
        d = s / float(min(pos0 + 1, w)) - new[:, cols]
        yp_ref[:, cols] = _hdot(d, pw_ref[gi]) * ps_ref[:, cols]
    np_ref[0:POOL_BUF - 1] = sp_ref[1:POOL_BUF]
    np_ref[POOL_BUF - 1] = new

    ys_ref[...] = u_ref[...] * (sw0_ref[...] * v_ref[...] + sb0_ref[...])

    raw = qkv_ref[...]
    conv = raw * cw_ref[CONV_W - 1:CONV_W, :]
    for j in range(CONV_W - 1):
        conv = conv + sc_ref[j] * cw_ref[j:j + 1, :]
    nc_ref[0:CONV_W - 2] = sc_ref[1:CONV_W - 1]
    nc_ref[CONV_W - 2] = raw
    act = _silu(conv)
    bg = _beta_and_log_decay(bgr_ref[...], na_ref[...], dt_ref[...])
    norm_g = ng_ref[...]
    d = HEAD_DIM
    pad = jnp.zeros((d - bt, d), F32)
    k_t, q_t = [], []
    for h in range(HEADS):
        k_t.append(jnp.concatenate([_l2_normalize(act[:, G_W + h * d:G_W + (h + 1) * d]), pad], axis=0).T)
        q_t.append(jnp.concatenate([_l2_normalize(act[:, h * d:(h + 1) * d]) * (d ** -0.5), pad], axis=0).T)
    pairs = [(b, h) for b in range(bt) for h in range(HEADS)]
    v_h = [act[:, 2 * G_W + h * d:2 * G_W + (h + 1) * d] for h in range(HEADS)]
    z_h = [z_ref[:, h * d:(h + 1) * d] for h in range(HEADS)]
    a = [jnp.exp(bg[b:b + 1, HEADS + h:HEADS + h + 1]) for b, h in pairs]
    k_col = [jnp.broadcast_to(k_t[h][:, b:b + 1], (d, d)) for b, h in pairs]
    k_s = [jnp.sum(k_col[i] * sd_ref[b, h], axis=0, keepdims=True) for i, (b, h) in enumerate(pairs)]
    u = [bg[b:b + 1, h:h + 1] * (v_h[h][b:b + 1, :] - a[i] * k_s[i]) for i, (b, h) in enumerate(pairs)]
    s1 = [a[i] * sd_ref[b, h] + k_col[i] * u[i] for i, (b, h) in enumerate(pairs)]
    for i, (b, h) in enumerate(pairs):
        nd_ref[b, h] = s1[i]
    o = [jnp.sum(jnp.broadcast_to(q_t[h][:, b:b + 1], (d, d)) * s1[i], axis=0, keepdims=True)
         for i, (b, h) in enumerate(pairs)]
    for i, (b, h) in enumerate(pairs):
        yd_ref[b:b + 1, h * d:(h + 1) * d] = _dn_out(o[i], z_h[h][b:b + 1, :], norm_g)


def _sample_mix(proj, bg_raw, state_pool, state_conv, state_delta, new_prev, pool_w, pool_scale, sgu_w0, sgu_b0,
                conv_w, neg_a_row, dt_row, norm_g, *, layer, bt, pos0):
    bs = proj.shape[0]
    qkv_w = 3 * G_W
    col = lambda c: pl.BlockSpec((bt, G_W), lambda i: (i, c))
    whole = lambda a: pl.BlockSpec(a.shape, lambda i: (0,) * a.ndim)
    row = pl.BlockSpec((bt, G_W), lambda i: (i, 0))
    ps = pool_scale.reshape(1, G_W)
    ng = norm_g.reshape(1, HEAD_DIM)
    consts = (pool_w, ps, sgu_w0, sgu_b0, conv_w, neg_a_row, dt_row, ng)
    out_sds = lambda a: jax.ShapeDtypeStruct(a.shape, F32)
    pool_spec = pl.BlockSpec((None, POOL_BUF, bt, G_W), lambda i: (layer, 0, i, 0))
    conv_spec = pl.BlockSpec((None, CONV_W - 1, bt, qkv_w), lambda i: (layer, 0, i, 0))
    delta_spec = pl.BlockSpec((None, bt, HEADS, HEAD_DIM, HEAD_DIM), lambda i: (layer, i, 0, 0, 0))
    y = jax.ShapeDtypeStruct((bs, G_W), F32)
    prev = () if new_prev is None else tuple(new_prev)
    n_in = 9 + len(consts)
    return pl.pallas_call(
        functools.partial(_sample_mix_kernel, bt=bt, pos0=pos0),
        out_shape=(y, y, y, out_sds(state_pool), out_sds(state_conv), out_sds(state_delta)),
        grid=(bs // bt,),
        in_specs=[col(COL_POOL), col(COL_SGU_U), col(COL_SGU_V),
                  pl.BlockSpec((bt, qkv_w), lambda i: (i, COL_DN_QKV * G_W // qkv_w)),
                  col(COL_DN_Z),
                  pl.BlockSpec((bt, BG_LANES), lambda i: (i, 0)),
                  pool_spec, conv_spec, delta_spec] + [whole(a) for a in consts]
                 + [pl.BlockSpec(memory_space=pl.ANY)] * len(prev),
        out_specs=(row, row, row, pool_spec, conv_spec, delta_spec),
        input_output_aliases={n_in + k: 3 + k for k in range(len(prev))},
        compiler_params=_params("parallel"),
        name="sample_mix",
    )(proj, proj, proj, proj, proj, bg_raw, state_pool, state_conv, state_delta, *consts, *prev)


def _lane_row(values, offset):
    return jnp.zeros((1, BG_LANES), F32).at[0, offset:offset + values.shape[0]].set(values)


def kernel(x_prompt, x_sample, cache_k, cache_v, page_table, state_pool, state_conv, state_delta,
           ln_g, ln_b, w_ffn1_in, w_ffn1_out, w_ffn2_in, w_ffn2_out, w_in, w_out,
           pool_w, pool_scale, sgu_w, sgu_b, sb_bias, dn_conv_w, dn_a_log, dn_dt_bias, dn_norm_g):
    depth = ln_g.shape[0]
    alpha = (2.0 * depth) ** 0.25
    bp, seq, d_model = x_prompt.shape
    bs, dec_seq, _ = x_sample.shape
    assert dec_seq == 1
    n_phys, page = cache_k.shape[1], cache_k.shape[2]
    n_pages = page_table.shape[1]
    pos0 = n_pages * page
    ck = cache_k.reshape(depth * n_phys * page * HEADS, HEAD_DIM)
    cv = cache_v.reshape(depth * n_phys * page * HEADS, HEAD_DIM)
    hp = x_prompt.reshape(bp * seq, d_model)
    hs = x_sample.reshape(bs, d_model)
    pool_tm = jnp.swapaxes(state_pool, 1, 2)
    conv_tm = jnp.swapaxes(state_conv, 1, 2)
    tm_p = 512
    tm_s = bs
    tail_w = w_in.shape[2] - N_MAIN_COLS
    w_in_tail = jnp.pad(w_in[:, :, N_MAIN_COLS:].astype(BF16), ((0, 0), (0, 0), (0, BG_LANES - tail_w)))
    w_in_t = jnp.swapaxes(w_in, 1, 2)
    w_out_bf = w_out.astype(BF16)

    outs = {name: [] for name in ("poolp", "convp", "sp", "ks", "vs", "sgus")}
    kv_rows = None
    new_states = None
    for l in range(depth):
        neg_a_row = _lane_row(-jnp.exp(dn_a_log[l]), HEADS)
        dt_row = _lane_row(dn_dt_bias[l], HEADS)
        ffn = functools.partial(_ffn_ln, ln_g=ln_g, ln_b=ln_b, layer=l, alpha=alpha, tf=512)
        out_proj = functools.partial(_out_proj_ln, w_out=w_out_bf, ln_g=ln_g, ln_b=ln_b, layer=l, alpha=alpha)

        g1, *w1 = ffn(hs, (w_ffn1_in, w_ffn1_out), which=0, tm=tm_s)
        sproj, sbg_raw, sk_rows, sv_rows, wi_main = _in_proj_cast(g1, w_in_t, w_in_tail, layer=l)
        sgu_w0 = jnp.repeat(sgu_w[l, :, 0, 0], HEAD_DIM).reshape(1, G_W)
        sgu_b0 = jnp.repeat(sgu_b[l, :, 0], HEAD_DIM).reshape(1, G_W)
        sy_pool, sy_sgu, sy_dn, *new_states = _sample_mix(
            sproj, sbg_raw, pool_tm, conv_tm, state_delta, new_states, pool_w[l], pool_scale[l],
            sgu_w0, sgu_b0, dn_conv_w[l], neg_a_row, dt_row, dn_norm_g[l], layer=l, bt=8, pos0=pos0)
        bias_row = jnp.tile(sb_bias[l], page).reshape(1, page * HEADS)
        phys = (page_table + l * n_phys).reshape(-1)
        h1, sy_sb = _ffn_ln_decode(hp, tuple(w1), ln_g, ln_b, sproj.reshape(bs, 1, N_MAIN_COLS), bias_row, ck, cv, phys,
                                   layer=l, which=0, alpha=alpha, tm=tm_p, tf=512,
                                   n_pages=n_pages, width=page * HEADS)
        g2 = out_proj(g1, (sy_pool, sy_sgu, sy_sb, sy_dn), tm=tm_s)
        hs, *w2 = ffn(g2, (w_ffn2_in, w_ffn2_out), which=2, tm=tm_s)
        outs["ks"].append(sk_rows.reshape(bs, 1, HEADS, HEAD_DIM))
        outs["vs"].append(sv_rows.reshape(bs, 1, HEADS, HEAD_DIM))
        outs["sgus"].append(sproj[:, COL_SGU_V * G_W:(COL_SGU_V + 1) * G_W].reshape(bs, 1, G_W))

        proj, bg_raw, *kv_rows = _in_proj(h1, wi_main, w_in_tail, kv_rows, layer=l, depth=depth, tm=256)
        y_pool, y_sgu = _pool_sgu(proj, pool_w[l], pool_scale[l], sgu_w[l], sgu_b[l], seq=seq, rows=512)
        y_sb = _sb_attn(proj, sb_bias[l], batch=bp, seq=seq, blk=256)
        dq, dk, dv, bg, bgt = _dn_prep(proj, bg_raw, dn_conv_w[l], neg_a_row, dt_row, seq=seq, rows=256)
        y_dn, s_end = _dn_scan(dq, dk, dv, proj, bg, bgt, dn_norm_g[l], batch=bp, seq=seq, chunks=4)
        h2 = out_proj(h1, (y_pool, y_sgu, y_sb, y_dn), tm=256)
        hp, = ffn(h2, tuple(w2), which=2, tm=tm_p)
        proj3 = proj.reshape(bp, seq, N_MAIN_COLS)
        outs["poolp"].append(proj3[:, seq - POOL_BUF:, :G_W])
        outs["convp"].append(proj3[:, seq - (CONV_W - 1):, COL_DN_QKV * G_W:(COL_DN_QKV + 3) * G_W])
        outs["sp"].append(s_end)

    st = lambda name: jnp.stack(outs[name])
    new_pool, new_conv, new_delta = new_states
    kp, vp = (a.reshape(depth, bp, seq, HEADS, HEAD_DIM) for a in kv_rows)
    return (hp.reshape(bp, seq, d_model), hs.reshape(bs, 1, d_model),
            kp, vp, st("poolp"), st("convp"), st("sp"),
            st("ks"), st("vs"), jnp.swapaxes(new_pool, 1, 2), jnp.swapaxes(new_conv, 1, 2),
            new_delta, st("sgus"))
```

```python
import functools

import jax
import jax.numpy as jnp
import numpy as np
from jax import lax
from jax.experimental import pallas as pl
from jax.experimental.pallas import tpu as pltpu

F32 = jnp.float32
BF16 = jnp.bfloat16
HIGHEST = lax.Precision.HIGHEST

HEADS = 4
HEAD_DIM = 128
G_W = HEADS * HEAD_DIM
POOL_WINDOWS = (2, 4, 8, 16)
POOL_BUF = max(POOL_WINDOWS) - 1
POOL_HALO = 16
SGU_CHUNK = 128
DN_CHUNK = 64
CONV_W = 4
CONV_HALO = 8
LN_EPS = 1e-5
NORM_EPS = 1e-6
VMEM_LIMIT_BYTES = 48 * 1024 * 1024
FUSED_VMEM_LIMIT_BYTES = 58 * 1024 * 1024

COL_POOL, COL_SGU_U, COL_SGU_V, COL_SB_Q, COL_SB_K, COL_SB_V, COL_DN_QKV, COL_DN_Z = 0, 1, 2, 3, 4, 5, 6, 9
N_MAIN_COLS = 10 * G_W
BG_LANES = 128


def _params(*sem):
    return pltpu.CompilerParams(dimension_semantics=sem, vmem_limit_bytes=VMEM_LIMIT_BYTES)


def _hdot(a, b):
    return jnp.dot(a, b, precision=HIGHEST, preferred_element_type=F32)


def _bdot(a, b):
    return jnp.dot(a.astype(BF16), b.astype(BF16), preferred_element_type=F32)


def _bdot_nt(a, b):
    return lax.dot_general(a.astype(BF16), b.astype(BF16), (((1,), (1,)), ((), ())),
                           preferred_element_type=F32)


def _silu(x):
    return x * jax.nn.sigmoid(x)


def _softplus(x):
    return jnp.maximum(x, 0.0) + jnp.log1p(jnp.exp(-jnp.abs(x)))


def _layer_norm(y, g, b):
    mu = jnp.mean(y, -1, keepdims=True)
    d = y - mu
    var = jnp.mean(d * d, -1, keepdims=True)
    return d * lax.rsqrt(var + LN_EPS) * g + b


def _split_bf16(x):
    hi = x.astype(BF16)
    lo = (x - hi.astype(F32)).astype(BF16)
    return hi, lo


_B_NN = (((2,), (1,)), ((0,), (0,)))
_B_NT = (((2,), (2,)), ((0,), (0,)))
_B_TN = (((1,), (1,)), ((0,), (0,)))


def _ffn_ln_kernel(x_ref, *refs, alpha, emit_bf16, tf):
    if emit_bf16:
        wg_ref, wu_ref, wo_ref, g_ref, b_ref, o_ref, wgu_out, wo_out, xb_ref, acc_ref = refs
    else:
        wgu_ref, wo_ref, g_ref, b_ref, o_ref, xb_ref, acc_ref = refs
    f = pl.program_id(1)

    @pl.when(f == 0)
    def _():
        xb_ref[...] = x_ref[...].astype(BF16)
        acc_ref[...] = jnp.zeros_like(acc_ref)

    xb = xb_ref[...]
    if emit_bf16:
        wg, wu, wo = wg_ref[...].astype(BF16), wu_ref[...].astype(BF16), wo_ref[...].astype(BF16)
        wgu_out[:, :tf], wgu_out[:, tf:], wo_out[...] = wg, wu, wo
        gate = jnp.dot(xb, wg, preferred_element_type=F32)
        up = jnp.dot(xb, wu, preferred_element_type=F32)
    else:
        wo = wo_ref[...]
        gate_up = jnp.dot(xb, wgu_ref[...], preferred_element_type=F32)
        gate, up = gate_up[:, :tf], gate_up[:, tf:]
    act = (_silu(gate) * up).astype(BF16)
    acc_ref[...] += jnp.dot(act, wo, preferred_element_type=F32)

    @pl.when(f == pl.num_programs(1) - 1)
    def _():
        o_ref[...] = _layer_norm(alpha * x_ref[...] + 0.5 * acc_ref[...], g_ref[...], b_ref[...])


def _ffn_ln(x, weights, ln_g, ln_b, *, layer, which, alpha, tm, tf):
    n, d = x.shape
    emit_bf16 = weights[0].dtype == F32
    if emit_bf16:
        w_in, w_out = weights
        dff = w_out.shape[1]
        nf = dff // tf
        operands = (w_in, w_in, w_out)
        w_specs = [pl.BlockSpec((None, d, tf), lambda i, f: (layer, 0, f)),
                   pl.BlockSpec((None, d, tf), lambda i, f: (layer, 0, f + nf)),
                   pl.BlockSpec((None, tf, d), lambda i, f: (layer, f, 0))]
    else:
        operands = weights
        dff = weights[1].shape[0]
        nf = dff // tf
        w_specs = [pl.BlockSpec((d, 2 * tf), lambda i, f: (0, f)),
                   pl.BlockSpec((tf, d), lambda i, f: (f, 0))]
    ln_spec = pl.BlockSpec((None, None, 1, d), lambda i, f: (layer, which, 0, 0))
    out_shape = [jax.ShapeDtypeStruct((n, d), F32)]
    out_specs = [pl.BlockSpec((tm, d), lambda i, f: (i, 0))]
    if emit_bf16:
        assert n == tm
        out_shape += [jax.ShapeDtypeStruct((d, 2 * dff), BF16), jax.ShapeDtypeStruct((dff, d), BF16)]
        out_specs += [pl.BlockSpec((d, 2 * tf), lambda i, f: (0, f)), pl.BlockSpec((tf, d), lambda i, f: (f, 0))]
    return pl.pallas_call(
        functools.partial(_ffn_ln_kernel, alpha=alpha, emit_bf16=emit_bf16, tf=tf),
        out_shape=tuple(out_shape),
        grid=(n // tm, nf),
        in_specs=[pl.BlockSpec((tm, d), lambda i, f: (i, 0))] + w_specs + [ln_spec, ln_spec],
        out_specs=tuple(out_specs),
        scratch_shapes=[pltpu.VMEM((tm, d), BF16), pltpu.VMEM((tm, d), F32)],
        compiler_params=_params("parallel", "arbitrary"),
        name="ffn_ln_cast" if emit_bf16 else "ffn_ln",
    )(x, *operands, ln_g.reshape(ln_g.shape[0], ln_g.shape[1], 1, d), ln_b.reshape(ln_b.shape[0], ln_b.shape[1], 1, d))


def _store_head_rows(dst_ref, block, tm):
    for h in range(HEADS):
        dst_ref[pl.ds(h, tm, stride=HEADS), :] = block[:, h * HEAD_DIM:(h + 1) * HEAD_DIM]


def _in_proj_kernel(*refs, tm, aliased):
    h_ref, w_ref, wt_ref = refs[:3]
    o_ref, bg_ref, k_ref, v_ref = refs[3 + aliased:]
    hb = h_ref[...].astype(BF16)
    bg_ref[...] = jnp.dot(hb, wt_ref[...], preferred_element_type=F32)
    o = jnp.dot(hb, w_ref[...], preferred_element_type=F32)
    o_ref[...] = o
    _store_head_rows(k_ref, o[:, COL_SB_K * G_W:(COL_SB_K + 1) * G_W], tm)
    _store_head_rows(v_ref, o[:, COL_SB_V * G_W:(COL_SB_V + 1) * G_W], tm)


def _in_proj(h, w_main, w_tail, kv_prev, *, layer, depth, tm):
    n, d = h.shape
    steps = n // tm
    resident = pl.Buffered(1)
    rows_kv = jax.ShapeDtypeStruct((depth * n * HEADS, HEAD_DIM), F32)
    kv_spec = pl.BlockSpec((tm * HEADS, HEAD_DIM), lambda i: (layer * steps + i, 0))
    aliased = 0 if kv_prev is None else 2
    prev = () if kv_prev is None else tuple(kv_prev)
    return pl.pallas_call(
        functools.partial(_in_proj_kernel, tm=tm, aliased=aliased),
        out_shape=(jax.ShapeDtypeStruct((n, N_MAIN_COLS), F32), jax.ShapeDtypeStruct((n, BG_LANES), F32),
                   rows_kv, rows_kv),
        grid=(steps,),
        in_specs=[
            pl.BlockSpec((tm, d), lambda i: (i, 0)),
            pl.BlockSpec((d, N_MAIN_COLS), lambda i: (0, 0), pipeline_mode=resident),
            pl.BlockSpec((None, d, BG_LANES), lambda i: (layer, 0, 0), pipeline_mode=resident),
        ] + [pl.BlockSpec(memory_space=pl.ANY)] * aliased,
        out_specs=(pl.BlockSpec((tm, N_MAIN_COLS), lambda i: (i, 0)),
                   pl.BlockSpec((tm, BG_LANES), lambda i: (i, 0)), kv_spec, kv_spec),
        input_output_aliases={3: 2, 4: 3} if aliased else {},
        compiler_params=_params("parallel"),
        name="in_proj",
    )(h, w_main, w_tail, *prev)


def _in_proj_cast_kernel(h_ref, w_ref, wt_ref, o_ref, bg_ref, k_ref, v_ref, wb_ref, *, tm):
    j = pl.program_id(0)
    hb = h_ref[...].astype(BF16)
    wb = w_ref[...].T.astype(BF16)
    wb_ref[...] = wb
    o = jnp.dot(hb, wb, preferred_element_type=F32)
    o_ref[...] = o

    @pl.when(j == 0)
    def _():
        bg_ref[...] = jnp.dot(hb, wt_ref[...], preferred_element_type=F32)

    @pl.when(j == COL_SB_K)
    def _():
        _store_head_rows(k_ref, o, tm)

    @pl.when(j == COL_SB_V)
    def _():
        _store_head_rows(v_ref, o, tm)


def _in_proj_cast(h, w_in_t, w_tail, *, layer):
    n, d = h.shape
    rows_kv = jax.ShapeDtypeStruct((n * HEADS, HEAD_DIM), F32)
    kv_spec = pl.BlockSpec((n * HEADS, HEAD_DIM), lambda j: (0, 0))
    return pl.pallas_call(
        functools.partial(_in_proj_cast_kernel, tm=n),
        out_shape=(jax.ShapeDtypeStruct((n, N_MAIN_COLS), F32), jax.ShapeDtypeStruct((n, BG_LANES), F32),
                   rows_kv, rows_kv, jax.ShapeDtypeStruct((d, N_MAIN_COLS), BF16)),
        grid=(N_MAIN_COLS // G_W,),
        in_specs=[
            pl.BlockSpec((n, d), lambda j: (0, 0)),
            pl.BlockSpec((None, G_W, d), lambda j: (layer, j, 0)),
            pl.BlockSpec((None, d, BG_LANES), lambda j: (layer, 0, 0)),
        ],
        out_specs=(pl.BlockSpec((n, G_W), lambda j: (0, j)),
                   pl.BlockSpec((n, BG_LANES), lambda j: (0, 0)), kv_spec, kv_spec,
                   pl.BlockSpec((d, G_W), lambda j: (0, j))),
        compiler_params=_params("arbitrary"),
        name="in_proj_cast",
    )(h, w_in_t, w_tail)


def _out_proj_ln_kernel(h_ref, y0_ref, y1_ref, y2_ref, y3_ref, w_ref, g_ref, b_ref, o_ref, *, alpha):
    mixed = jnp.concatenate([y_ref[...].astype(BF16) for y_ref in (y0_ref, y1_ref, y2_ref, y3_ref)], axis=1)
    acc = jnp.dot(mixed, w_ref[...], preferred_element_type=F32)
    o_ref[...] = _layer_norm(alpha * h_ref[...] + acc, g_ref[...], b_ref[...])


def _out_proj_ln(h, ys, w_out, ln_g, ln_b, *, layer, alpha, tm):
    n, d = h.shape
    y_spec = pl.BlockSpec((tm, G_W), lambda i: (i, 0))
    ln_spec = pl.BlockSpec((None, None, 1, d), lambda i: (layer, 1, 0, 0))
    return pl.pallas_call(
        functools.partial(_out_proj_ln_kernel, alpha=alpha),
        out_shape=jax.ShapeDtypeStruct((n, d), F32),
        grid=(n // tm,),
        in_specs=[pl.BlockSpec((tm, d), lambda i: (i, 0)), y_spec, y_spec, y_spec, y_spec,
                  pl.BlockSpec((None, 4 * G_W, d), lambda i: (layer, 0, 0)), ln_spec, ln_spec],
        out_specs=pl.BlockSpec((tm, d), lambda i: (i, 0)),
        compiler_params=_params("parallel"),
        name="out_proj_ln",
    )(h, *ys, w_out, ln_g.reshape(ln_g.shape[0], ln_g.shape[1], 1, d), ln_b.reshape(ln_b.shape[0], ln_b.shape[1], 1, d))


def _pool_sgu_kernel(cur_ref, halo_ref, u_ref, v_ref, pw_ref, ps_ref, sw_ref, sbc_ref,
                     yp_ref, ys_ref, ext_ref, *, tiles_per_seq, rows):
    c = pl.program_id(0) % tiles_per_seq
    cur = cur_ref[...]
    ext_ref[0:POOL_HALO, :] = jnp.where(c == 0, 0.0, halo_ref[...])
    ext_ref[POOL_HALO:, :] = cur
    pos = c * rows + lax.broadcasted_iota(jnp.int32, (rows, 1), 0)
    for gi, w in enumerate(POOL_WINDOWS):
        cols = slice(gi * HEAD_DIM, (gi + 1) * HEAD_DIM)
        s = cur[:, cols]
        for j in range(1, w):
            s = s + ext_ref[POOL_HALO - j:POOL_HALO - j + rows, cols]
        cnt = jnp.minimum(pos + 1, w).astype(F32)
        d = s / cnt - cur[:, cols]
        yp_ref[:, cols] = _bdot(d, pw_ref[gi]) * ps_ref[:, cols]

    r = lax.broadcasted_iota(jnp.int32, (SGU_CHUNK, SGU_CHUNK), 0)
    s_ = lax.broadcasted_iota(jnp.int32, (SGU_CHUNK, SGU_CHUNK), 1)
    causal = r >= s_
    for h in range(HEADS):
        cols = slice(h * HEAD_DIM, (h + 1) * HEAD_DIM)
        wm = jnp.where(causal, sw_ref[h], 0.0).astype(BF16)
        for ci in range(rows // SGU_CHUNK):
            chunk = slice(ci * SGU_CHUNK, (ci + 1) * SGU_CHUNK)
            mixed = _bdot(wm, v_ref[chunk, cols]) + sbc_ref[:, h:h + 1]
            ys_ref[chunk, cols] = u_ref[chunk, cols] * mixed


def _pool_sgu(proj, pool_w, pool_scale, sgu_w, sgu_b, *, seq, rows):
    n = proj.shape[0]
    assert rows % SGU_CHUNK == 0 and seq % rows == 0
    halo_blocks = rows // POOL_HALO
    blk = lambda col: pl.BlockSpec((rows, G_W), lambda g: (g, col))
    whole = lambda a: pl.BlockSpec(a.shape, lambda g: (0,) * a.ndim)
    ps = pool_scale.reshape(1, G_W)
    sbc = sgu_b.T
    return pl.pallas_call(
        functools.partial(_pool_sgu_kernel, tiles_per_seq=seq // rows, rows=rows),
        out_shape=(jax.ShapeDtypeStruct((n, G_W), F32), jax.ShapeDtypeStruct((n, G_W), F32)),
        grid=(n // rows,),
        in_specs=[blk(COL_POOL),
                  pl.BlockSpec((POOL_HALO, G_W), lambda g: (jnp.maximum(g * halo_blocks - 1, 0), COL_POOL)),
                  blk(COL_SGU_U), blk(COL_SGU_V), whole(pool_w), whole(ps), whole(sgu_w), whole(sbc)],
        out_specs=(pl.BlockSpec((rows, G_W), lambda g: (g, 0)), pl.BlockSpec((rows, G_W), lambda g: (g, 0))),
        scratch_shapes=[pltpu.VMEM((POOL_HALO + rows, G_W), F32)],
        compiler_params=_params("parallel"),
        name="pool_sgu",
    )(proj, proj, proj, proj, pool_w, ps, sgu_w, sbc)


def _sb_attn_kernel(bias_ref, q_ref, k_ref, v_ref, o_ref, *, blk, scale):
    qi = pl.program_id(1)
    heads = lambda x: jnp.stack([x[:, h * HEAD_DIM:(h + 1) * HEAD_DIM] for h in range(HEADS)])
    q = heads(q_ref[...]).astype(BF16)
    bias = bias_ref[...]
    row = lax.broadcasted_iota(jnp.int32, (blk, blk), 0)
    col = lax.broadcasted_iota(jnp.int32, (blk, blk), 1)
    after = (row > col).astype(BF16)

    def visit(j, carry, valid):
        o, run = carry
        start = pl.multiple_of(j * blk, blk)
        kb = heads(k_ref[pl.ds(start, blk), :]).astype(BF16)
        vb = heads(v_ref[pl.ds(start, blk), :]).astype(BF16)
        z = lax.dot_general(q, kb, _B_NT, preferred_element_type=F32) * scale + bias
        sp = jnp.maximum(z, 0.0) + jnp.log(1.0 + jnp.exp(-jnp.abs(z)))
        log_fail = -sp if valid is None else jnp.where(valid, -sp, 0.0)
        hi = log_fail.reshape(HEADS * blk, blk).astype(BF16)
        later = jnp.dot(hi, after, preferred_element_type=F32).reshape(HEADS, blk, blk) + run
        att = jnp.exp(z - sp + later)
        if valid is not None:
            att = jnp.where(valid, att, 0.0)
        o = o + lax.dot_general(att.astype(BF16), vb, _B_NN, preferred_element_type=F32)
        run = run + jnp.sum(log_fail, axis=-1, keepdims=True)
        return o, run

    init = (jnp.zeros((HEADS, blk, HEAD_DIM), F32), jnp.zeros((HEADS, blk, 1), F32))
    carry = visit(qi, init, (col < row)[None])
    o, _ = lax.fori_loop(0, qi, lambda step, carry: visit(qi - 1 - step, carry, None), carry)
    for h in range(HEADS):
        o_ref[:, h * HEAD_DIM:(h + 1) * HEAD_DIM] = o[h]


def _sb_attn(proj, sb_bias, *, batch, seq, blk):
    n = proj.shape[0]
    nq = seq // blk
    return pl.pallas_call(
        functools.partial(_sb_attn_kernel, blk=blk, scale=HEAD_DIM ** -0.5),
        out_shape=jax.ShapeDtypeStruct((n, G_W), F32),
        grid=(batch, nq),
        in_specs=[pl.BlockSpec((HEADS, 1, 1), lambda b, i: (0, 0, 0)),
                  pl.BlockSpec((blk, G_W), lambda b, i: (b * nq + i, COL_SB_Q)),
                  pl.BlockSpec((seq, G_W), lambda b, i: (b, COL_SB_K)),
                  pl.BlockSpec((seq, G_W), lambda b, i: (b, COL_SB_V))],
        out_specs=pl.BlockSpec((blk, G_W), lambda b, i: (b * nq + i, 0)),
        compiler_params=_params("parallel", "arbitrary"),
        name="sb_attn",
    )(sb_bias.reshape(HEADS, 1, 1), proj, proj, proj)


def _sb_decode_body(q_ref, bias_ref, cum_ref, k_refs, v_refs, o_ref, *, width, scale):
    n_pages = len(k_refs)
    rows = 8
    q = q_ref[0]
    q_rows = jnp.concatenate([q[:, h * HEAD_DIM:(h + 1) * HEAD_DIM] for h in range(HEADS)]
                             + [jnp.zeros((rows - HEADS, HEAD_DIM), F32)], axis=0).astype(BF16)
    rid = lax.broadcasted_iota(jnp.int32, (rows, width), 0)
    cid = lax.broadcasted_iota(jnp.int32, (rows, width), 1)
    own = (cid & (HEADS - 1)) == rid

    z_rows = []
    for p in range(n_pages):
        zf = _bdot_nt(q_rows, k_refs[p][...])
        z_rows.append(jnp.sum(jnp.where(own, zf, 0.0), axis=0, keepdims=True))
    z = jnp.concatenate(z_rows, axis=0) * scale + bias_ref[...]
    sp = _softplus(z)
    hi, lo = _split_bf16(-sp)
    cum = (jnp.dot(hi, cum_ref[...], preferred_element_type=F32)
           + jnp.dot(lo, cum_ref[...], preferred_element_type=F32))
    in_page, page_tot = cum[:, :width], cum[:, width:]
    pr = lax.broadcasted_iota(jnp.int32, (n_pages, n_pages), 0)
    pc = lax.broadcasted_iota(jnp.int32, (n_pages, n_pages), 1)
    later_pages = _hdot((pc > pr).astype(F32), page_tot)
    att = jnp.exp(z - sp + in_page + later_pages)

    acc = jnp.zeros((rows, HEAD_DIM), F32)
    for p in range(n_pages):
        a_p = jnp.where(own, jnp.broadcast_to(att[p:p + 1, :], (rows, width)), 0.0)
        acc = acc + _bdot(a_p, v_refs[p][...])
    for h in range(HEADS):
        o_ref[0, :, h * HEAD_DIM:(h + 1) * HEAD_DIM] = acc[h:h + 1, :]


def _same_head_sums(width):
    src = np.arange(width)[:, None]
    dst = np.arange(width)[None, :]
    same = (src - dst) % HEADS == 0
    return jnp.asarray(np.concatenate([same & (src > dst), same], axis=1), BF16)


def _ffn_ln_decode_kernel(pt_ref, x_ref, wgu_ref, wo_ref, g_ref, b_ref, q_ref, bias_ref, cum_ref, k_hbm, v_hbm,
                          o_ref, y_ref, xb_ref, kbuf, vbuf, sem, *, alpha, tf, n_pages, width, scale, seqs, total):
    i, f = pl.program_id(0), pl.program_id(1)

    @pl.when(f == 0)
    def _():
        xb_ref[...] = x_ref[...].astype(BF16)
        o_ref[...] = jnp.zeros_like(o_ref)

    gate_up = jnp.dot(xb_ref[...], wgu_ref[...], preferred_element_type=F32)
    act = (_silu(gate_up[:, :tf]) * gate_up[:, tf:]).astype(BF16)
    o_ref[...] += jnp.dot(act, wo_ref[...], preferred_element_type=F32)

    def page_copies(t):
        slot = lax.rem(t, 2)
        copies = []
        for p in range(n_pages):
            row0 = pl.multiple_of(pt_ref[t * n_pages + p] * width, width)
            copies.append(pltpu.make_async_copy(k_hbm.at[pl.ds(row0, width), :], kbuf.at[slot, p], sem.at[slot]))
            copies.append(pltpu.make_async_copy(v_hbm.at[pl.ds(row0, width), :], vbuf.at[slot, p], sem.at[slot]))
        return copies

    @pl.when(f < seqs)
    def _():
        t = i * seqs + f

        @pl.when(t == 0)
        def _():
            for c in page_copies(t):
                c.start()

        @pl.when(t + 1 < total)
        def _():
            for c in page_copies(t + 1):
                c.start()

        for c in page_copies(t):
            c.wait()
        slot = lax.rem(t, 2)
        _sb_decode_body(q_ref, bias_ref, cum_ref, [kbuf.at[slot, p] for p in range(n_pages)],
                        [vbuf.at[slot, p] for p in range(n_pages)], y_ref, width=width, scale=scale)

    @pl.when(f == pl.num_programs(1) - 1)
    def _():
        o_ref[...] = _layer_norm(alpha * x_ref[...] + 0.5 * o_ref[...], g_ref[...], b_ref[...])


def _ffn_ln_decode(x, weights, ln_g, ln_b, proj3, bias_row, cache_k, cache_v, phys_pages, *,
                   layer, which, alpha, tm, tf, n_pages, width):
    n, d = x.shape
    bs = proj3.shape[0]
    w_gate_up, w_out = weights
    nf = w_out.shape[0] // tf
    tiles = n // tm
    seqs = bs // tiles
    assert bs == seqs * tiles and seqs <= nf
    seq = lambda i, f: i * seqs + jnp.minimum(f, seqs - 1)
    ln_spec = pl.BlockSpec((None, None, 1, d), lambda i, f, pt: (layer, which, 0, 0))
    grid_spec = pltpu.PrefetchScalarGridSpec(
        num_scalar_prefetch=1,
        grid=(tiles, nf),
        in_specs=[pl.BlockSpec((tm, d), lambda i, f, pt: (i, 0)),
                  pl.BlockSpec((d, 2 * tf), lambda i, f, pt: (0, f)),
                  pl.BlockSpec((tf, d), lambda i, f, pt: (f, 0)),
                  ln_spec, ln_spec,
                  pl.BlockSpec((1, 1, G_W), lambda i, f, pt: (seq(i, f), 0, COL_SB_Q)),
                  pl.BlockSpec((1, width), lambda i, f, pt: (0, 0)),
                  pl.BlockSpec((width, 2 * width), lambda i, f, pt: (0, 0), pipeline_mode=pl.Buffered(1)),
                  pl.BlockSpec(memory_space=pl.ANY), pl.BlockSpec(memory_space=pl.ANY)],
        out_specs=(pl.BlockSpec((tm, d), lambda i, f, pt: (i, 0)),
                   pl.BlockSpec((1, 1, G_W), lambda i, f, pt: (seq(i, f), 0, 0))),
        scratch_shapes=[pltpu.VMEM((tm, d), BF16),
                        pltpu.VMEM((2, n_pages, width, HEAD_DIM), F32),
                        pltpu.VMEM((2, n_pages, width, HEAD_DIM), F32),
                        pltpu.SemaphoreType.DMA((2,))],
    )
    out, y_sb = pl.pallas_call(
        functools.partial(_ffn_ln_decode_kernel, alpha=alpha, tf=tf, n_pages=n_pages, width=width,
                          scale=HEAD_DIM ** -0.5, seqs=seqs, total=bs),
        out_shape=(jax.ShapeDtypeStruct((n, d), F32), jax.ShapeDtypeStruct((bs, 1, G_W), F32)),
        grid_spec=grid_spec,
        compiler_params=pltpu.CompilerParams(dimension_semantics=("arbitrary", "arbitrary"),
                                             vmem_limit_bytes=FUSED_VMEM_LIMIT_BYTES),
        name="ffn_ln_decode",
    )(phys_pages, x, w_gate_up, w_out,
      ln_g.reshape(ln_g.shape[0], ln_g.shape[1], 1, d), ln_b.reshape(ln_b.shape[0], ln_b.shape[1], 1, d),
      proj3, bias_row, _same_head_sums(width), cache_k, cache_v)
    return out, y_sb.reshape(bs, G_W)


def _beta_and_log_decay(raw, neg_a_row, dt_row):
    lane = lax.broadcasted_iota(jnp.int32, raw.shape, 1)
    beta = jax.nn.sigmoid(raw)
    g = neg_a_row * _softplus(raw + dt_row)
    return jnp.where(lane < HEADS, beta, g)


def _l2_normalize(x):
    return x * lax.rsqrt(jnp.sum(x * x, -1, keepdims=True) + NORM_EPS)


def _dn_prep_kernel(raw_ref, halo_ref, bgr_ref, cw_ref, na_ref, dt_ref,
                    q_ref, k_ref, v_ref, bg_ref, bgt_ref, ext_ref, *, tiles_per_seq, rows):
    first = (pl.program_id(0) % tiles_per_seq) == 0
    ext_ref[0:CONV_HALO, :] = jnp.where(first, 0.0, halo_ref[...])
    ext_ref[CONV_HALO:, :] = raw_ref[...]
    conv = None
    for j in range(CONV_W):
        off = CONV_HALO - (CONV_W - 1) + j
        term = ext_ref[off:off + rows, :] * cw_ref[j:j + 1, :]
        conv = term if conv is None else conv + term
    act = _silu(conv)
    for h in range(HEADS):
        cols = slice(h * HEAD_DIM, (h + 1) * HEAD_DIM)
        q_ref[:, cols] = _l2_normalize(act[:, h * HEAD_DIM:(h + 1) * HEAD_DIM]) * (HEAD_DIM ** -0.5)
        k_ref[:, cols] = _l2_normalize(act[:, G_W + h * HEAD_DIM:G_W + (h + 1) * HEAD_DIM])
    v_ref[...] = act[:, 2 * G_W:]
    bg = _beta_and_log_decay(bgr_ref[...], na_ref[...], dt_ref[...])
    r = lax.broadcasted_iota(jnp.int32, (rows, rows), 0)
    s = lax.broadcasted_iota(jnp.int32, (rows, rows), 1)
    shift = DN_CHUNK.bit_length() - 1
    chunk_prefix = ((r >= s) & (lax.shift_right_logical(r, shift) == lax.shift_right_logical(s, shift)))
    lane = lax.broadcasted_iota(jnp.int32, bg.shape, 1)
    bg = jnp.where(lane < HEADS, bg, _hdot(chunk_prefix.astype(F32), bg))
    bg_ref[...] = bg
    per_tile = BG_LANES // DN_CHUNK
    for t in range(rows // BG_LANES):
        tile_t = bg[t * BG_LANES:(t + 1) * BG_LANES, :].T
        for c in range(per_tile):
            bgt_ref[t * per_tile + c] = tile_t[0:2 * HEADS, c * DN_CHUNK:(c + 1) * DN_CHUNK]


def _dn_prep(proj, bg_raw, conv_w, neg_a_row, dt_row, *, seq, rows):
    n = proj.shape[0]
    halo_blocks = rows // CONV_HALO
    qkv_w = 3 * G_W
    col = COL_DN_QKV * G_W // qkv_w
    whole = lambda a: pl.BlockSpec(a.shape, lambda g: (0,) * a.ndim)
    out = jax.ShapeDtypeStruct((n, G_W), F32)
    o_spec = pl.BlockSpec((rows, G_W), lambda g: (g, 0))
    return pl.pallas_call(
        functools.partial(_dn_prep_kernel, tiles_per_seq=seq // rows, rows=rows),
        out_shape=(out, out, out, jax.ShapeDtypeStruct((n, BG_LANES), F32),
                   jax.ShapeDtypeStruct((n // DN_CHUNK, 2 * HEADS, DN_CHUNK), F32)),
        grid=(n // rows,),
        in_specs=[pl.BlockSpec((rows, qkv_w), lambda g: (g, col)),
                  pl.BlockSpec((CONV_HALO, qkv_w), lambda g: (jnp.maximum(g * halo_blocks - 1, 0), col)),
                  pl.BlockSpec((rows, BG_LANES), lambda g: (g, 0)),
                  whole(conv_w), whole(neg_a_row), whole(dt_row)],
        out_specs=(o_spec, o_spec, o_spec, pl.BlockSpec((rows, BG_LANES), lambda g: (g, 0)),
                   pl.BlockSpec((rows // DN_CHUNK, 2 * HEADS, DN_CHUNK), lambda g: (g, 0, 0))),
        scratch_shapes=[pltpu.VMEM((CONV_HALO + rows, qkv_w), F32)],
        compiler_params=_params("parallel"),
        name="dn_prep",
    )(proj, proj, bg_raw, conv_w, neg_a_row, dt_row)


def _cat_lhs(x, axis):
    hi = x.astype(BF16).astype(F32)
    return jnp.concatenate([hi, x - hi, hi], axis=axis).astype(BF16)


def _cat_rhs(x, axis):
    hi = x.astype(BF16)
    lo = (x - hi.astype(F32)).astype(BF16)
    return jnp.concatenate([hi, hi, lo], axis=axis)


def _mm(lhs_cat, rhs_cat, dims=_B_NN):
    return lax.dot_general(lhs_cat, rhs_cat, dims, preferred_element_type=F32)


def _unit_lower_inverse(a_mat, eye):
    n = a_mat.shape[-1]
    inv = eye - a_mat
    power = _mm(_cat_lhs(a_mat, 2), _cat_rhs(a_mat, 1))
    span = 2
    while span < n:
        p_rhs = _cat_rhs(power, 1)
        inv = inv + _mm(_cat_lhs(inv, 2), p_rhs)
        span *= 2
        if span < n:
            power = _mm(_cat_lhs(power, 2), p_rhs)
    return inv


def _dn_out(o, z, norm_g):
    o = o * lax.rsqrt(jnp.mean(o * o, -1, keepdims=True) + NORM_EPS) * norm_g
    return o * _silu(z)


def _dn_scan_kernel(q_ref, k_ref, v_ref, z_ref, bg_ref, bgt_ref, ng_ref, y_ref, s_out_ref, s_ref, *, chunks):
    t = pl.program_id(1)

    @pl.when(t == 0)
    def _():
        s_ref[...] = jnp.zeros_like(s_ref)

    c, d = DN_CHUNK, HEAD_DIM
    units = [(ci, h) for ci in range(chunks) for h in range(HEADS)]
    rows = lambda ci: slice(ci * c, (ci + 1) * c)
    cols = lambda h: slice(h * d, (h + 1) * d)
    gather = lambda ref: jnp.stack([ref[rows(ci), cols(h)] for ci, h in units])
    q, k, v = gather(q_ref), gather(k_ref), gather(v_ref)
    beta = jnp.stack([bg_ref[rows(ci), h:h + 1] for ci, h in units])
    g_col = jnp.stack([bg_ref[rows(ci), HEADS + h:HEADS + h + 1] for ci, h in units])
    g_row = jnp.stack([bgt_ref[ci, HEADS + h:HEADS + h + 1, :] for ci, h in units])

    r = lax.broadcasted_iota(jnp.int32, (c, c), 0)
    s_ = lax.broadcasted_iota(jnp.int32, (c, c), 1)
    incl, strict, eye = (r >= s_)[None], (r > s_)[None], (r == s_).astype(F32)[None]
    decay = jnp.where(incl, jnp.exp(jnp.where(incl, g_col - g_row, 0.0)), 0.0)
    qk_hi, qk_lo = _split_bf16(jnp.concatenate([q, k], axis=1))
    k_hi, k_lo = _split_bf16(k)
    qk_kk = (_mm(jnp.concatenate([qk_hi, qk_lo], axis=2), jnp.concatenate([k_hi, k_hi], axis=2), _B_NT)
             + _mm(qk_hi, k_lo, _B_NT))
    qk = qk_kk[:, :c] * decay
    a_mat = jnp.where(strict, beta * qk_kk[:, c:] * decay, 0.0)
    inv = _unit_lower_inverse(a_mat, eye)
    e_g = jnp.exp(g_col)
    rhs = jnp.concatenate([beta * v, (beta * e_g) * k], axis=2)
    sol = _mm(_cat_lhs(inv, 2), _cat_rhs(rhs, 1))
    u_base, w_mat = sol[:, :, :d], sol[:, :, d:]
    w_hi, w_lo = _split_bf16(w_mat)
    w_cat = jnp.concatenate([w_hi, w_lo], axis=2)
    q_dec = (q * e_g).astype(BF16)
    g_last = g_col[:, c - 1:c, :]
    k_dec = k * jnp.exp(g_last - g_col)
    keep = jnp.exp(g_last)

    norm_g = ng_ref[...]
    s = s_ref[...]
    for ci in range(chunks):
        sl = slice(ci * HEADS, (ci + 1) * HEADS)
        s_hi, s_lo = _split_bf16(s)
        u = u_base[sl] - (_mm(w_cat[sl], jnp.concatenate([s_hi, s_hi], axis=1)) + _mm(w_hi[sl], s_lo))
        o = _mm(q_dec[sl], s_hi) + _mm(qk[sl].astype(BF16), u.astype(BF16))
        s = s * keep[sl] + _mm(_cat_lhs(k_dec[sl], 1), _cat_rhs(u, 1), _B_TN)
        for h in range(HEADS):
            y_ref[rows(ci), cols(h)] = _dn_out(o[h], z_ref[rows(ci), cols(h)], norm_g)
    s_ref[...] = s

    @pl.when(t == pl.num_programs(1) - 1)
    def _():
        s_out_ref[0] = s_ref[...]


def _dn_scan(q, k, v, proj, bg, bgt, norm_g, *, batch, seq, chunks):
    n = q.shape[0]
    rows = chunks * DN_CHUNK
    steps = seq // rows
    spec = pl.BlockSpec((rows, G_W), lambda b, t: (b * steps + t, 0))
    ng = norm_g.reshape(1, HEAD_DIM)
    return pl.pallas_call(
        functools.partial(_dn_scan_kernel, chunks=chunks),
        out_shape=(jax.ShapeDtypeStruct((n, G_W), F32),
                   jax.ShapeDtypeStruct((batch, HEADS, HEAD_DIM, HEAD_DIM), F32)),
        grid=(batch, steps),
        in_specs=[spec, spec, spec,
                  pl.BlockSpec((rows, G_W), lambda b, t: (b * steps + t, COL_DN_Z)),
                  pl.BlockSpec((rows, BG_LANES), lambda b, t: (b * steps + t, 0)),
                  pl.BlockSpec((chunks, 2 * HEADS, DN_CHUNK), lambda b, t: (b * steps + t, 0, 0)),
                  pl.BlockSpec((1, HEAD_DIM), lambda b, t: (0, 0))],
        out_specs=(spec, pl.BlockSpec((1, HEADS, HEAD_DIM, HEAD_DIM), lambda b, t: (b, 0, 0, 0))),
        scratch_shapes=[pltpu.VMEM((HEADS, HEAD_DIM, HEAD_DIM), F32)],
        compiler_params=_params("parallel", "arbitrary"),
        name="dn_scan",
    )(q, k, v, proj, bg, bgt, ng)


def _sample_mix_kernel(pool_ref, u_ref, v_ref, qkv_ref, z_ref, bgr_ref, sp_ref, sc_ref, sd_ref,
                       pw_ref, ps_ref, sw0_ref, sb0_ref, cw_ref, na_ref, dt_ref, ng_ref, *rest, bt, pos0):
    yp_ref, ys_ref, yd_ref, np_ref, nc_ref, nd_ref = rest[-6:]
    new = pool_ref[...]
    for gi, w in enumerate(POOL_WINDOWS):
        cols = slice(gi * HEAD_DIM, (gi + 1) * HEAD_DIM)
        s = new[:, cols]
        for j in range(1, w):
            s = s + sp_ref[POOL_BUF - j, :, cols]
        d = s / float(min(pos0 + 1, w)) - new[:, cols]
        yp_ref[:, cols] = _hdot(d, pw_ref[gi]) * ps_ref[:, cols]
    np_ref[0:POOL_BUF - 1] = sp_ref[1:POOL_BUF]
    np_ref[POOL_BUF - 1] = new

    ys_ref[...] = u_ref[...] * (sw0_ref[...] * v_ref[...] + sb0_ref[...])

    raw = qkv_ref[...]
    conv = raw * cw_ref[CONV_W - 1:CONV_W, :]
    for j in range(CONV_W - 1):
        conv = conv + sc_ref[j] * cw_ref[j:j + 1, :]
    nc_ref[0:CONV_W - 2] = sc_ref[1:CONV_W - 1]
    nc_ref[CONV_W - 2] = raw
    act = _silu(conv)
    bg = _beta_and_log_decay(bgr_ref[...], na_ref[...], dt_ref[...])
    norm_g = ng_ref[...]
    d = HEAD_DIM
    pad = jnp.zeros((d - bt, d), F32)
    k_t, q_t = [], []
    for h in range(HEADS):
        k_t.append(jnp.concatenate([_l2_normalize(act[:, G_W + h * d:G_W + (h + 1) * d]), pad], axis=0).T)
        q_t.append(jnp.concatenate([_l2_normalize(act[:, h * d:(h + 1) * d]) * (d ** -0.5), pad], axis=0).T)
    pairs = [(b, h) for b in range(bt) for h in range(HEADS)]
    v_h = [act[:, 2 * G_W + h * d:2 * G_W + (h + 1) * d] for h in range(HEADS)]
    z_h = [z_ref[:, h * d:(h + 1) * d] for h in range(HEADS)]
    a = [jnp.exp(bg[b:b + 1, HEADS + h:HEADS + h + 1]) for b, h in pairs]
    k_col = [jnp.broadcast_to(k_t[h][:, b:b + 1], (d, d)) for b, h in pairs]
    k_s = [jnp.sum(k_col[i] * sd_ref[b, h], axis=0, keepdims=True) for i, (b, h) in enumerate(pairs)]
    u = [bg[b:b + 1, h:h + 1] * (v_h[h][b:b + 1, :] - a[i] * k_s[i]) for i, (b, h) in enumerate(pairs)]
    s1 = [a[i] * sd_ref[b, h] + k_col[i] * u[i] for i, (b, h) in enumerate(pairs)]
    for i, (b, h) in enumerate(pairs):
        nd_ref[b, h] = s1[i]
    o = [jnp.sum(jnp.broadcast_to(q_t[h][:, b:b + 1], (d, d)) * s1[i], axis=0, keepdims=True)
         for i, (b, h) in enumerate(pairs)]
    for i, (b, h) in enumerate(pairs):
        yd_ref[b:b + 1, h * d:(h + 1) * d] = _dn_out(o[i], z_h[h][b:b + 1, :], norm_g)


def _sample_mix(proj, bg_raw, state_pool, state_conv, state_delta, new_prev, pool_w, pool_scale, sgu_w0, sgu_b0,
                conv_w, neg_a_row, dt_row, norm_g, *, layer, bt, pos0):
    bs = proj.shape[0]
    qkv_w = 3 * G_W
    col = lambda c: pl.BlockSpec((bt, G_W), lambda i: (i, c))
    whole = lambda a: pl.BlockSpec(a.shape, lambda i: (0,) * a.ndim)
    row = pl.BlockSpec((bt, G_W), lambda i: (i, 0))
    ps = pool_scale.reshape(1, G_W)
    ng = norm_g.reshape(1, HEAD_DIM)
    consts = (pool_w, ps, sgu_w0, sgu_b0, conv_w, neg_a_row, dt_row, ng)
    out_sds = lambda a: jax.ShapeDtypeStruct(a.shape, F32)
    pool_spec = pl.BlockSpec((None, POOL_BUF, bt, G_W), lambda i: (layer, 0, i, 0))
    conv_spec = pl.BlockSpec((None, CONV_W - 1, bt, qkv_w), lambda i: (layer, 0, i, 0))
    delta_spec = pl.BlockSpec((None, bt, HEADS, HEAD_DIM, HEAD_DIM), lambda i: (layer, i, 0, 0, 0))
    y = jax.ShapeDtypeStruct((bs, G_W), F32)
    prev = () if new_prev is None else tuple(new_prev)
    n_in = 9 + len(consts)
    return pl.pallas_call(
        functools.partial(_sample_mix_kernel, bt=bt, pos0=pos0),
        out_shape=(y, y, y, out_sds(state_pool), out_sds(state_conv), out_sds(state_delta)),
        grid=(bs // bt,),
        in_specs=[col(COL_POOL), col(COL_SGU_U), col(COL_SGU_V),
                  pl.BlockSpec((bt, qkv_w), lambda i: (i, COL_DN_QKV * G_W // qkv_w)),
                  col(COL_DN_Z),
                  pl.BlockSpec((bt, BG_LANES), lambda i: (i, 0)),
                  pool_spec, conv_spec, delta_spec] + [whole(a) for a in consts]
                 + [pl.BlockSpec(memory_space=pl.ANY)] * len(prev),
        out_specs=(row, row, row, pool_spec, conv_spec, delta_spec),
        input_output_aliases={n_in + k: 3 + k for k in range(len(prev))},
        compiler_params=_params("parallel"),
        name="sample_mix",
    )(proj, proj, proj, proj, proj, bg_raw, state_pool, state_conv, state_delta, *consts, *prev)


def _lane_row(values, offset):
    return jnp.zeros((1, BG_LANES), F32).at[0, offset:offset + values.shape[0]].set(values)


def kernel(x_prompt, x_sample, cache_k, cache_v, page_table, state_pool, state_conv, state_delta,
           ln_g, ln_b, w_ffn1_in, w_ffn1_out, w_ffn2_in, w_ffn2_out, w_in, w_out,
           pool_w, pool_scale, sgu_w, sgu_b, sb_bias, dn_conv_w, dn_a_log, dn_dt_bias, dn_norm_g):
    depth = ln_g.shape[0]
    alpha = (2.0 * depth) ** 0.25
    bp, seq, d_model = x_prompt.shape
    bs, dec_seq, _ = x_sample.shape
    assert dec_seq == 1
    n_phys, page = cache_k.shape[1], cache_k.shape[2]
    n_pages = page_table.shape[1]
    pos0 = n_pages * page
    ck = cache_k.reshape(depth * n_phys * page * HEADS, HEAD_DIM)
    cv = cache_v.reshape(depth * n_phys * page * HEADS, HEAD_DIM)
    hp = x_prompt.reshape(bp * seq, d_model)
    hs = x_sample.reshape(bs, d_model)
    pool_tm = jnp.swapaxes(state_pool, 1, 2)
    conv_tm = jnp.swapaxes(state_conv, 1, 2)
    tm_p = 512
    tm_s = bs
    tail_w = w_in.shape[2] - N_MAIN_COLS
    w_in_tail = jnp.pad(w_in[:, :, N_MAIN_COLS:].astype(BF16), ((0, 0), (0, 0), (0, BG_LANES - tail_w)))
    w_in_t = jnp.swapaxes(w_in, 1, 2)
    w_out_bf = w_out.astype(BF16)

    outs = {name: [] for name in ("poolp", "convp", "sp", "ks", "vs", "sgus")}
    kv_rows = None
    new_states = None
    for l in range(depth):
        neg_a_row = _lane_row(-jnp.exp(dn_a_log[l]), HEADS)
        dt_row = _lane_row(dn_dt_bias[l], HEADS)
        ffn = functools.partial(_ffn_ln, ln_g=ln_g, ln_b=ln_b, layer=l, alpha=alpha, tf=512)
        out_proj = functools.partial(_out_proj_ln, w_out=w_out_bf, ln_g=ln_g, ln_b=ln_b, layer=l, alpha=alpha)

        g1, *w1 = ffn(hs, (w_ffn1_in, w_ffn1_out), which=0, tm=tm_s)
        sproj, sbg_raw, sk_rows, sv_rows, wi_main = _in_proj_cast(g1, w_in_t, w_in_tail, layer=l)
        sgu_w0 = jnp.repeat(sgu_w[l, :, 0, 0], HEAD_DIM).reshape(1, G_W)
        sgu_b0 = jnp.repeat(sgu_b[l, :, 0], HEAD_DIM).reshape(1, G_W)
        sy_pool, sy_sgu, sy_dn, *new_states = _sample_mix(
            sproj, sbg_raw, pool_tm, conv_tm, state_delta, new_states, pool_w[l], pool_scale[l],
            sgu_w0, sgu_b0, dn_conv_w[l], neg_a_row, dt_row, dn_norm_g[l], layer=l, bt=16, pos0=pos0)
        bias_row = jnp.tile(sb_bias[l], page).reshape(1, page * HEADS)
        phys = (page_table + l * n_phys).reshape(-1)
        h1, sy_sb = _ffn_ln_decode(hp, tuple(w1), ln_g, ln_b, sproj.reshape(bs, 1, N_MAIN_COLS), bias_row, ck, cv, phys,
                                   layer=l, which=0, alpha=alpha, tm=tm_p, tf=512,
                                   n_pages=n_pages, width=page * HEADS)
        g2 = out_proj(g1, (sy_pool, sy_sgu, sy_sb, sy_dn), tm=tm_s)
        hs, *w2 = ffn(g2, (w_ffn2_in, w_ffn2_out), which=2, tm=tm_s)
        outs["ks"].append(sk_rows.reshape(bs, 1, HEADS, HEAD_DIM))
        outs["vs"].append(sv_rows.reshape(bs, 1, HEADS, HEAD_DIM))
        outs["sgus"].append(sproj[:, COL_SGU_V * G_W:(COL_SGU_V + 1) * G_W].reshape(bs, 1, G_W))

        proj, bg_raw, *kv_rows = _in_proj(h1, wi_main, w_in_tail, kv_rows, layer=l, depth=depth, tm=256)
        y_pool, y_sgu = _pool_sgu(proj, pool_w[l], pool_scale[l], sgu_w[l], sgu_b[l], seq=seq, rows=512)
        y_sb = _sb_attn(proj, sb_bias[l], batch=bp, seq=seq, blk=256)
        dq, dk, dv, bg, bgt = _dn_prep(proj, bg_raw, dn_conv_w[l], neg_a_row, dt_row, seq=seq, rows=256)
        y_dn, s_end = _dn_scan(dq, dk, dv, proj, bg, bgt, dn_norm_g[l], batch=bp, seq=seq, chunks=4)
        h2 = out_proj(h1, (y_pool, y_sgu, y_sb, y_dn), tm=256)
        hp, = ffn(h2, tuple(w2), which=2, tm=tm_p)
        proj3 = proj.reshape(bp, seq, N_MAIN_COLS)
        outs["poolp"].append(proj3[:, seq - POOL_BUF:, :G_W])
        outs["convp"].append(proj3[:, seq - (CONV_W - 1):, COL_DN_QKV * G_W:(COL_DN_QKV + 3) * G_W])
        outs["sp"].append(s_end)

    st = lambda name: jnp.stack(outs[name])
    new_pool, new_conv, new_delta = new_states
    kp, vp = (a.reshape(depth, bp, seq, HEADS, HEAD_DIM) for a in kv_rows)
    return (hp.reshape(bp, seq, d_model), hs.reshape(bs, 1, d_model),
            kp, vp, st("poolp"), st("convp"), st("sp"),
            st("ks"), st("vs"), jnp.swapaxes(new_pool, 1, 2), jnp.swapaxes(new_conv, 1, 2),
            new_delta, st("sgus"))
```
